```python
import jax, jax.numpy as jnp
from jax import lax
import numpy as np

D_MODEL = 1024
BATCH = 16
SEQ = 2048
DEPTH = 1

HG_HEADS = 4
HG_KEY_DIM = 128
HG_VAL_DIM = 128
HG_CHUNK = 64
HG_QK_W = HG_HEADS * HG_KEY_DIM
HG_V_W = HG_HEADS * HG_VAL_DIM
ATT_Q_HEADS = 8
ATT_KV_HEADS = 2
ATT_HEAD_DIM = 64
ATT_GROUP = ATT_Q_HEADS // ATT_KV_HEADS
ATT_Q_W = ATT_Q_HEADS * ATT_HEAD_DIM
ATT_KV_W = ATT_KV_HEADS * ATT_HEAD_DIM
WINDOW = 128
ATT_BLOCK = 128
ROPE_THETA = 500000.0
ROPE_DIM = ATT_HEAD_DIM // 4
N_EXPERTS = 32
TOP_K = 4
D_EXPERT = D_MODEL
SWIGLU_ALPHA = 1.702
SWIGLU_LIMIT = 7.0
MOE_BLOCK = 128
EPS = 1e-6
IN_WIDTHS = (HG_QK_W, HG_QK_W, HG_V_W, HG_V_W, ATT_Q_W, ATT_KV_W, ATT_KV_W, D_MODEL, D_MODEL)
IN_WIDTH = HG_QK_W * 2 + HG_V_W * 2 + ATT_Q_W + ATT_KV_W * 2 + D_MODEL * 2

kernel_name = 'hybrid_hgrn2_swa_sink_moe_adaln'


def _split_points():
    pts, acc = [], 0
    for w in IN_WIDTHS[:-1]:
        acc += w
        pts.append(acc)
    return pts


def rms_norm(x, gain):
    xf = x.astype(jnp.float32)
    y = xf * lax.rsqrt(jnp.mean(xf * xf, axis=-1, keepdims=True) + EPS)
    return (y * gain.astype(jnp.float32)).astype(x.dtype)


def partial_rotary(x, positions):
    half = ROPE_DIM // 2
    inv_freq = ROPE_THETA ** (-jnp.arange(half, dtype=jnp.float32) / half)
    ang = positions.astype(jnp.float32)[:, None, :, None] * inv_freq
    cos, sin = jnp.cos(ang), jnp.sin(ang)
    xr = x[..., :ROPE_DIM].astype(jnp.float32)
    x1, x2 = xr[..., :half], xr[..., half:]
    rot = jnp.concatenate([x1 * cos - x2 * sin, x2 * cos + x1 * sin], axis=-1).astype(x.dtype)
    return jnp.concatenate([rot, x[..., ROPE_DIM:]], axis=-1)


def hgrn2_branch(q_pre, f_pre, i_pre, g_pre, lower_bound, out_norm_gain):
    B, S, _ = q_pre.shape
    f32 = jnp.float32

    def heads(t, d):
        return t.reshape(B, S, HG_HEADS, d).transpose(0, 2, 1, 3)

    forget = lower_bound + (1.0 - lower_bound) * jax.nn.sigmoid(f_pre.astype(f32))
    q = heads(jax.nn.silu(q_pre.astype(f32)), HG_KEY_DIM)
    k = heads(1.0 - forget, HG_KEY_DIM)
    log_f = heads(jnp.log(forget), HG_KEY_DIM)
    v = heads(i_pre.astype(f32), HG_VAL_DIM)
    n_chunks = S // HG_CHUNK

    def chunks(t):
        return t.reshape(B, HG_HEADS, n_chunks, HG_CHUNK, t.shape[-1]).transpose(2, 0, 1, 3, 4)

    causal = jnp.tril(jnp.ones((HG_CHUNK, HG_CHUNK), dtype=bool))[:, :, None]

    def step(state, inp):
        qc, kc, vc, lfc = inp
        b = jnp.cumsum(lfc, axis=-2)
        o_inter = jnp.einsum('bhck,bhkv->bhcv', qc * jnp.exp(b), state)
        rel = b[:, :, :, None, :] - b[:, :, None, :, :]
        decay = jnp.exp(jnp.where(causal, rel, -jnp.inf))
        scores = jnp.einsum('bhik,bhjk,bhijk->bhij', qc, kc, decay)
        o = o_inter + jnp.einsum('bhij,bhjv->bhiv', scores, vc)
        b_last = b[:, :, -1:, :]
        new_state = jnp.exp(b_last[:, :, 0, :])[..., None] * state + jnp.einsum(
            'bhjk,bhjv->bhkv', kc * jnp.exp(b_last - b), vc)
        return new_state, o

    state0 = jnp.zeros((B, HG_HEADS, HG_KEY_DIM, HG_VAL_DIM), f32)
    _, o = lax.scan(step, state0, (chunks(q), chunks(k), chunks(v), chunks(log_f)))
    o = o.transpose(1, 2, 0, 3, 4).reshape(B, HG_HEADS, S, HG_VAL_DIM).transpose(0, 2, 1, 3)
    o = rms_norm(o, out_norm_gain).reshape(B, S, HG_V_W) * jax.nn.silu(g_pre.astype(f32))
    return o.astype(q_pre.dtype)


def swa_branch(q_pre, k_pre, v_pre, positions, q_gain, k_gain, sinks):
    B, S, _ = q_pre.shape
    nb = S // ATT_BLOCK
    f32 = jnp.float32
    q = q_pre.reshape(B, S, ATT_Q_HEADS, ATT_HEAD_DIM).transpose(0, 2, 1, 3)
    k = k_pre.reshape(B, S, ATT_KV_HEADS, ATT_HEAD_DIM).transpose(0, 2, 1, 3)
    v = v_pre.reshape(B, S, ATT_KV_HEADS, ATT_HEAD_DIM).transpose(0, 2, 1, 3)
    q = partial_rotary(rms_norm(q, q_gain), positions)
    k = partial_rotary(rms_norm(k, k_gain), positions)
    q = q.reshape(B, ATT_KV_HEADS, ATT_GROUP, nb, ATT_BLOCK, ATT_HEAD_DIM)

    def banded(t):
        tb = t.reshape(B, ATT_KV_HEADS, nb, ATT_BLOCK, ATT_HEAD_DIM)
        prev = jnp.pad(tb, ((0, 0), (0, 0), (1, 0), (0, 0), (0, 0)))[:, :, :-1]
        return jnp.concatenate([prev, tb], axis=3)

    kb, vb = banded(k), banded(v)
    scores = jnp.einsum('bhgnqd,bhnkd->bhgnqk', q, kb).astype(f32) * (ATT_HEAD_DIM ** -0.5)
    blk = jnp.arange(nb)[:, None] * ATT_BLOCK
    q_pos = blk + jnp.arange(ATT_BLOCK)[None, :]
    k_pos = blk - ATT_BLOCK + jnp.arange(2 * ATT_BLOCK)[None, :]
    dist = q_pos[:, :, None] - k_pos[:, None, :]
    mask = (dist >= 0) & (dist < WINDOW) & (k_pos[:, None, :] >= 0)
    scores = jnp.where(mask, scores, -jnp.inf)
    sink = sinks.astype(f32).reshape(1, ATT_KV_HEADS, ATT_GROUP, 1, 1, 1)
    logits = jnp.concatenate([scores, jnp.broadcast_to(sink, scores.shape[:-1] + (1,))], axis=-1)
    probs = jax.nn.softmax(logits, axis=-1)[..., :-1]
    out = jnp.einsum('bhgnqk,bhnkd->bhgnqd', probs.astype(vb.dtype), vb)
    out = out.reshape(B, ATT_Q_HEADS, S, ATT_HEAD_DIM).transpose(0, 2, 1, 3)
    return out.reshape(B, S, ATT_Q_W)


def moe_ffn(h, w_router, b_router, w_up, b_up, w_down, b_down):
    B, S, D = h.shape
    T = B * S
    A = T * TOP_K
    xf = h.reshape(T, D)
    logits = (xf @ w_router + b_router).astype(jnp.float32)
    top_vals, top_idx = lax.top_k(logits, TOP_K)
    gates = jax.nn.softmax(top_vals, axis=-1)
    flat_e = top_idx.reshape(A)
    flat_tok = (jnp.arange(A, dtype=jnp.int32) // TOP_K).astype(jnp.int32)
    flat_w = gates.reshape(A)
    order = jnp.argsort(flat_e)
    sorted_e = flat_e[order]
    counts = jnp.bincount(flat_e, length=N_EXPERTS)
    padded = (counts + MOE_BLOCK - 1) // MOE_BLOCK * MOE_BLOCK
    start = jnp.cumsum(counts) - counts
    pad_end = jnp.cumsum(padded)
    pad_start = pad_end - padded
    dest = pad_start[sorted_e] + jnp.arange(A) - start[sorted_e]
    P = A + N_EXPERTS * MOE_BLOCK
    n_blocks = P // MOE_BLOCK
    tok_buf = jnp.full((P,), T, jnp.int32).at[dest].set(flat_tok[order])
    w_buf = jnp.zeros((P,), jnp.float32).at[dest].set(flat_w[order])
    block_expert = jnp.minimum(
        jnp.searchsorted(pad_end, jnp.arange(n_blocks) * MOE_BLOCK, side='right'), N_EXPERTS - 1)
    x_pad = jnp.concatenate([xf, jnp.zeros((1, D), xf.dtype)], axis=0)
    x_buf = x_pad[tok_buf].reshape(n_blocks, MOE_BLOCK, D)

    def expert_block(args):
        xb, e = args
        hu = xb @ w_up[e] + b_up[e]
        x_glu = jnp.minimum(hu[:, ::2], SWIGLU_LIMIT)
        x_lin = jnp.clip(hu[:, 1::2], -SWIGLU_LIMIT, SWIGLU_LIMIT)
        act = x_glu * jax.nn.sigmoid(SWIGLU_ALPHA * x_glu) * (x_lin + 1.0)
        return act @ w_down[e] + b_down[e]

    y_buf = lax.map(expert_block, (x_buf, block_expert)).reshape(P, D)
    y = jax.ops.segment_sum(y_buf * w_buf[:, None].astype(y_buf.dtype), tok_buf, num_segments=T + 1)[:T]
    return y.reshape(B, S, D)


def setup_inputs(seed: int = 0) -> dict:
    key = jax.random.key(seed)
    ks = jax.random.split(key, 24)
    D = D_MODEL
    f32 = jnp.float32

    def dense(k, shape, fan_in, scale=1.0):
        return jax.random.normal(k, shape, f32) * (scale * fan_in ** -0.5)

    def gain(k, shape):
        return 1.0 + 0.02 * jax.random.normal(k, shape, f32)

    x = jax.random.normal(ks[0], (BATCH, SEQ, D), f32)
    c = jax.random.normal(ks[1], (BATCH, D), f32)
    positions = (jax.random.randint(ks[2], (BATCH, 1), 0, 4096, dtype=jnp.int32)
                 + jnp.arange(SEQ, dtype=jnp.int32)[None, :]).astype(jnp.int32)
    return {
        'x': x,
        'c': c,
        'positions': positions,
        'w_ada': dense(ks[3], (DEPTH, D, 6 * D), D, 0.5),
        'b_ada': 0.02 * jax.random.normal(ks[4], (DEPTH, 6 * D), f32),
        'norm1_gain': gain(ks[5], (DEPTH, D)),
        'w_in': dense(ks[6], (DEPTH, D, IN_WIDTH), D),
        'lower_bound_logits': 0.1 * jax.random.normal(ks[7], (DEPTH + 1, HG_QK_W), f32),
        'hg_norm_gain': gain(ks[8], (DEPTH, HG_VAL_DIM)),
        'w_hg_branch': dense(ks[9], (DEPTH, HG_V_W, D), HG_V_W),
        'q_norm_gain': gain(ks[10], (DEPTH, ATT_HEAD_DIM)),
        'k_norm_gain': gain(ks[11], (DEPTH, ATT_HEAD_DIM)),
        'attn_sinks': jax.random.normal(ks[12], (DEPTH, ATT_Q_HEADS), f32),
        'w_attn_branch': dense(ks[13], (DEPTH, ATT_Q_W, D), ATT_Q_W),
        'w_out': dense(ks[14], (DEPTH, D, D), D),
        'norm2_gain': gain(ks[15], (DEPTH, D)),
        'w_router': dense(ks[16], (DEPTH, D, N_EXPERTS), D),
        'b_router': 0.01 * jax.random.normal(ks[17], (DEPTH, N_EXPERTS), f32),
        'w_up': dense(ks[18], (DEPTH, N_EXPERTS, D, 2 * D_EXPERT), D),
        'b_up': 0.01 * jax.random.normal(ks[19], (DEPTH, N_EXPERTS, 2 * D_EXPERT), f32),
        'w_down': dense(ks[20], (DEPTH, N_EXPERTS, D_EXPERT, D), D_EXPERT),
        'b_down': 0.01 * jax.random.normal(ks[21], (DEPTH, N_EXPERTS, D), f32),
    }


def reference(x, c, positions, w_ada, b_ada, norm1_gain, w_in, lower_bound_logits, hg_norm_gain,
              w_hg_branch, q_norm_gain, k_norm_gain, attn_sinks, w_attn_branch, w_out, norm2_gain,
              w_router, b_router, w_up, b_up, w_down, b_down):
    lower_bounds = jnp.cumsum(jax.nn.softmax(lower_bound_logits.astype(jnp.float32), axis=0), axis=0)
    cond = jax.nn.silu(c)
    split_pts = _split_points()
    for l in range(DEPTH):
        mod = cond @ w_ada[l] + b_ada[l]
        sh1, sc1, g1, sh2, sc2, g2 = jnp.split(mod[:, None, :], 6, axis=-1)
        h = rms_norm(x, norm1_gain[l]) * (1.0 + sc1) + sh1
        proj = h @ w_in[l]
        hq, hf, hi, hg, aq, ak, av, gate_h, gate_a = jnp.split(proj, split_pts, axis=-1)
        y_h = hgrn2_branch(hq, hf, hi, hg, lower_bounds[l], hg_norm_gain[l]) @ w_hg_branch[l]
        y_a = swa_branch(aq, ak, av, positions, q_norm_gain[l], k_norm_gain[l], attn_sinks[l]) @ w_attn_branch[l]
        merged = jax.nn.sigmoid(gate_h) * y_h + jax.nn.sigmoid(gate_a) * y_a
        x = x + g1 * (merged @ w_out[l])
        h2 = rms_norm(x, norm2_gain[l]) * (1.0 + sc2) + sh2
        x = x + g2 * moe_ffn(h2, w_router[l], b_router[l], w_up[l], b_up[l], w_down[l], b_down[l])
    return x
```

```python
import functools

import numpy as np
import jax
import jax.numpy as jnp
from jax import lax
from jax.experimental import pallas as pl
from jax.experimental.pallas import tpu as pltpu

F32 = jnp.float32
BF16 = jnp.bfloat16
I32 = jnp.int32

HG_HEADS = 4
HG_DIM = 128
HG_W = HG_HEADS * HG_DIM
ATT_Q_HEADS = 8
ATT_KV_HEADS = 2
ATT_GROUP = ATT_Q_HEADS // ATT_KV_HEADS
ATT_DH = 64
ATT_Q_W = ATT_Q_HEADS * ATT_DH
ATT_KV_W = ATT_KV_HEADS * ATT_DH
ATT_BLOCK = 128
ROPE_THETA = 500000.0
ROPE_DIM = ATT_DH // 4
N_EXPERTS = 32
TOP_K = 4
SWIGLU_ALPHA = 1.702
SWIGLU_LIMIT = 7.0
EPS = 1e-6

V7X_LANES = 128
V7X_SUBLANES = 8
V7X_VMEM_LIMIT_BYTES = 56 * 1024 * 1024

TM_PROJ = 256
HG_TILE = 512
HG_CHUNK = 64
HG_SUB = 8
MOE_BM = 256
TM_DISPATCH = 256
TM_COMBINE = 128

NEG_INF = float("-inf")


def _cparams(sem, vmem=V7X_VMEM_LIMIT_BYTES):
    return pltpu.CompilerParams(dimension_semantics=sem, vmem_limit_bytes=vmem)


def _sigmoid(x):
    return 1.0 / (1.0 + jnp.exp(-x))


def _dot(a, b):
    return jnp.dot(a, b, preferred_element_type=F32)


def _dot_nt(a, b):
    return lax.dot_general(a, b, (((1,), (1,)), ((), ())), preferred_element_type=F32)


def _dot_tn(a, b):
    return lax.dot_general(a, b, (((0,), (0,)), ((), ())), preferred_element_type=F32)


def _dot_exact01(m01, x):
    hi = x.astype(BF16)
    r1 = x - hi.astype(F32)
    mid = r1.astype(BF16)
    lo = (r1 - mid.astype(F32)).astype(BF16)
    return _dot(m01, hi) + _dot(m01, mid) + _dot(m01, lo)


def _ada_kernel(c_ref, w_ref, b_ref, o_ref):
    c = c_ref[...]
    cond = c * _sigmoid(c)
    o_ref[...] = _dot(cond.astype(BF16), w_ref[...].astype(BF16)) + b_ref[...]


def _ada(c, w_ada, b_ada):
    B, D = c.shape
    N = w_ada.shape[1]
    tn = D
    return pl.pallas_call(
        _ada_kernel,
        grid=(N // tn,),
        in_specs=[pl.BlockSpec((B, D), lambda j: (0, 0)),
                  pl.BlockSpec((D, tn), lambda j: (0, j)),
                  pl.BlockSpec((1, tn), lambda j: (0, j))],
        out_specs=pl.BlockSpec((B, tn), lambda j: (0, j)),
        out_shape=jax.ShapeDtypeStruct((B, N), F32),
        compiler_params=_cparams(("parallel",)),
        name="ada",
    )(c, w_ada, b_ada.reshape(1, N))


def _inproj_kernel(x_ref, mod_ref, g_ref, w_ref, hg_ref, at_ref, gt_ref):
    x = x_ref[...]
    ms = jnp.mean(x * x, axis=-1, keepdims=True)
    y = x * lax.rsqrt(ms + EPS) * g_ref[...]
    h = (y * (1.0 + mod_ref[1:2, :]) + mod_ref[0:1, :]).astype(BF16)
    n_hg = hg_ref.shape[1]
    n_at = at_ref.shape[1]
    n_gt = gt_ref.shape[1]
    hg_ref[...] = _dot(h, w_ref[:, 0:n_hg])
    at_ref[...] = _dot(h, w_ref[:, n_hg:n_hg + n_at])
    gt_ref[...] = _dot(h, w_ref[:, n_hg + n_at:n_hg + n_at + n_gt])


def _inproj(x2, mod3, gain, w_in_bf, S):
    T, D = x2.shape
    tm = TM_PROJ
    n_hg = 4 * HG_W
    n_at = ATT_Q_W + 2 * ATT_KV_W
    n_gt = 2 * D
    assert w_in_bf.shape[1] == n_hg + n_at + n_gt
    per_b = S // tm
    return pl.pallas_call(
        _inproj_kernel,
        grid=(T // tm,),
        in_specs=[pl.BlockSpec((tm, D), lambda i: (i, 0)),
                  pl.BlockSpec((None, 6, D), lambda i: (i // per_b, 0, 0)),
                  pl.BlockSpec((1, D), lambda i: (0, 0)),
                  pl.BlockSpec(w_in_bf.shape, lambda i: (0, 0))],
        out_specs=[pl.BlockSpec((tm, n_hg), lambda i: (i, 0)),
                   pl.BlockSpec((tm, n_at), lambda i: (i, 0)),
                   pl.BlockSpec((tm, n_gt), lambda i: (i, 0))],
        out_shape=[jax.ShapeDtypeStruct((T, n_hg), F32),
                   jax.ShapeDtypeStruct((T, n_at), F32),
                   jax.ShapeDtypeStruct((T, n_gt), F32)],
        compiler_params=_cparams(("parallel",)),
        name="inproj",
    )(x2, mod3, gain.reshape(1, D), w_in_bf)


def _hgrn_constants():
    C = HG_CHUNK
    t = np.arange(C)[None, :]
    i = np.arange(C)[:, None]
    mats = [(t <= i), (t > i)]
    halves = []
    lh = C // 2
    while lh >= HG_SUB:
        halves.append(lh)
        lh //= 2
    code = np.zeros((C, C), np.int32)
    ii, jj = np.meshgrid(np.arange(C), np.arange(C), indexing="ij")
    for li, lh in enumerate(halves):
        ref = (i // (2 * lh)) * (2 * lh) + lh - 1
        second = (i % (2 * lh)) >= lh
        m = np.where(second, (t > ref) & (t <= i), (t > i) & (t <= ref))
        mats.append(m)
        sel = ((ii // (2 * lh)) == (jj // (2 * lh))) & ((ii % (2 * lh)) >= lh) & ((jj % (2 * lh)) < lh)
        code[sel] = li + 1
    m_all = np.concatenate(mats, axis=0).astype(np.float32)
    return m_all, code, len(halves)


def _hgrn_kernel(in_ref, lbl_ref, gain_ref, mall_ref, code_ref, o_ref, st_ref, *, n_levels):
    C = HG_CHUNK
    W = HG_W
    n_chunks = in_ref.shape[0] // C

    @pl.when(pl.program_id(1) == 0)
    def _():
        st_ref[...] = jnp.zeros_like(st_ref)

    l0 = lbl_ref[0:1, :]
    l1 = lbl_ref[1:2, :]
    mx = jnp.maximum(l0, l1)
    e0 = jnp.exp(l0 - mx)
    lb = e0 / (e0 + jnp.exp(l1 - mx))
    gain = gain_ref[...]
    mall = mall_ref[...].astype(BF16)
    code = code_ref[...]
    row = lax.broadcasted_iota(I32, (C, C), 0)
    col = lax.broadcasted_iota(I32, (C, C), 1)
    row_in_sub = lax.broadcasted_iota(I32, (C, HG_DIM), 0) % HG_SUB
    sub_base = (row // HG_SUB) * HG_SUB

    def bcast_sub(a, j):
        a3 = a.reshape(C // HG_SUB, HG_SUB, HG_DIM)
        return jnp.broadcast_to(a3[:, j:j + 1, :], a3.shape).reshape(C, HG_DIM)

    def chunk(c, carry):
        r0 = pl.multiple_of(c * C, C)
        rows = pl.ds(r0, C)
        qp = in_ref[rows, 0:W]
        fp = in_ref[rows, W:2 * W]
        vv = in_ref[rows, 2 * W:3 * W]
        gp = in_ref[rows, 3 * W:4 * W]
        forget = lb + (1.0 - lb) * _sigmoid(fp)
        q = qp * _sigmoid(qp)
        k = 1.0 - forget
        lf = jnp.log(forget)
        expo = _dot_exact01(mall, lf)
        e_all = jnp.exp(expo)
        outs = []
        for h in range(HG_HEADS):
            ls = slice(h * HG_DIM, (h + 1) * HG_DIM)
            qh, kh, vh = q[:, ls], k[:, ls], vv[:, ls]
            bh = expo[0:C, ls]
            st = st_ref[h]
            o = _dot_nt((qh * e_all[0:C, ls]).astype(BF16), st.astype(BF16))
            scores = jnp.zeros((C, C), F32)
            for j in range(HG_SUB):
                d = jnp.where(row_in_sub >= j, bh - bcast_sub(bh, j), NEG_INF)
                p = qh * bcast_sub(kh, j) * jnp.exp(d)
                s = jnp.sum(p, axis=-1, keepdims=True)
                scores = jnp.where(col == sub_base + j, s, scores)
            for li in range(n_levels):
                e = e_all[(2 + li) * C:(3 + li) * C, ls]
                s = _dot_nt((qh * e).astype(BF16), (kh * e).astype(BF16))
                scores = jnp.where(code == li + 1, s, scores)
            o = o + _dot(scores.astype(BF16), vh.astype(BF16))
            kst = (kh * e_all[C:2 * C, ls]).astype(BF16)
            st_ref[h] = st * e_all[C - 1:C, ls] + _dot_tn(vh.astype(BF16), kst)
            ms = jnp.mean(o * o, axis=-1, keepdims=True)
            on = o * lax.rsqrt(ms + EPS) * gain
            gh = gp[:, ls]
            outs.append(on * (gh * _sigmoid(gh)))
        o_ref[rows, :] = jnp.concatenate(outs, axis=-1).astype(o_ref.dtype)
        return carry

    lax.fori_loop(0, n_chunks, chunk, 0)


def _hgrn(hg_in, lb_logits, gain, B, S):
    T = hg_in.shape[0]
    lt = HG_TILE
    m_all, code, n_levels = _hgrn_constants()
    per_b = S // lt
    return pl.pallas_call(
        functools.partial(_hgrn_kernel, n_levels=n_levels),
        grid=(B, per_b),
        in_specs=[pl.BlockSpec((lt, 4 * HG_W), lambda b, s: (b * per_b + s, 0)),
                  pl.BlockSpec(lb_logits.shape, lambda b, s: (0, 0)),
                  pl.BlockSpec((1, HG_DIM), lambda b, s: (0, 0)),
                  pl.BlockSpec(m_all.shape, lambda b, s: (0, 0)),
                  pl.BlockSpec(code.shape, lambda b, s: (0, 0))],
        out_specs=pl.BlockSpec((lt, HG_W), lambda b, s: (b * per_b + s, 0)),
        out_shape=jax.ShapeDtypeStruct((T, HG_W), BF16),
        scratch_shapes=[pltpu.VMEM((HG_HEADS, HG_DIM, HG_DIM), F32)],
        compiler_params=_cparams(("parallel", "arbitrary")),
        name="hgrn2",
    )(hg_in, lb_logits, gain.reshape(1, HG_DIM), jnp.asarray(m_all), jnp.asarray(code))


def _swa_constants():
    lane = np.arange(V7X_LANES)
    c = lane % ATT_DH
    half = ROPE_DIM // 2
    inv_freq = ROPE_THETA ** (-jnp.arange(half, dtype=F32) / half)
    freq = jnp.where(c < ROPE_DIM, inv_freq[c % half], 0.0).astype(F32)
    sign = np.where(c < half, -1.0, np.where(c < ROPE_DIM, 1.0, 0.0)).astype(np.float32)
    first = (c < half).astype(np.float32)
    tab = jnp.stack([freq, jnp.asarray(sign), jnp.asarray(first)], axis=0)
    g = (lane[:, None] // ATT_DH == lane[None, :] // ATT_DH).astype(np.float32) / ATT_DH
    return tab, g


def _swa_kernel(sink_ref, cur_ref, prev_ref, pcur_ref, pprev_ref, qg_ref, kg_ref, tab_ref, g_ref, o_ref):
    Bq = ATT_BLOCK
    n = pl.program_id(1)
    tab = tab_ref[...]
    freq, sign, first = tab[0:1, :], tab[1:2, :], tab[2:3, :]
    gmat = g_ref[...].astype(BF16)
    half = ROPE_DIM // 2

    def group_ms(x):
        sq = x * x
        hi = sq.astype(BF16)
        lo = (sq - hi.astype(F32)).astype(BF16)
        return _dot(hi, gmat) + _dot(lo, gmat)

    def norm_rope(x, gain, pos):
        y = x * lax.rsqrt(group_ms(x) + EPS) * gain
        ang = pos * freq
        cs = jnp.cos(ang)
        sn = jnp.sin(ang) * sign
        partner = jnp.where(first > 0.5, pltpu.roll(y, V7X_LANES - half, 1), pltpu.roll(y, half, 1))
        return y * cs + partner * sn

    pos_c = pcur_ref[...].astype(F32)
    pos_p = pprev_ref[...].astype(F32)
    kq = ATT_Q_W
    k_cur = norm_rope(cur_ref[:, kq:kq + ATT_KV_W], kg_ref[...], pos_c)
    k_prev = norm_rope(prev_ref[:, 0:ATT_KV_W], kg_ref[...], pos_p)
    v_cur = cur_ref[:, kq + ATT_KV_W:kq + 2 * ATT_KV_W]
    v_prev = prev_ref[:, ATT_KV_W:2 * ATT_KV_W]
    kcat = jnp.concatenate([k_prev, k_cur], axis=0)
    vcat = jnp.concatenate([v_prev, v_cur], axis=0)
    lane = lax.broadcasted_iota(I32, (2 * Bq, V7X_LANES), 1)
    lo_half = lane < ATT_DH
    kroll = pltpu.roll(kcat, ATT_DH, 1)
    vroll = pltpu.roll(vcat, ATT_DH, 1)
    zero = jnp.zeros_like(kcat)
    kpad = [[jnp.where(lo_half, kcat, zero).astype(BF16), jnp.where(lo_half, zero, kroll).astype(BF16)],
            [jnp.where(lo_half, kroll, zero).astype(BF16), jnp.where(lo_half, zero, kcat).astype(BF16)]]
    vpad = [[jnp.where(lo_half, vcat, zero).astype(BF16), jnp.where(lo_half, zero, vroll).astype(BF16)],
            [jnp.where(lo_half, vroll, zero).astype(BF16), jnp.where(lo_half, zero, vcat).astype(BF16)]]
    qi = lax.broadcasted_iota(I32, (Bq, 2 * Bq), 0)
    kj = lax.broadcasted_iota(I32, (Bq, 2 * Bq), 1)
    has_prev = (jnp.zeros((Bq, 2 * Bq), I32) + n) > 0
    mask = ((kj < Bq) & (kj > qi) & has_prev) | ((kj >= Bq) & ((kj - Bq) <= qi))
    scale = ATT_DH ** -0.5
    for t in range(ATT_Q_W // V7X_LANES):
        ls = slice(t * V7X_LANES, (t + 1) * V7X_LANES)
        qt = norm_rope(cur_ref[:, ls], qg_ref[...], pos_c).astype(BF16)
        acc = jnp.zeros((Bq, V7X_LANES), F32)
        for u in range(2):
            head = 2 * t + u
            kvh = head // ATT_GROUP
            s = _dot_nt(qt, kpad[kvh][u]) * scale
            s = jnp.where(mask, s, NEG_INF)
            sink = sink_ref[head]
            m = jnp.maximum(jnp.max(s, axis=-1, keepdims=True), sink)
            p = jnp.exp(s - m)
            denom = jnp.sum(p, axis=-1, keepdims=True) + jnp.exp(sink - m)
            probs = p / denom
            acc = acc + _dot(probs.astype(BF16), vpad[kvh][u])
        o_ref[:, ls] = acc.astype(o_ref.dtype)


def _swa(at_in, pos3, q_gain, k_gain, sinks, B, S):
    T = at_in.shape[0]
    nb = S // ATT_BLOCK
    tab, g = _swa_constants()
    qg = jnp.tile(q_gain.reshape(1, ATT_DH), (1, V7X_LANES // ATT_DH))
    kg = jnp.tile(k_gain.reshape(1, ATT_DH), (1, V7X_LANES // ATT_DH))
    n_at = at_in.shape[1]
    kv_blk = 2 * ATT_KV_W
    assert ATT_Q_W % kv_blk == 0
    return pl.pallas_call(
        _swa_kernel,
        grid=(B, nb),
        in_specs=[pl.BlockSpec(memory_space=pltpu.SMEM),
                  pl.BlockSpec((ATT_BLOCK, n_at), lambda b, n: (b * nb + n, 0)),
                  pl.BlockSpec((ATT_BLOCK, kv_blk), lambda b, n: (b * nb + jnp.maximum(n - 1, 0), ATT_Q_W // kv_blk)),
                  pl.BlockSpec((None, ATT_BLOCK, 1), lambda b, n: (b, n, 0)),
                  pl.BlockSpec((None, ATT_BLOCK, 1), lambda b, n: (b, jnp.maximum(n - 1, 0), 0)),
                  pl.BlockSpec((1, V7X_LANES), lambda b, n: (0, 0)),
                  pl.BlockSpec((1, V7X_LANES), lambda b, n: (0, 0)),
                  pl.BlockSpec(tab.shape, lambda b, n: (0, 0)),
                  pl.BlockSpec(g.shape, lambda b, n: (0, 0))],
        out_specs=pl.BlockSpec((ATT_BLOCK, ATT_Q_W), lambda b, n: (b * nb + n, 0)),
        out_shape=jax.ShapeDtypeStruct((T, ATT_Q_W), BF16),
        compiler_params=_cparams(("parallel", "parallel")),
        name="swa",
    )(sinks, at_in, at_in, pos3, pos3, qg, kg, jnp.asarray(tab), jnp.asarray(g))


def _merge_router_kernel(x_ref, hg_ref, at_ref, gt_ref, mod_ref, whg_ref, wat_ref, wout_ref, g2_ref,
                         wr_ref, br_ref, tri_ref,
                         x1_ref, h2_ref, idx_ref, gate_ref, rank_ref, cnt_ref, run_ref):
    i = pl.program_id(0)
    D = x_ref.shape[1]
    tm = x_ref.shape[0]

    @pl.when(i == 0)
    def _():
        run_ref[...] = jnp.zeros_like(run_ref)

    y_h = _dot(hg_ref[...], whg_ref[...])
    y_a = _dot(at_ref[...], wat_ref[...])
    merged = _sigmoid(gt_ref[:, 0:D]) * y_h + _sigmoid(gt_ref[:, D:2 * D]) * y_a
    x1 = x_ref[...] + mod_ref[2:3, :] * _dot(merged.astype(BF16), wout_ref[...])
    x1_ref[...] = x1
    ms = jnp.mean(x1 * x1, axis=-1, keepdims=True)
    h2 = x1 * lax.rsqrt(ms + EPS) * g2_ref[...] * (1.0 + mod_ref[4:5, :]) + mod_ref[3:4, :]
    h2_ref[...] = h2
    logits = _dot(h2.astype(BF16), wr_ref[...]) + br_ref[...]
    E = logits.shape[1]
    lane = lax.broadcasted_iota(I32, (tm, E), 1).astype(F32)
    vals, idxs = [], []
    l = logits
    for _ in range(TOP_K):
        m = jnp.max(l, axis=-1, keepdims=True)
        ik = jnp.min(jnp.where(l == m, lane, float(E)), axis=-1, keepdims=True)
        vals.append(m)
        idxs.append(ik)
        l = jnp.where(lane == ik, NEG_INF, l)
    ex = [jnp.exp(v - vals[0]) for v in vals]
    den = ex[0]
    for e in ex[1:]:
        den = den + e
    onehot = jnp.zeros((tm, E), F32)
    for ik in idxs:
        onehot = onehot + (lane == ik).astype(F32)
    cum = _dot(tri_ref[...], onehot.astype(BF16))
    run = run_ref[0:1, 0:E]
    excl = cum - onehot + run
    lane_k = lax.broadcasted_iota(I32, (tm, TOP_K), 1)
    idx_o = jnp.zeros((tm, TOP_K), I32)
    gate_o = jnp.zeros((tm, TOP_K), F32)
    rank_o = jnp.zeros((tm, TOP_K), I32)
    for kk in range(TOP_K):
        rk = jnp.sum(jnp.where(lane == idxs[kk], excl, 0.0), axis=-1, keepdims=True)
        idx_o = jnp.where(lane_k == kk, idxs[kk].astype(I32), idx_o)
        gate_o = jnp.where(lane_k == kk, ex[kk] / den, gate_o)
        rank_o = jnp.where(lane_k == kk, rk.astype(I32), rank_o)
    idx_ref[...] = idx_o
    gate_ref[...] = gate_o
    rank_ref[...] = rank_o
    new_run = run + cum[tm - 1:tm, :]
    run_ref[0:1, 0:E] = new_run
    cnt_ref[...] = jnp.broadcast_to(new_run, cnt_ref.shape)


def _merge_router(x2, hg_o, at_o, gates, mod3, whg, wat, wout, g2, wr, br, S):
    T, D = x2.shape
    tm = TM_PROJ
    per_b = S // tm
    E = wr.shape[1]
    tri = jnp.asarray(np.tril(np.ones((tm, tm), np.float32)), dtype=BF16)
    row = lambda i: (i, 0)
    const = lambda i: (0, 0)
    return pl.pallas_call(
        _merge_router_kernel,
        grid=(T // tm,),
        in_specs=[pl.BlockSpec((tm, D), row),
                  pl.BlockSpec((tm, HG_W), row),
                  pl.BlockSpec((tm, ATT_Q_W), row),
                  pl.BlockSpec((tm, 2 * D), row),
                  pl.BlockSpec((None, 6, D), lambda i: (i // per_b, 0, 0)),
                  pl.BlockSpec(whg.shape, const),
                  pl.BlockSpec(wat.shape, const),
                  pl.BlockSpec(wout.shape, const),
                  pl.BlockSpec((1, D), const),
                  pl.BlockSpec(wr.shape, const),
                  pl.BlockSpec((1, E), const),
                  pl.BlockSpec((tm, tm), const)],
        out_specs=[pl.BlockSpec((tm, D), row),
                   pl.BlockSpec((tm, D), row),
                   pl.BlockSpec((tm, TOP_K), row),
                   pl.BlockSpec((tm, TOP_K), row),
                   pl.BlockSpec((tm, TOP_K), row),
                   pl.BlockSpec((V7X_SUBLANES, E), const)],
        out_shape=[jax.ShapeDtypeStruct((T, D), F32),
                   jax.ShapeDtypeStruct((T, D), F32),
                   jax.ShapeDtypeStruct((T, TOP_K), I32),
                   jax.ShapeDtypeStruct((T, TOP_K), F32),
                   jax.ShapeDtypeStruct((T, TOP_K), I32),
                   jax.ShapeDtypeStruct((V7X_SUBLANES, E), F32)],
        scratch_shapes=[pltpu.VMEM((V7X_SUBLANES, V7X_LANES), F32)],
        compiler_params=_cparams(("arbitrary",)),
        name="merge_router",
    )(x2, hg_o, at_o, gates, mod3, whg, wat, wout, g2.reshape(1, D), wr, br.reshape(1, E), tri)


def _dest_kernel(idx_ref, rank_ref, ps_ref, o_ref):
    idx = idx_ref[...]
    tm = idx.shape[0]
    E = ps_ref.shape[1]
    lane = lax.broadcasted_iota(I32, (tm, E), 1)
    lane_k = lax.broadcasted_iota(I32, (tm, TOP_K), 1)
    ps = ps_ref[...].astype(F32)
    out = rank_ref[...]
    for kk in range(TOP_K):
        start = jnp.sum(jnp.where(lane == idx[:, kk:kk + 1], ps, 0.0), axis=-1, keepdims=True)
        out = out + jnp.where(lane_k == kk, start.astype(I32), 0)
    o_ref[...] = out


def _dest(idx, rank, pad_start):
    T = idx.shape[0]
    tm = 1024
    E = pad_start.shape[0]
    row = lambda i: (i, 0)
    return pl.pallas_call(
        _dest_kernel,
        grid=(T // tm,),
        in_specs=[pl.BlockSpec((tm, TOP_K), row), pl.BlockSpec((tm, TOP_K), row),
                  pl.BlockSpec((1, E), lambda i: (0, 0))],
        out_specs=pl.BlockSpec((tm, TOP_K), row),
        out_shape=jax.ShapeDtypeStruct((T, TOP_K), I32),
        compiler_params=_cparams(("parallel",)),
        name="dest",
    )(idx, rank, pad_start.reshape(1, E))


def _dispatch_kernel(dest_ref, h_ref, xz_ref, xbuf_ref, sem):
    del xz_ref
    tm = h_ref.shape[0]

    def issue(t, c):
        for kk in range(TOP_K):
            d = dest_ref[t * TOP_K + kk]
            pltpu.make_async_copy(h_ref.at[pl.ds(t, 1)], xbuf_ref.at[pl.ds(d, 1)], sem).start()
        return c

    lax.fori_loop(0, tm, issue, 0)

    def drain(t, c):
        for kk in range(TOP_K):
            pltpu.make_async_copy(h_ref.at[pl.ds(0, 1)], xbuf_ref.at[pl.ds(0, 1)], sem).wait()
        return c

    lax.fori_loop(0, tm, drain, 0)


def _dispatch(dest_flat, h2, P):
    T, D = h2.shape
    tm = TM_DISPATCH
    xz = jnp.zeros((P, D), h2.dtype)
    return pl.pallas_call(
        _dispatch_kernel,
        grid=(T // tm,),
        in_specs=[pl.BlockSpec((tm * TOP_K,), lambda i: (i,), memory_space=pltpu.SMEM),
                  pl.BlockSpec((tm, D), lambda i: (i, 0)),
                  pl.BlockSpec(memory_space=pl.ANY)],
        out_specs=pl.BlockSpec(memory_space=pl.ANY),
        out_shape=jax.ShapeDtypeStruct((P, D), h2.dtype),
        scratch_shapes=[pltpu.SemaphoreType.DMA],
        input_output_aliases={2: 0},
        compiler_params=_cparams(("arbitrary",)),
        name="dispatch",
    )(dest_flat, h2, xz)


def _ffn_kernel(be_ref, nu_ref, x_ref, wg_ref, wl_ref, bg_ref, bl_ref, wd_ref, bd_ref, y_ref):
    del be_ref

    @pl.when(pl.program_id(0) < nu_ref[0])
    def _():
        x = x_ref[...].astype(BF16)
        glu = _dot(x, wg_ref[...]) + bg_ref[...]
        lin = _dot(x, wl_ref[...]) + bl_ref[...]
        glu = jnp.minimum(glu, SWIGLU_LIMIT)
        lin = jnp.clip(lin, -SWIGLU_LIMIT, SWIGLU_LIMIT)
        act = glu * _sigmoid(SWIGLU_ALPHA * glu) * (lin + 1.0)
        y_ref[...] = _dot(act.astype(BF16), wd_ref[...]) + bd_ref[...]

    @pl.when(pl.program_id(0) >= nu_ref[0])
    def _():
        y_ref[...] = jnp.zeros_like(y_ref)


def _ffn(block_expert, n_used, xbuf, wg, wl, bg, bl, wd, bd):
    P, D = xbuf.shape
    bm = MOE_BM
    De = wg.shape[2]
    wmap = lambda i, be, nu: (be[i], 0, 0)
    grid_spec = pltpu.PrefetchScalarGridSpec(
        num_scalar_prefetch=2,
        grid=(P // bm,),
        in_specs=[pl.BlockSpec((bm, D), lambda i, be, nu: (i, 0)),
                  pl.BlockSpec((None, D, De), wmap),
                  pl.BlockSpec((None, D, De), wmap),
                  pl.BlockSpec((None, 1, De), wmap),
                  pl.BlockSpec((None, 1, De), wmap),
                  pl.BlockSpec((None, De, D), wmap),
                  pl.BlockSpec((None, 1, D), wmap)],
        out_specs=pl.BlockSpec((bm, D), lambda i, be, nu: (i, 0)),
    )
    return pl.pallas_call(
        _ffn_kernel,
        grid_spec=grid_spec,
        out_shape=jax.ShapeDtypeStruct((P, D), F32),
        compiler_params=_cparams(("arbitrary",)),
        name="expert_ffn",
    )(block_expert, n_used, xbuf, wg, wl, bg, bl, wd, bd)


def _combine_kernel(dcur_ref, dnext_ref, gate_ref, x1_ref, mod_ref, y_hbm, o_ref, buf, sem):
    i = pl.program_id(0)
    n = pl.num_programs(0)
    tm = x1_ref.shape[0]
    slot = i % 2

    def issue(dref, s):
        def body(t, c):
            for kk in range(TOP_K):
                d = dref[t * TOP_K + kk]
                pltpu.make_async_copy(y_hbm.at[pl.ds(d, 1)], buf.at[s, kk, pl.ds(t, 1)], sem.at[s]).start()
            return c
        lax.fori_loop(0, tm, body, 0)

    @pl.when(i == 0)
    def _():
        issue(dcur_ref, 0)

    @pl.when(i + 1 < n)
    def _():
        issue(dnext_ref, 1 - slot)

    def drain(t, c):
        for kk in range(TOP_K):
            pltpu.make_async_copy(y_hbm.at[pl.ds(0, 1)], buf.at[slot, 0, pl.ds(0, 1)], sem.at[slot]).wait()
        return c

    lax.fori_loop(0, tm, drain, 0)
    gate = gate_ref[...]
    acc = gate[:, 0:1] * buf[slot, 0]
    for kk in range(1, TOP_K):
        acc = acc + gate[:, kk:kk + 1] * buf[slot, kk]
    o_ref[...] = x1_ref[...] + mod_ref[5:6, :] * acc


def _combine(dest_flat, gate, x1, mod3, ybuf, S):
    T, D = x1.shape
    tm = TM_COMBINE
    per_b = S // tm
    nt = T // tm
    return pl.pallas_call(
        _combine_kernel,
        grid=(nt,),
        in_specs=[pl.BlockSpec((tm * TOP_K,), lambda i: (i,), memory_space=pltpu.SMEM),
                  pl.BlockSpec((tm * TOP_K,), lambda i: (jnp.minimum(i + 1, nt - 1),), memory_space=pltpu.SMEM),
                  pl.BlockSpec((tm, TOP_K), lambda i: (i, 0)),
                  pl.BlockSpec((tm, D), lambda i: (i, 0)),
                  pl.BlockSpec((None, 6, D), lambda i: (i // per_b, 0, 0)),
                  pl.BlockSpec(memory_space=pl.ANY)],
        out_specs=pl.BlockSpec((tm, D), lambda i: (i, 0)),
        out_shape=jax.ShapeDtypeStruct((T, D), F32),
        scratch_shapes=[pltpu.VMEM((2, TOP_K, tm, D), F32), pltpu.SemaphoreType.DMA((2,))],
        compiler_params=_cparams(("arbitrary",)),
        name="combine",
    )(dest_flat, dest_flat, gate, x1, mod3, ybuf)


def kernel(x, c, positions, w_ada, b_ada, norm1_gain, w_in, lower_bound_logits, hg_norm_gain, w_hg_branch,
           q_norm_gain, k_norm_gain, attn_sinks, w_attn_branch, w_out, norm2_gain, w_router, b_router,
           w_up, b_up, w_down, b_down):
    B, S, D = x.shape
    T = B * S
    assert w_ada.shape[0] == 1, "one layer"
    x2 = x.reshape(T, D)

    mod = _ada(c, w_ada[0], b_ada[0])
    mod3 = mod.reshape(B, 6, D)

    hg_in, at_in, gates = _inproj(x2, mod3, norm1_gain[0], w_in[0].astype(BF16), S)
    hg_o = _hgrn(hg_in, lower_bound_logits, hg_norm_gain[0], B, S)
    at_o = _swa(at_in, positions.reshape(B, S, 1), q_norm_gain[0], k_norm_gain[0], attn_sinks[0], B, S)

    x1, h2, idx, gate, rank, cnt = _merge_router(
        x2, hg_o, at_o, gates, mod3, w_hg_branch[0].astype(BF16), w_attn_branch[0].astype(BF16),
        w_out[0].astype(BF16), norm2_gain[0], w_router[0].astype(BF16), b_router[0], S)

    bm = MOE_BM
    counts = cnt[0].astype(I32)
    padded = (counts + bm - 1) // bm * bm
    pad_end = jnp.cumsum(padded)
    pad_start = pad_end - padded
    P = T * TOP_K + N_EXPERTS * bm
    n_blocks = P // bm
    block_expert = jnp.minimum(
        jnp.searchsorted(pad_end, jnp.arange(n_blocks, dtype=I32) * bm, side="right"), N_EXPERTS - 1).astype(I32)
    n_used = (pad_end[-1:] // bm).astype(I32)

    dest = _dest(idx, rank, pad_start)
    dest_flat = dest.reshape(T * TOP_K)
    xbuf = _dispatch(dest_flat, h2, P)

    wu = w_up[0]
    ybuf = _ffn(block_expert, n_used, xbuf,
                wu[:, :, 0::2].astype(BF16), wu[:, :, 1::2].astype(BF16),
                b_up[0][:, None, 0::2], b_up[0][:, None, 1::2],
                w_down[0].astype(BF16), b_down[0][:, None, :])

    out = _combine(dest_flat, gate, x1, mod3, ybuf, S)
    return out.reshape(B, S, D)
```

```python
import functools

import numpy as np
import jax
import jax.numpy as jnp
from jax import lax
from jax.experimental import pallas as pl
from jax.experimental.pallas import tpu as pltpu

F32 = jnp.float32
BF16 = jnp.bfloat16
I32 = jnp.int32

HG_HEADS = 4
HG_DIM = 128
HG_W = HG_HEADS * HG_DIM
ATT_Q_HEADS = 8
ATT_KV_HEADS = 2
ATT_GROUP = ATT_Q_HEADS // ATT_KV_HEADS
ATT_DH = 64
ATT_Q_W = ATT_Q_HEADS * ATT_DH
ATT_KV_W = ATT_KV_HEADS * ATT_DH
ATT_BLOCK = 128
ROPE_THETA = 500000.0
ROPE_DIM = ATT_DH // 4
N_EXPERTS = 32
TOP_K = 4
SWIGLU_ALPHA = 1.702
SWIGLU_LIMIT = 7.0
EPS = 1e-6

V7X_LANES = 128
V7X_SUBLANES = 8
V7X_VMEM_LIMIT_BYTES = 56 * 1024 * 1024

TM_PROJ = 256
HG_TILE = 512
HG_CHUNK = 64
HG_SUB = 8
MOE_BM = 256
TM_DISPATCH = 256
TM_COMBINE = 128

NEG_INF = float("-inf")


def _cparams(sem, vmem=V7X_VMEM_LIMIT_BYTES):
    return pltpu.CompilerParams(dimension_semantics=sem, vmem_limit_bytes=vmem)


def _sigmoid(x):
    return 1.0 / (1.0 + jnp.exp(-x))


def _dot(a, b):
    return jnp.dot(a, b, preferred_element_type=F32)


def _dot_nt(a, b):
    return lax.dot_general(a, b, (((1,), (1,)), ((), ())), preferred_element_type=F32)


def _dot_tn(a, b):
    return lax.dot_general(a, b, (((0,), (0,)), ((), ())), preferred_element_type=F32)


ROW_TILE = V7X_SUBLANES


def _store_row_tiles(ref, val):
    rows = val.shape[0]
    for g in range(ROW_TILE):
        ref[pl.ds(g, rows, stride=ROW_TILE), :] = val[:, g * V7X_LANES:(g + 1) * V7X_LANES]


def _load_row_tiles(ref, rows):
    return jnp.concatenate([ref[pl.ds(g, rows, stride=ROW_TILE), :] for g in range(ROW_TILE)], axis=1)


def _dot_exact01(m01, x):
    hi = x.astype(BF16)
    r1 = x - hi.astype(F32)
    mid = r1.astype(BF16)
    lo = (r1 - mid.astype(F32)).astype(BF16)
    return _dot(m01, hi) + _dot(m01, mid) + _dot(m01, lo)


def _ada_kernel(c_ref, w_ref, b_ref, o_ref):
    c = c_ref[...]
    cond = c * _sigmoid(c)
    o_ref[...] = _dot(cond.astype(BF16), w_ref[...].astype(BF16)) + b_ref[...]


def _ada(c, w_ada, b_ada):
    B, D = c.shape
    N = w_ada.shape[1]
    tn = D
    return pl.pallas_call(
        _ada_kernel,
        grid=(N // tn,),
        in_specs=[pl.BlockSpec((B, D), lambda j: (0, 0)),
                  pl.BlockSpec((D, tn), lambda j: (0, j)),
                  pl.BlockSpec((1, tn), lambda j: (0, j))],
        out_specs=pl.BlockSpec((B, tn), lambda j: (0, j)),
        out_shape=jax.ShapeDtypeStruct((B, N), F32),
        compiler_params=_cparams(("parallel",)),
        name="ada",
    )(c, w_ada, b_ada.reshape(1, N))


def _inproj_kernel(x_ref, mod_ref, g_ref, w_ref, hg_ref, at_ref, gt_ref):
    x = x_ref[...]
    ms = jnp.mean(x * x, axis=-1, keepdims=True)
    y = x * lax.rsqrt(ms + EPS) * g_ref[...]
    h = (y * (1.0 + mod_ref[1:2, :]) + mod_ref[0:1, :]).astype(BF16)
    n_hg = hg_ref.shape[1]
    n_at = at_ref.shape[1]
    n_gt = gt_ref.shape[1]
    hg_ref[...] = _dot(h, w_ref[:, 0:n_hg])
    at_ref[...] = _dot(h, w_ref[:, n_hg:n_hg + n_at])
    gt_ref[...] = _dot(h, w_ref[:, n_hg + n_at:n_hg + n_at + n_gt])


def _inproj(x2, mod3, gain, w_in_bf, S):
    T, D = x2.shape
    tm = TM_PROJ
    n_hg = 4 * HG_W
    n_at = ATT_Q_W + 2 * ATT_KV_W
    n_gt = 2 * D
    assert w_in_bf.shape[1] == n_hg + n_at + n_gt
    per_b = S // tm
    return pl.pallas_call(
        _inproj_kernel,
        grid=(T // tm,),
        in_specs=[pl.BlockSpec((tm, D), lambda i: (i, 0)),
                  pl.BlockSpec((None, 6, D), lambda i: (i // per_b, 0, 0)),
                  pl.BlockSpec((1, D), lambda i: (0, 0)),
                  pl.BlockSpec(w_in_bf.shape, lambda i: (0, 0))],
        out_specs=[pl.BlockSpec((tm, n_hg), lambda i: (i, 0)),
                   pl.BlockSpec((tm, n_at), lambda i: (i, 0)),
                   pl.BlockSpec((tm, n_gt), lambda i: (i, 0))],
        out_shape=[jax.ShapeDtypeStruct((T, n_hg), F32),
                   jax.ShapeDtypeStruct((T, n_at), F32),
                   jax.ShapeDtypeStruct((T, n_gt), F32)],
        compiler_params=_cparams(("parallel",)),
        name="inproj",
    )(x2, mod3, gain.reshape(1, D), w_in_bf)


def _hgrn_constants():
    C = HG_CHUNK
    t = np.arange(C)[None, :]
    i = np.arange(C)[:, None]
    mats = [(t <= i), (t > i)]
    halves = []
    lh = C // 2
    while lh >= HG_SUB:
        halves.append(lh)
        lh //= 2
    code = np.zeros((C, C), np.int32)
    ii, jj = np.meshgrid(np.arange(C), np.arange(C), indexing="ij")
    for li, lh in enumerate(halves):
        ref = (i // (2 * lh)) * (2 * lh) + lh - 1
        second = (i % (2 * lh)) >= lh
        m = np.where(second, (t > ref) & (t <= i), (t > i) & (t <= ref))
        mats.append(m)
        sel = ((ii // (2 * lh)) == (jj // (2 * lh))) & ((ii % (2 * lh)) >= lh) & ((jj % (2 * lh)) < lh)
        code[sel] = li + 1
    m_all = np.concatenate(mats, axis=0).astype(np.float32)
    return m_all, code, len(halves)


def _hgrn_kernel(in_ref, lbl_ref, gain_ref, mall_ref, code_ref, o_ref, st_ref, *, n_levels):
    C = HG_CHUNK
    W = HG_W
    n_chunks = in_ref.shape[0] // C

    @pl.when(pl.program_id(1) == 0)
    def _():
        st_ref[...] = jnp.zeros_like(st_ref)

    l0 = lbl_ref[0:1, :]
    l1 = lbl_ref[1:2, :]
    mx = jnp.maximum(l0, l1)
    e0 = jnp.exp(l0 - mx)
    lb = e0 / (e0 + jnp.exp(l1 - mx))
    gain = gain_ref[...]
    mall = mall_ref[...].astype(BF16)
    code = code_ref[...]
    row = lax.broadcasted_iota(I32, (C, C), 0)
    col = lax.broadcasted_iota(I32, (C, C), 1)
    row_in_sub = lax.broadcasted_iota(I32, (C, HG_DIM), 0) % HG_SUB
    sub_base = (row // HG_SUB) * HG_SUB

    def bcast_sub(a, j):
        a3 = a.reshape(C // HG_SUB, HG_SUB, HG_DIM)
        return jnp.broadcast_to(a3[:, j:j + 1, :], a3.shape).reshape(C, HG_DIM)

    def chunk(c, carry):
        r0 = pl.multiple_of(c * C, C)
        rows = pl.ds(r0, C)
        qp = in_ref[rows, 0:W]
        fp = in_ref[rows, W:2 * W]
        vv = in_ref[rows, 2 * W:3 * W]
        gp = in_ref[rows, 3 * W:4 * W]
        forget = lb + (1.0 - lb) * _sigmoid(fp)
        q = qp * _sigmoid(qp)
        k = 1.0 - forget
        lf = jnp.log(forget)
        expo = _dot_exact01(mall, lf)
        e_all = jnp.exp(expo)
        outs = []
        for h in range(HG_HEADS):
            ls = slice(h * HG_DIM, (h + 1) * HG_DIM)
            qh, kh, vh = q[:, ls], k[:, ls], vv[:, ls]
            bh = expo[0:C, ls]
            st = st_ref[h]
            o = _dot_nt((qh * e_all[0:C, ls]).astype(BF16), st.astype(BF16))
            scores = jnp.zeros((C, C), F32)
            for j in range(HG_SUB):
                d = jnp.where(row_in_sub >= j, bh - bcast_sub(bh, j), NEG_INF)
                p = qh * bcast_sub(kh, j) * jnp.exp(d)
                s = jnp.sum(p, axis=-1, keepdims=True)
                scores = jnp.where(col == sub_base + j, s, scores)
            for li in range(n_levels):
                e = e_all[(2 + li) * C:(3 + li) * C, ls]
                s = _dot_nt((qh * e).astype(BF16), (kh * e).astype(BF16))
                scores = jnp.where(code == li + 1, s, scores)
            o = o + _dot(scores.astype(BF16), vh.astype(BF16))
            kst = (kh * e_all[C:2 * C, ls]).astype(BF16)
            st_ref[h] = st * e_all[C - 1:C, ls] + _dot_tn(vh.astype(BF16), kst)
            ms = jnp.mean(o * o, axis=-1, keepdims=True)
            on = o * lax.rsqrt(ms + EPS) * gain
            gh = gp[:, ls]
            outs.append(on * (gh * _sigmoid(gh)))
        o_ref[rows, :] = jnp.concatenate(outs, axis=-1).astype(o_ref.dtype)
        return carry

    lax.fori_loop(0, n_chunks, chunk, 0, unroll=2)


def _hgrn(hg_in, lb_logits, gain, B, S):
    T = hg_in.shape[0]
    lt = HG_TILE
    m_all, code, n_levels = _hgrn_constants()
    per_b = S // lt
    return pl.pallas_call(
        functools.partial(_hgrn_kernel, n_levels=n_levels),
        grid=(B, per_b),
        in_specs=[pl.BlockSpec((lt, 4 * HG_W), lambda b, s: (b * per_b + s, 0)),
                  pl.BlockSpec(lb_logits.shape, lambda b, s: (0, 0)),
                  pl.BlockSpec((1, HG_DIM), lambda b, s: (0, 0)),
                  pl.BlockSpec(m_all.shape, lambda b, s: (0, 0)),
                  pl.BlockSpec(code.shape, lambda b, s: (0, 0))],
        out_specs=pl.BlockSpec((lt, HG_W), lambda b, s: (b * per_b + s, 0)),
        out_shape=jax.ShapeDtypeStruct((T, HG_W), BF16),
        scratch_shapes=[pltpu.VMEM((HG_HEADS, HG_DIM, HG_DIM), F32)],
        compiler_params=_cparams(("parallel", "arbitrary")),
        name="hgrn2",
    )(hg_in, lb_logits, gain.reshape(1, HG_DIM), jnp.asarray(m_all), jnp.asarray(code))


def _swa_constants():
    lane = np.arange(V7X_LANES)
    c = lane % ATT_DH
    half = ROPE_DIM // 2
    inv_freq = ROPE_THETA ** (-jnp.arange(half, dtype=F32) / half)
    freq = jnp.where(c < ROPE_DIM, inv_freq[c % half], 0.0).astype(F32)
    sign = np.where(c < half, -1.0, np.where(c < ROPE_DIM, 1.0, 0.0)).astype(np.float32)
    first = (c < half).astype(np.float32)
    tab = jnp.stack([freq, jnp.asarray(sign), jnp.asarray(first)], axis=0)
    g = (lane[:, None] // ATT_DH == lane[None, :] // ATT_DH).astype(np.float32) / ATT_DH
    return tab, g


def _swa_kernel(sink_ref, cur_ref, prev_ref, pcur_ref, pprev_ref, qg_ref, kg_ref, tab_ref, g_ref, o_ref):
    Bq = ATT_BLOCK
    n = pl.program_id(1)
    tab = tab_ref[...]
    freq, sign, first = tab[0:1, :], tab[1:2, :], tab[2:3, :]
    gmat = g_ref[...].astype(BF16)
    half = ROPE_DIM // 2

    def group_ms(x):
        sq = x * x
        hi = sq.astype(BF16)
        lo = (sq - hi.astype(F32)).astype(BF16)
        return _dot(hi, gmat) + _dot(lo, gmat)

    def rope_table(pos):
        ang = pos * freq
        return jnp.cos(ang), jnp.sin(ang) * sign

    def norm_rope(x, gain, cs_sn):
        y = x * lax.rsqrt(group_ms(x) + EPS) * gain
        partner = jnp.where(first > 0.5, pltpu.roll(y, V7X_LANES - half, 1), pltpu.roll(y, half, 1))
        return y * cs_sn[0] + partner * cs_sn[1]

    pos_c = rope_table(pcur_ref[...].astype(F32))
    pos_p = rope_table(pprev_ref[...].astype(F32))
    kq = ATT_Q_W
    k_cur = norm_rope(cur_ref[:, kq:kq + ATT_KV_W], kg_ref[...], pos_c)
    k_prev = norm_rope(prev_ref[:, 0:ATT_KV_W], kg_ref[...], pos_p)
    v_cur = cur_ref[:, kq + ATT_KV_W:kq + 2 * ATT_KV_W]
    v_prev = prev_ref[:, ATT_KV_W:2 * ATT_KV_W]
    kcat = jnp.concatenate([k_prev, k_cur], axis=0)
    vcat = jnp.concatenate([v_prev, v_cur], axis=0)
    lane = lax.broadcasted_iota(I32, (2 * Bq, V7X_LANES), 1)
    lo_half = lane < ATT_DH
    kroll = pltpu.roll(kcat, ATT_DH, 1)
    vroll = pltpu.roll(vcat, ATT_DH, 1)
    zero = jnp.zeros_like(kcat)
    kpad = [[jnp.where(lo_half, kcat, zero).astype(BF16), jnp.where(lo_half, zero, kroll).astype(BF16)],
            [jnp.where(lo_half, kroll, zero).astype(BF16), jnp.where(lo_half, zero, kcat).astype(BF16)]]
    vpad = [[jnp.where(lo_half, vcat, zero).astype(BF16), jnp.where(lo_half, zero, vroll).astype(BF16)],
            [jnp.where(lo_half, vroll, zero).astype(BF16), jnp.where(lo_half, zero, vcat).astype(BF16)]]
    qi = lax.broadcasted_iota(I32, (Bq, 2 * Bq), 0)
    kj = lax.broadcasted_iota(I32, (Bq, 2 * Bq), 1)
    has_prev = (jnp.zeros((Bq, 2 * Bq), I32) + n) > 0
    mask = ((kj < Bq) & (kj > qi) & has_prev) | ((kj >= Bq) & ((kj - Bq) <= qi))
    scale = ATT_DH ** -0.5
    for t in range(ATT_Q_W // V7X_LANES):
        ls = slice(t * V7X_LANES, (t + 1) * V7X_LANES)
        qt = norm_rope(cur_ref[:, ls], qg_ref[...], pos_c).astype(BF16)
        acc = jnp.zeros((Bq, V7X_LANES), F32)
        for u in range(2):
            head = 2 * t + u
            kvh = head // ATT_GROUP
            s = _dot_nt(qt, kpad[kvh][u]) * scale
            s = jnp.where(mask, s, NEG_INF)
            sink = sink_ref[head]
            m = jnp.maximum(jnp.max(s, axis=-1, keepdims=True), sink)
            p = jnp.exp(s - m)
            denom = jnp.sum(p, axis=-1, keepdims=True) + jnp.exp(sink - m)
            probs = p / denom
            acc = acc + _dot(probs.astype(BF16), vpad[kvh][u])
        o_ref[:, ls] = acc.astype(o_ref.dtype)


def _swa(at_in, pos3, q_gain, k_gain, sinks, B, S):
    T = at_in.shape[0]
    nb = S // ATT_BLOCK
    tab, g = _swa_constants()
    qg = jnp.tile(q_gain.reshape(1, ATT_DH), (1, V7X_LANES // ATT_DH))
    kg = jnp.tile(k_gain.reshape(1, ATT_DH), (1, V7X_LANES // ATT_DH))
    n_at = at_in.shape[1]
    kv_blk = 2 * ATT_KV_W
    assert ATT_Q_W % kv_blk == 0
    return pl.pallas_call(
        _swa_kernel,
        grid=(B, nb),
        in_specs=[pl.BlockSpec(memory_space=pltpu.SMEM),
                  pl.BlockSpec((ATT_BLOCK, n_at), lambda b, n: (b * nb + n, 0)),
                  pl.BlockSpec((ATT_BLOCK, kv_blk), lambda b, n: (b * nb + jnp.maximum(n - 1, 0), ATT_Q_W // kv_blk)),
                  pl.BlockSpec((None, ATT_BLOCK, 1), lambda b, n: (b, n, 0)),
                  pl.BlockSpec((None, ATT_BLOCK, 1), lambda b, n: (b, jnp.maximum(n - 1, 0), 0)),
                  pl.BlockSpec((1, V7X_LANES), lambda b, n: (0, 0)),
                  pl.BlockSpec((1, V7X_LANES), lambda b, n: (0, 0)),
                  pl.BlockSpec(tab.shape, lambda b, n: (0, 0)),
                  pl.BlockSpec(g.shape, lambda b, n: (0, 0))],
        out_specs=pl.BlockSpec((ATT_BLOCK, ATT_Q_W), lambda b, n: (b * nb + n, 0)),
        out_shape=jax.ShapeDtypeStruct((T, ATT_Q_W), BF16),
        compiler_params=_cparams(("parallel", "parallel")),
        name="swa",
    )(sinks, at_in, at_in, pos3, pos3, qg, kg, jnp.asarray(tab), jnp.asarray(g))


def _merge_router_kernel(x_ref, hg_ref, at_ref, gt_ref, mod_ref, whg_ref, wat_ref, wout_ref, g2_ref,
                         wr_ref, br_ref, tri_ref,
                         x1_ref, h2_ref, idx_ref, gate_ref, rank_ref, cnt_ref, run_ref):
    i = pl.program_id(0)
    D = x_ref.shape[1]
    tm = x_ref.shape[0]

    @pl.when(i == 0)
    def _():
        run_ref[...] = jnp.zeros_like(run_ref)

    y_h = _dot(hg_ref[...], whg_ref[...])
    y_a = _dot(at_ref[...], wat_ref[...])
    merged = _sigmoid(gt_ref[:, 0:D]) * y_h + _sigmoid(gt_ref[:, D:2 * D]) * y_a
    x1 = x_ref[...] + mod_ref[2:3, :] * _dot(merged.astype(BF16), wout_ref[...])
    x1_ref[...] = x1
    ms = jnp.mean(x1 * x1, axis=-1, keepdims=True)
    h2 = x1 * lax.rsqrt(ms + EPS) * g2_ref[...] * (1.0 + mod_ref[4:5, :]) + mod_ref[3:4, :]
    _store_row_tiles(h2_ref, h2)
    logits = _dot(h2.astype(BF16), wr_ref[...]) + br_ref[...]
    E = logits.shape[1]
    lane = lax.broadcasted_iota(I32, (tm, E), 1).astype(F32)
    vals, idxs = [], []
    l = logits
    for _ in range(TOP_K):
        m = jnp.max(l, axis=-1, keepdims=True)
        ik = jnp.min(jnp.where(l == m, lane, float(E)), axis=-1, keepdims=True)
        vals.append(m)
        idxs.append(ik)
        l = jnp.where(lane == ik, NEG_INF, l)
    ex = [jnp.exp(v - vals[0]) for v in vals]
    den = ex[0]
    for e in ex[1:]:
        den = den + e
    onehot = jnp.zeros((tm, E), F32)
    for ik in idxs:
        onehot = onehot + (lane == ik).astype(F32)
    cum = _dot(tri_ref[...], onehot.astype(BF16))
    run = run_ref[0:1, 0:E]
    excl = cum - onehot + run
    lane_k = lax.broadcasted_iota(I32, (tm, TOP_K), 1)
    idx_o = jnp.zeros((tm, TOP_K), I32)
    gate_o = jnp.zeros((tm, TOP_K), F32)
    rank_o = jnp.zeros((tm, TOP_K), I32)
    for kk in range(TOP_K):
        rk = jnp.sum(jnp.where(lane == idxs[kk], excl, 0.0), axis=-1, keepdims=True)
        idx_o = jnp.where(lane_k == kk, idxs[kk].astype(I32), idx_o)
        gate_o = jnp.where(lane_k == kk, ex[kk] / den, gate_o)
        rank_o = jnp.where(lane_k == kk, rk.astype(I32), rank_o)
    idx_ref[...] = idx_o
    gate_ref[...] = gate_o
    rank_ref[...] = rank_o
    new_run = run + cum[tm - 1:tm, :]
    run_ref[0:1, 0:E] = new_run
    cnt_ref[...] = jnp.broadcast_to(new_run, cnt_ref.shape)


def _merge_router(x2, hg_o, at_o, gates, mod3, whg, wat, wout, g2, wr, br, S):
    T, D = x2.shape
    tm = TM_PROJ
    per_b = S // tm
    E = wr.shape[1]
    tri = jnp.asarray(np.tril(np.ones((tm, tm), np.float32)), dtype=BF16)
    row = lambda i: (i, 0)
    const = lambda i: (0, 0)
    return pl.pallas_call(
        _merge_router_kernel,
        grid=(T // tm,),
        in_specs=[pl.BlockSpec((tm, D), row),
                  pl.BlockSpec((tm, HG_W), row),
                  pl.BlockSpec((tm, ATT_Q_W), row),
                  pl.BlockSpec((tm, 2 * D), row),
                  pl.BlockSpec((None, 6, D), lambda i: (i // per_b, 0, 0)),
                  pl.BlockSpec(whg.shape, const),
                  pl.BlockSpec(wat.shape, const),
                  pl.BlockSpec(wout.shape, const),
                  pl.BlockSpec((1, D), const),
                  pl.BlockSpec(wr.shape, const),
                  pl.BlockSpec((1, E), const),
                  pl.BlockSpec((tm, tm), const)],
        out_specs=[pl.BlockSpec((tm, D), row),
                   pl.BlockSpec((tm * ROW_TILE, V7X_LANES), row),
                   pl.BlockSpec((tm, TOP_K), row),
                   pl.BlockSpec((tm, TOP_K), row),
                   pl.BlockSpec((tm, TOP_K), row),
                   pl.BlockSpec((V7X_SUBLANES, E), const)],
        out_shape=[jax.ShapeDtypeStruct((T, D), F32),
                   jax.ShapeDtypeStruct((T * ROW_TILE, V7X_LANES), F32),
                   jax.ShapeDtypeStruct((T, TOP_K), I32),
                   jax.ShapeDtypeStruct((T, TOP_K), F32),
                   jax.ShapeDtypeStruct((T, TOP_K), I32),
                   jax.ShapeDtypeStruct((V7X_SUBLANES, E), F32)],
        scratch_shapes=[pltpu.VMEM((V7X_SUBLANES, V7X_LANES), F32)],
        compiler_params=_cparams(("arbitrary",)),
        name="merge_router",
    )(x2, hg_o, at_o, gates, mod3, whg, wat, wout, g2.reshape(1, D), wr, br.reshape(1, E), tri)


def _dest_kernel(idx_ref, rank_ref, ps_ref, o_ref):
    idx = idx_ref[...]
    tm = idx.shape[0]
    E = ps_ref.shape[1]
    lane = lax.broadcasted_iota(I32, (tm, E), 1)
    lane_k = lax.broadcasted_iota(I32, (tm, TOP_K), 1)
    ps = ps_ref[...].astype(F32)
    out = rank_ref[...]
    for kk in range(TOP_K):
        start = jnp.sum(jnp.where(lane == idx[:, kk:kk + 1], ps, 0.0), axis=-1, keepdims=True)
        out = out + jnp.where(lane_k == kk, start.astype(I32), 0)
    o_ref[...] = out


def _dest(idx, rank, pad_start):
    T = idx.shape[0]
    tm = 1024
    E = pad_start.shape[0]
    row = lambda i: (i, 0)
    return pl.pallas_call(
        _dest_kernel,
        grid=(T // tm,),
        in_specs=[pl.BlockSpec((tm, TOP_K), row), pl.BlockSpec((tm, TOP_K), row),
                  pl.BlockSpec((1, E), lambda i: (0, 0))],
        out_specs=pl.BlockSpec((tm, TOP_K), row),
        out_shape=jax.ShapeDtypeStruct((T, TOP_K), I32),
        compiler_params=_cparams(("parallel",)),
        name="dest",
    )(idx, rank, pad_start.reshape(1, E))


DISPATCH_UNROLL = 4


def _row(ref, r):
    return ref.at[pl.ds(pl.multiple_of(r * ROW_TILE, ROW_TILE), ROW_TILE)]


def _dispatch_kernel(fill_start_ref, fill_n_ref, tail_ref, dest_ref, h_ref, xbuf_ref, zero_ref, sem):
    tm = h_ref.shape[0] // ROW_TILE
    zrows = zero_ref.shape[0]

    def zero_row_copy(r):
        return pltpu.make_async_copy(_row(zero_ref, 0), _row(xbuf_ref, r), sem.at[1])

    def zero_block_copy(b):
        dst = xbuf_ref.at[pl.ds(pl.multiple_of(b * zrows, zrows), zrows)]
        return pltpu.make_async_copy(zero_ref, dst, sem.at[2])

    @pl.when(pl.program_id(0) == 0)
    def _():
        zero_ref[...] = jnp.zeros_like(zero_ref)

        def per_expert(start):
            def body(e, c):
                def rows(r, c2):
                    cp = zero_row_copy(fill_start_ref[e] + r)
                    cp.start() if start else cp.wait()
                    return c2
                return lax.fori_loop(0, fill_n_ref[e], rows, c)
            return body

        def tail(start):
            def body(b, c):
                cp = zero_block_copy(tail_ref[0] + b)
                cp.start() if start else cp.wait()
                return c
            return body

        lax.fori_loop(0, N_EXPERTS, per_expert(True), 0)
        lax.fori_loop(0, tail_ref[1], tail(True), 0)
        lax.fori_loop(0, N_EXPERTS, per_expert(False), 0)
        lax.fori_loop(0, tail_ref[1], tail(False), 0)

    def issue(tb, c):
        for u in range(DISPATCH_UNROLL):
            t = tb * DISPATCH_UNROLL + u
            src = _row(h_ref, t)
            for kk in range(TOP_K):
                d = dest_ref[t * TOP_K + kk]
                pltpu.make_async_copy(src, _row(xbuf_ref, d), sem.at[0]).start(priority=kk % 2)
        return c

    lax.fori_loop(0, tm // DISPATCH_UNROLL, issue, 0)

    def drain(tb, c):
        for _ in range(DISPATCH_UNROLL * TOP_K):
            pltpu.make_async_copy(_row(h_ref, 0), _row(xbuf_ref, 0), sem.at[0]).wait()
        return c

    lax.fori_loop(0, tm // DISPATCH_UNROLL, drain, 0)


def _dispatch(fill_start, fill_n, tail, dest_flat, h2t, P):
    T = h2t.shape[0] // ROW_TILE
    tm = TM_DISPATCH
    grid_spec = pltpu.PrefetchScalarGridSpec(
        num_scalar_prefetch=3,
        grid=(T // tm,),
        in_specs=[pl.BlockSpec((tm * TOP_K,), lambda i, *_: (i,), memory_space=pltpu.SMEM),
                  pl.BlockSpec((tm * ROW_TILE, V7X_LANES), lambda i, *_: (i, 0))],
        out_specs=pl.BlockSpec(memory_space=pl.ANY),
        scratch_shapes=[pltpu.VMEM((MOE_BM * ROW_TILE, V7X_LANES), F32), pltpu.SemaphoreType.DMA((3,))],
    )
    return pl.pallas_call(
        _dispatch_kernel,
        grid_spec=grid_spec,
        out_shape=jax.ShapeDtypeStruct((P * ROW_TILE, V7X_LANES), F32),
        compiler_params=_cparams(("arbitrary",)),
        name="dispatch",
    )(fill_start, fill_n, tail, dest_flat, h2t)


FFN_PREP_COLS = 256


def _ffn_kernel(be_ref, nu_ref, x_ref, wu_ref, bg_ref, bl_ref, wd_ref, bd_ref, y_ref,
                t_ref, wgt_ref, wlt_ref, wdb_ref):
    i = pl.program_id(0)
    bm = x_ref.shape[0] // ROW_TILE
    D, De2 = wu_ref.shape
    n_slab = D // V7X_LANES
    half = FFN_PREP_COLS // 2
    used = i < nu_ref[0]
    new_expert = (i == 0) | (be_ref[i] != be_ref[jnp.maximum(i - 1, 0)])

    @pl.when(used & new_expert)
    def _():
        for c in range(De2 // FFN_PREP_COLS):
            tt = wu_ref[:, c * FFN_PREP_COLS:(c + 1) * FFN_PREP_COLS].T
            for s in range(n_slab):
                t_ref[s] = tt[:, s * V7X_LANES:(s + 1) * V7X_LANES]
            for s in range(n_slab):
                ls = slice(s * V7X_LANES, (s + 1) * V7X_LANES)
                wgt_ref[c * half:(c + 1) * half, ls] = t_ref[s, pl.ds(0, half, stride=2), :].astype(BF16)
                wlt_ref[c * half:(c + 1) * half, ls] = t_ref[s, pl.ds(1, half, stride=2), :].astype(BF16)
        wdb_ref[...] = wd_ref[...].astype(BF16)

    @pl.when(used)
    def _():
        x = _load_row_tiles(x_ref, bm).astype(BF16)
        glu = _dot_nt(x, wgt_ref[...]) + bg_ref[...]
        lin = _dot_nt(x, wlt_ref[...]) + bl_ref[...]
        glu = jnp.minimum(glu, SWIGLU_LIMIT)
        lin = jnp.clip(lin, -SWIGLU_LIMIT, SWIGLU_LIMIT)
        act = glu * _sigmoid(SWIGLU_ALPHA * glu) * (lin + 1.0)
        _store_row_tiles(y_ref, _dot(act.astype(BF16), wdb_ref[...]) + bd_ref[...])

    @pl.when(jnp.logical_not(used))
    def _():
        y_ref[...] = jnp.zeros_like(y_ref)


def _ffn(block_expert, n_used, xbuf, w_up, bg, bl, w_down, bd):
    P = xbuf.shape[0] // ROW_TILE
    bm = MOE_BM
    _, D, De2 = w_up.shape
    De = De2 // 2
    wmap = lambda i, be, nu: (be[i], 0, 0)
    rows = lambda i, be, nu: (i, 0)
    grid_spec = pltpu.PrefetchScalarGridSpec(
        num_scalar_prefetch=2,
        grid=(P // bm,),
        in_specs=[pl.BlockSpec((bm * ROW_TILE, V7X_LANES), rows),
                  pl.BlockSpec((None, D, De2), wmap),
                  pl.BlockSpec((None, 1, De), wmap),
                  pl.BlockSpec((None, 1, De), wmap),
                  pl.BlockSpec((None, De, D), wmap),
                  pl.BlockSpec((None, 1, D), wmap)],
        out_specs=pl.BlockSpec((bm * ROW_TILE, V7X_LANES), rows),
        scratch_shapes=[pltpu.VMEM((D // V7X_LANES, FFN_PREP_COLS, V7X_LANES), F32),
                        pltpu.VMEM((De, D), BF16),
                        pltpu.VMEM((De, D), BF16),
                        pltpu.VMEM((De, D), BF16)],
    )
    return pl.pallas_call(
        _ffn_kernel,
        grid_spec=grid_spec,
        out_shape=jax.ShapeDtypeStruct((P * ROW_TILE, V7X_LANES), F32),
        compiler_params=_cparams(("arbitrary",)),
        name="expert_ffn",
    )(block_expert, n_used, xbuf, w_up, bg, bl, w_down, bd)


def _combine_kernel(dcur_ref, dnext_ref, gate_ref, x1_ref, mod_ref, y_hbm, o_ref, buf, sem):
    i = pl.program_id(0)
    n = pl.num_programs(0)
    tm = x1_ref.shape[0]
    slot = i % 2

    def issue(dref, s):
        def body(tb, c):
            for u in range(DISPATCH_UNROLL):
                t = tb * DISPATCH_UNROLL + u
                for kk in range(TOP_K):
                    d = dref[t * TOP_K + kk]
                    pltpu.make_async_copy(_row(y_hbm, d), _row(buf.at[s, kk], t), sem.at[s]).start(priority=kk % 2)
            return c
        lax.fori_loop(0, tm // DISPATCH_UNROLL, body, 0)

    @pl.when(i == 0)
    def _():
        issue(dcur_ref, 0)

    @pl.when(i + 1 < n)
    def _():
        issue(dnext_ref, 1 - slot)

    def drain(tb, c):
        for _ in range(DISPATCH_UNROLL * TOP_K):
            pltpu.make_async_copy(_row(y_hbm, 0), _row(buf.at[slot, 0], 0), sem.at[slot]).wait()
        return c

    lax.fori_loop(0, tm // DISPATCH_UNROLL, drain, 0)
    gate = gate_ref[...]
    acc = gate[:, 0:1] * _load_row_tiles(buf.at[slot, 0], tm)
    for kk in range(1, TOP_K):
        acc = acc + gate[:, kk:kk + 1] * _load_row_tiles(buf.at[slot, kk], tm)
    o_ref[...] = x1_ref[...] + mod_ref[5:6, :] * acc


def _combine(dest_flat, gate, x1, mod3, ybuf, S):
    T, D = x1.shape
    tm = TM_COMBINE
    per_b = S // tm
    nt = T // tm
    return pl.pallas_call(
        _combine_kernel,
        grid=(nt,),
        in_specs=[pl.BlockSpec((tm * TOP_K,), lambda i: (i,), memory_space=pltpu.SMEM),
                  pl.BlockSpec((tm * TOP_K,), lambda i: (jnp.minimum(i + 1, nt - 1),), memory_space=pltpu.SMEM),
                  pl.BlockSpec((tm, TOP_K), lambda i: (i, 0)),
                  pl.BlockSpec((tm, D), lambda i: (i, 0)),
                  pl.BlockSpec((None, 6, D), lambda i: (i // per_b, 0, 0)),
                  pl.BlockSpec(memory_space=pl.ANY)],
        out_specs=pl.BlockSpec((tm, D), lambda i: (i, 0)),
        out_shape=jax.ShapeDtypeStruct((T, D), F32),
        scratch_shapes=[pltpu.VMEM((2, TOP_K, tm * ROW_TILE, V7X_LANES), F32), pltpu.SemaphoreType.DMA((2,))],
        compiler_params=_cparams(("arbitrary",)),
        name="combine",
    )(dest_flat, dest_flat, gate, x1, mod3, ybuf)


def kernel(x, c, positions, w_ada, b_ada, norm1_gain, w_in, lower_bound_logits, hg_norm_gain, w_hg_branch,
           q_norm_gain, k_norm_gain, attn_sinks, w_attn_branch, w_out, norm2_gain, w_router, b_router,
           w_up, b_up, w_down, b_down):
    B, S, D = x.shape
    T = B * S
    assert w_ada.shape[0] == 1, "one layer"
    x2 = x.reshape(T, D)

    mod = _ada(c, w_ada[0], b_ada[0])
    mod3 = mod.reshape(B, 6, D)

    hg_in, at_in, gates = _inproj(x2, mod3, norm1_gain[0], w_in[0].astype(BF16), S)
    hg_o = _hgrn(hg_in, lower_bound_logits, hg_norm_gain[0], B, S)
    at_o = _swa(at_in, positions.reshape(B, S, 1), q_norm_gain[0], k_norm_gain[0], attn_sinks[0], B, S)

    x1, h2, idx, gate, rank, cnt = _merge_router(
        x2, hg_o, at_o, gates, mod3, w_hg_branch[0].astype(BF16), w_attn_branch[0].astype(BF16),
        w_out[0].astype(BF16), norm2_gain[0], w_router[0].astype(BF16), b_router[0], S)

    bm = MOE_BM
    counts = cnt[0].astype(I32)
    padded = (counts + bm - 1) // bm * bm
    pad_end = jnp.cumsum(padded)
    pad_start = pad_end - padded
    P = T * TOP_K + N_EXPERTS * bm
    n_blocks = P // bm
    block_start = jnp.arange(n_blocks, dtype=I32) * bm
    block_expert = jnp.minimum(
        jnp.sum((pad_end[None, :] <= block_start[:, None]).astype(I32), axis=1), N_EXPERTS - 1).astype(I32)
    n_used = (pad_end[-1:] // bm).astype(I32)
    tail = jnp.concatenate([n_used, n_blocks - n_used]).astype(I32)

    dest = _dest(idx, rank, pad_start)
    dest_flat = dest.reshape(T * TOP_K)
    xbuf = _dispatch((pad_start + counts).astype(I32), (padded - counts).astype(I32), tail, dest_flat, h2, P)

    ybuf = _ffn(block_expert, n_used, xbuf, w_up[0],
                b_up[0][:, None, 0::2], b_up[0][:, None, 1::2],
                w_down[0], b_down[0][:, None, :])

    out = _combine(dest_flat, gate, x1, mod3, ybuf, S)
    return out.reshape(B, S, D)
```

```python
import functools

import numpy as np
import jax
import jax.numpy as jnp
from jax import lax
from jax.experimental import pallas as pl
from jax.experimental.pallas import tpu as pltpu

F32 = jnp.float32
BF16 = jnp.bfloat16
I32 = jnp.int32

HG_HEADS = 4
HG_DIM = 128
HG_W = HG_HEADS * HG_DIM
ATT_Q_HEADS = 8
ATT_KV_HEADS = 2
ATT_GROUP = ATT_Q_HEADS // ATT_KV_HEADS
ATT_DH = 64
ATT_Q_W = ATT_Q_HEADS * ATT_DH
ATT_KV_W = ATT_KV_HEADS * ATT_DH
ATT_BLOCK = 128
ROPE_THETA = 500000.0
ROPE_DIM = ATT_DH // 4
N_EXPERTS = 32
TOP_K = 4
SWIGLU_ALPHA = 1.702
SWIGLU_LIMIT = 7.0
EPS = 1e-6

V7X_LANES = 128
V7X_SUBLANES = 8
V7X_VMEM_LIMIT_BYTES = 56 * 1024 * 1024

TM_PROJ = 256
HG_TILE = 512
HG_CHUNK = 64
HG_SUB = 8
MOE_BM = 256
TM_DISPATCH = 256
TM_COMBINE = 128

NEG_INF = float("-inf")


def _cparams(sem, vmem=V7X_VMEM_LIMIT_BYTES):
    return pltpu.CompilerParams(dimension_semantics=sem, vmem_limit_bytes=vmem)


def _sigmoid(x):
    return 1.0 / (1.0 + jnp.exp(-x))


def _dot(a, b):
    return jnp.dot(a, b, preferred_element_type=F32)


def _dot_nt(a, b):
    return lax.dot_general(a, b, (((1,), (1,)), ((), ())), preferred_element_type=F32)


def _dot_tn(a, b):
    return lax.dot_general(a, b, (((0,), (0,)), ((), ())), preferred_element_type=F32)


ROW_TILE = V7X_SUBLANES


def _store_row_tiles(ref, val):
    rows = val.shape[0]
    for g in range(ROW_TILE):
        ref[pl.ds(g, rows, stride=ROW_TILE), :] = val[:, g * V7X_LANES:(g + 1) * V7X_LANES]


def _load_row_tiles(ref, rows):
    return jnp.concatenate([ref[pl.ds(g, rows, stride=ROW_TILE), :] for g in range(ROW_TILE)], axis=1)


def _dot_exact01(m01, x):
    hi = x.astype(BF16)
    r1 = x - hi.astype(F32)
    mid = r1.astype(BF16)
    lo = (r1 - mid.astype(F32)).astype(BF16)
    return _dot(m01, hi) + _dot(m01, mid) + _dot(m01, lo)


def _ada_kernel(c_ref, w_ref, b_ref, o_ref):
    c = c_ref[...]
    cond = c * _sigmoid(c)
    o_ref[...] = _dot(cond.astype(BF16), w_ref[...].astype(BF16)) + b_ref[...]


def _ada(c, w_ada, b_ada):
    B, D = c.shape
    N = w_ada.shape[1]
    tn = D
    return pl.pallas_call(
        _ada_kernel,
        grid=(N // tn,),
        in_specs=[pl.BlockSpec((B, D), lambda j: (0, 0)),
                  pl.BlockSpec((D, tn), lambda j: (0, j)),
                  pl.BlockSpec((1, tn), lambda j: (0, j))],
        out_specs=pl.BlockSpec((B, tn), lambda j: (0, j)),
        out_shape=jax.ShapeDtypeStruct((B, N), F32),
        compiler_params=_cparams(("parallel",)),
        name="ada",
    )(c, w_ada, b_ada.reshape(1, N))


def _inproj_kernel(x_ref, mod_ref, g_ref, w_ref, hg_ref, at_ref, gt_ref, fmin_ref):
    x = x_ref[...]
    ms = jnp.mean(x * x, axis=-1, keepdims=True)
    y = x * lax.rsqrt(ms + EPS) * g_ref[...]
    h = (y * (1.0 + mod_ref[1:2, :]) + mod_ref[0:1, :]).astype(BF16)
    n_hg = hg_ref.shape[1]
    n_at = at_ref.shape[1]
    n_gt = gt_ref.shape[1]
    hg = _dot(h, w_ref[:, 0:n_hg])
    hg_ref[...] = hg
    f_pre = hg[:, HG_W:2 * HG_W]
    f_min = jnp.min(jnp.min(f_pre, axis=-1, keepdims=True), axis=0, keepdims=True)
    fmin_ref[...] = jnp.broadcast_to(f_min, fmin_ref.shape)
    at_ref[...] = _dot(h, w_ref[:, n_hg:n_hg + n_at])
    gt_ref[...] = _dot(h, w_ref[:, n_hg + n_at:n_hg + n_at + n_gt])


def _inproj(x2, mod3, gain, w_in_bf, S):
    T, D = x2.shape
    tm = TM_PROJ
    n_hg = 4 * HG_W
    n_at = ATT_Q_W + 2 * ATT_KV_W
    n_gt = 2 * D
    assert w_in_bf.shape[1] == n_hg + n_at + n_gt
    per_b = S // tm
    return pl.pallas_call(
        _inproj_kernel,
        grid=(T // tm,),
        in_specs=[pl.BlockSpec((tm, D), lambda i: (i, 0)),
                  pl.BlockSpec((None, 6, D), lambda i: (i // per_b, 0, 0)),
                  pl.BlockSpec((1, D), lambda i: (0, 0)),
                  pl.BlockSpec(w_in_bf.shape, lambda i: (0, 0))],
        out_specs=[pl.BlockSpec((tm, n_hg), lambda i: (i, 0)),
                   pl.BlockSpec((tm, n_at), lambda i: (i, 0)),
                   pl.BlockSpec((tm, n_gt), lambda i: (i, 0)),
                   pl.BlockSpec((V7X_SUBLANES, V7X_LANES), lambda i: (i, 0))],
        out_shape=[jax.ShapeDtypeStruct((T, n_hg), F32),
                   jax.ShapeDtypeStruct((T, n_at), F32),
                   jax.ShapeDtypeStruct((T, n_gt), F32),
                   jax.ShapeDtypeStruct((T // tm * V7X_SUBLANES, V7X_LANES), F32)],
        compiler_params=_cparams(("parallel",)),
        name="inproj",
    )(x2, mod3, gain.reshape(1, D), w_in_bf)


def _hgrn_constants():
    C = HG_CHUNK
    t = np.arange(C)[None, :]
    i = np.arange(C)[:, None]
    mats = [(t <= i), (t > i)]
    halves = []
    lh = C // 2
    while lh >= HG_SUB:
        halves.append(lh)
        lh //= 2
    code = np.zeros((C, C), np.int32)
    ii, jj = np.meshgrid(np.arange(C), np.arange(C), indexing="ij")
    for li, lh in enumerate(halves):
        ref = (i // (2 * lh)) * (2 * lh) + lh - 1
        second = (i % (2 * lh)) >= lh
        m = np.where(second, (t > ref) & (t <= i), (t > i) & (t <= ref))
        mats.append(m)
        sel = ((ii // (2 * lh)) == (jj // (2 * lh))) & ((ii % (2 * lh)) >= lh) & ((jj % (2 * lh)) < lh)
        code[sel] = li + 1
    sub_end = (i // HG_SUB) * HG_SUB + HG_SUB - 1
    mats.append((t > i) & (t <= sub_end))
    m_all = np.concatenate(mats, axis=0).astype(np.float32)
    return m_all, code, len(halves)


HG_MILD_FMIN = -7.0


def _hgrn_kernel(mild_ref, in_ref, lbl_ref, gain_ref, mall_ref, code_ref, wsum_ref, o_ref, st_ref, *, n_levels):
    C = HG_CHUNK
    W = HG_W
    n_chunks = in_ref.shape[1] // C
    step_is_mild = mild_ref[pl.program_id(0) * pl.num_programs(1) + pl.program_id(1)] == 1

    @pl.when(pl.program_id(1) == 0)
    def _():
        st_ref[...] = jnp.zeros_like(st_ref)

    l0 = lbl_ref[0:1, :]
    l1 = lbl_ref[1:2, :]
    mx = jnp.maximum(l0, l1)
    e0 = jnp.exp(l0 - mx)
    lb = e0 / (e0 + jnp.exp(l1 - mx))
    gain = gain_ref[...]
    mall = mall_ref[...].astype(BF16)
    code = code_ref[...]
    wsum = wsum_ref[...]
    row = lax.broadcasted_iota(I32, (C, C), 0)
    col = lax.broadcasted_iota(I32, (C, C), 1)
    row_in_sub = lax.broadcasted_iota(I32, (C, HG_DIM), 0) % HG_SUB
    same_sub = (row // HG_SUB) == (col // HG_SUB)
    sub_causal = same_sub & (col <= row)
    n_rows = (2 + n_levels) * C

    def bcast_sub(a, j):
        a3 = a.reshape(C // HG_SUB, HG_SUB, HG_DIM)
        return jnp.broadcast_to(a3[:, j:j + 1, :], a3.shape).reshape(C, HG_DIM)

    def chunk(mild, c, carry):
        r0 = pl.multiple_of(c * C, C)
        rows = pl.ds(r0, C)
        for bb in range(in_ref.shape[0]):
            qp = in_ref[bb, rows, 0:W]
            fp = in_ref[bb, rows, W:2 * W]
            vv = in_ref[bb, rows, 2 * W:3 * W]
            gp = in_ref[bb, rows, 3 * W:4 * W]
            forget = lb + (1.0 - lb) * _sigmoid(fp)
            q = qp * _sigmoid(qp)
            k = 1.0 - forget
            lf = jnp.log(forget)
            expo = _dot_exact01(mall if mild else mall[0:n_rows], lf)
            e_all = jnp.exp(expo[0:n_rows])
            outs = []
            for h in range(HG_HEADS):
                ls = slice(h * HG_DIM, (h + 1) * HG_DIM)
                qh, kh, vh = q[:, ls], k[:, ls], vv[:, ls]
                bh = expo[0:C, ls]
                st = st_ref[bb, h]
                o = _dot_nt((qh * e_all[0:C, ls]).astype(BF16), st.astype(BF16))
                if mild:
                    x = expo[n_rows:n_rows + C, ls]
                    s = _dot_nt((qh * jnp.exp(-x)).astype(BF16), (kh * jnp.exp(x)).astype(BF16))
                    scores = jnp.where(sub_causal, s, 0.0)
                else:
                    ps = []
                    for j in range(HG_SUB):
                        d = jnp.where(row_in_sub >= j, bh - bcast_sub(bh, j), NEG_INF)
                        ps.append((qh * bcast_sub(kh, j) * jnp.exp(d)).astype(BF16))
                    scores = jnp.where(same_sub, _dot(jnp.concatenate(ps, axis=1), wsum), 0.0)
                for li in range(n_levels):
                    e = e_all[(2 + li) * C:(3 + li) * C, ls]
                    s = _dot_nt((qh * e).astype(BF16), (kh * e).astype(BF16))
                    scores = jnp.where(code == li + 1, s, scores)
                o = o + _dot(scores.astype(BF16), vh.astype(BF16))
                kst = (kh * e_all[C:2 * C, ls]).astype(BF16)
                st_ref[bb, h] = st * e_all[C - 1:C, ls] + _dot_tn(vh.astype(BF16), kst)
                ms = jnp.mean(o * o, axis=-1, keepdims=True)
                on = o * lax.rsqrt(ms + EPS) * gain
                gh = gp[:, ls]
                outs.append(on * (gh * _sigmoid(gh)))
            o_ref[bb, rows, :] = jnp.concatenate(outs, axis=-1).astype(o_ref.dtype)
        return carry

    @pl.when(step_is_mild)
    def _():
        lax.fori_loop(0, n_chunks, functools.partial(chunk, True), 0)

    @pl.when(jnp.logical_not(step_is_mild))
    def _():
        lax.fori_loop(0, n_chunks, functools.partial(chunk, False), 0)


HG_SEQS = 2


def _hgrn(hg_in, f_min, lb_logits, gain, B, S):
    T = hg_in.shape[0]
    lt = HG_TILE
    m_all, code, n_levels = _hgrn_constants()
    nseq = HG_SEQS if B % HG_SEQS == 0 else 1
    step_min = jnp.min(f_min.reshape(B // nseq, nseq, S // lt, lt // TM_PROJ), axis=(1, 3))
    mild = (step_min >= HG_MILD_FMIN).astype(I32).reshape(-1)
    wsum = (np.arange(HG_SUB * HG_DIM)[:, None] // HG_DIM == np.arange(HG_CHUNK)[None, :] % HG_SUB).astype(np.float32)
    const = lambda b, s, m: (0, 0)
    grid_spec = pltpu.PrefetchScalarGridSpec(
        num_scalar_prefetch=1,
        grid=(B // nseq, S // lt),
        in_specs=[pl.BlockSpec((nseq, lt, 4 * HG_W), lambda b, s, m: (b, s, 0)),
                  pl.BlockSpec(lb_logits.shape, const),
                  pl.BlockSpec((1, HG_DIM), const),
                  pl.BlockSpec(m_all.shape, const),
                  pl.BlockSpec(code.shape, const),
                  pl.BlockSpec(wsum.shape, const)],
        out_specs=pl.BlockSpec((nseq, lt, HG_W), lambda b, s, m: (b, s, 0)),
        scratch_shapes=[pltpu.VMEM((nseq, HG_HEADS, HG_DIM, HG_DIM), F32)],
    )
    out = pl.pallas_call(
        functools.partial(_hgrn_kernel, n_levels=n_levels),
        grid_spec=grid_spec,
        out_shape=jax.ShapeDtypeStruct((B, S, HG_W), BF16),
        compiler_params=_cparams(("parallel", "arbitrary")),
        name="hgrn2",
    )(mild, hg_in.reshape(B, S, 4 * HG_W), lb_logits, gain.reshape(1, HG_DIM), jnp.asarray(m_all),
      jnp.asarray(code), jnp.asarray(wsum, dtype=BF16))
    return out.reshape(T, HG_W)


SWA_QBLOCKS = 4
ROPE_ROWS = 16


def _swa_constants():
    lane = np.arange(V7X_LANES)
    c = lane % ATT_DH
    half = ROPE_DIM // 2
    inv_freq = ROPE_THETA ** (-jnp.arange(half, dtype=F32) / half)
    freq_rows = jnp.broadcast_to(
        jnp.concatenate([inv_freq, jnp.zeros((ROPE_ROWS - half,), F32)])[:, None], (ROPE_ROWS, V7X_LANES))
    sel = ((np.arange(ROPE_ROWS)[:, None] == (c % half)[None, :]) & (c < ROPE_DIM)[None, :]).astype(np.float32)
    sign = np.where(c < half, -1.0, np.where(c < ROPE_DIM, 1.0, 0.0)).astype(np.float32)
    first = (c < half).astype(np.float32)
    tab = np.stack([sign, first], axis=0)
    g = (lane[:, None] // ATT_DH == lane[None, :] // ATT_DH).astype(np.float32) / ATT_DH
    return freq_rows, sel, tab, g


def _swa_kernel(sink_ref, cur_ref, prev_ref, pcur_ref, pprev_ref, qg_ref, kg_ref, freq_ref, sel_ref, tab_ref, g_ref,
                o_ref):
    Bq = ATT_BLOCK
    n = pl.program_id(1)
    tab = tab_ref[...]
    sign, first = tab[0:1, :], tab[1:2, :]
    gmat = g_ref[...].astype(BF16)
    sel = sel_ref[...].astype(BF16)
    freq_rows = freq_ref[...]
    half = ROPE_DIM // 2
    scale = ATT_DH ** -0.5

    def group_ms(x):
        sq = x * x
        hi = sq.astype(BF16)
        lo = (sq - hi.astype(F32)).astype(BF16)
        return _dot(hi, gmat) + _dot(lo, gmat)

    def spread(a):
        hi = a.astype(BF16)
        r1 = a - hi.astype(F32)
        mid = r1.astype(BF16)
        lo = (r1 - mid.astype(F32)).astype(BF16)
        return _dot_tn(hi, sel) + _dot_tn(mid, sel) + _dot_tn(lo, sel)

    def rope_table(pos_row):
        ang = freq_rows * pos_row
        return 1.0 + spread(jnp.cos(ang) - 1.0), spread(jnp.sin(ang)) * sign

    def norm_rope(x, gain, cs_sn):
        y = x * lax.rsqrt(group_ms(x) + EPS) * gain
        partner = jnp.where(first > 0.5, pltpu.roll(y, V7X_LANES - half, 1), pltpu.roll(y, half, 1))
        return y * cs_sn[0] + partner * cs_sn[1]

    kq = ATT_Q_W
    lane = lax.broadcasted_iota(I32, (Bq, V7X_LANES), 1)
    lo_half = lane < ATT_DH

    def pad_variants(a):
        r = pltpu.roll(a, ATT_DH, 1)
        z = jnp.zeros_like(a)
        return [[jnp.where(lo_half, a, z).astype(BF16), jnp.where(lo_half, z, r).astype(BF16)],
                [jnp.where(lo_half, r, z).astype(BF16), jnp.where(lo_half, z, a).astype(BF16)]]

    tables = [rope_table(pprev_ref[0].astype(F32))]
    kblocks = [pad_variants(norm_rope(prev_ref[:, 0:ATT_KV_W], kg_ref[...], tables[0]))]
    vblocks = [pad_variants(prev_ref[:, ATT_KV_W:2 * ATT_KV_W])]
    for j in range(SWA_QBLOCKS):
        rows = slice(j * Bq, (j + 1) * Bq)
        tables.append(rope_table(pcur_ref[j].astype(F32)))
        kblocks.append(pad_variants(norm_rope(cur_ref[rows, kq:kq + ATT_KV_W], kg_ref[...], tables[j + 1])))
        vblocks.append(pad_variants(cur_ref[rows, kq + ATT_KV_W:kq + 2 * ATT_KV_W]))

    qi = lax.broadcasted_iota(I32, (Bq, 2 * Bq), 0)
    kj = lax.broadcasted_iota(I32, (Bq, 2 * Bq), 1)
    in_band = ((kj < Bq) & (kj > qi)) | ((kj >= Bq) & ((kj - Bq) <= qi))
    first_of_seq = (jnp.zeros((Bq, 2 * Bq), I32) + n) == 0
    for j in range(SWA_QBLOCKS):
        rows = slice(j * Bq, (j + 1) * Bq)
        mask = (in_band & jnp.logical_not(first_of_seq & (kj < Bq))) if j == 0 else in_band
        for t in range(ATT_Q_W // V7X_LANES):
            ls = slice(t * V7X_LANES, (t + 1) * V7X_LANES)
            qt = (norm_rope(cur_ref[rows, ls], qg_ref[...], tables[j + 1]) * scale).astype(BF16)
            acc = jnp.zeros((Bq, V7X_LANES), F32)
            for u in range(2):
                head = 2 * t + u
                kvh = head // ATT_GROUP
                kcat = jnp.concatenate([kblocks[j][kvh][u], kblocks[j + 1][kvh][u]], axis=0)
                vcat = jnp.concatenate([vblocks[j][kvh][u], vblocks[j + 1][kvh][u]], axis=0)
                s = jnp.where(mask, _dot_nt(qt, kcat), NEG_INF)
                sink = sink_ref[head]
                m = jnp.maximum(jnp.max(s, axis=-1, keepdims=True), sink)
                p = jnp.exp(s - m)
                denom = jnp.sum(p, axis=-1, keepdims=True) + jnp.exp(sink - m)
                acc = acc + _dot(p.astype(BF16), vcat) * (1.0 / denom)
            o_ref[rows, ls] = acc.astype(o_ref.dtype)


def _swa(at_in, positions, q_gain, k_gain, sinks, B, S):
    T = at_in.shape[0]
    nb = S // ATT_BLOCK
    qb = SWA_QBLOCKS
    assert nb % qb == 0
    steps = nb // qb
    freq_rows, sel, tab, g = _swa_constants()
    qg = jnp.tile(q_gain.reshape(1, ATT_DH), (1, V7X_LANES // ATT_DH))
    kg = jnp.tile(k_gain.reshape(1, ATT_DH), (1, V7X_LANES // ATT_DH))
    pos3 = positions.reshape(B * nb, 1, ATT_BLOCK)
    n_at = at_in.shape[1]
    kv_blk = 2 * ATT_KV_W
    assert ATT_Q_W % kv_blk == 0
    prev_blk = lambda b, n: b * nb + jnp.maximum(qb * n - 1, 0)
    const = lambda b, n: (0, 0)
    return pl.pallas_call(
        _swa_kernel,
        grid=(B, steps),
        in_specs=[pl.BlockSpec(memory_space=pltpu.SMEM),
                  pl.BlockSpec((qb * ATT_BLOCK, n_at), lambda b, n: (b * steps + n, 0)),
                  pl.BlockSpec((ATT_BLOCK, kv_blk), lambda b, n: (prev_blk(b, n), ATT_Q_W // kv_blk)),
                  pl.BlockSpec((qb, 1, ATT_BLOCK), lambda b, n: (b * steps + n, 0, 0)),
                  pl.BlockSpec((1, 1, ATT_BLOCK), lambda b, n: (prev_blk(b, n), 0, 0)),
                  pl.BlockSpec((1, V7X_LANES), const),
                  pl.BlockSpec((1, V7X_LANES), const),
                  pl.BlockSpec(freq_rows.shape, const),
                  pl.BlockSpec(sel.shape, const),
                  pl.BlockSpec(tab.shape, const),
                  pl.BlockSpec(g.shape, const)],
        out_specs=pl.BlockSpec((qb * ATT_BLOCK, ATT_Q_W), lambda b, n: (b * steps + n, 0)),
        out_shape=jax.ShapeDtypeStruct((T, ATT_Q_W), BF16),
        compiler_params=_cparams(("parallel", "parallel")),
        name="swa",
    )(sinks, at_in, at_in, pos3, pos3, qg, kg, freq_rows, jnp.asarray(sel), jnp.asarray(tab), jnp.asarray(g))


def _merge_router_kernel(x_ref, hg_ref, at_ref, gt_ref, mod_ref, whg_ref, wat_ref, wout_ref, g2_ref,
                         wr_ref, br_ref, tri_ref,
                         x1_ref, h2_ref, idx_ref, gate_ref, rank_ref, cnt_ref, run_ref):
    i = pl.program_id(0)
    D = x_ref.shape[1]
    tm = x_ref.shape[0]

    @pl.when(i == 0)
    def _():
        run_ref[...] = jnp.zeros_like(run_ref)

    y_h = _dot(hg_ref[...], whg_ref[...])
    y_a = _dot(at_ref[...], wat_ref[...])
    merged = _sigmoid(gt_ref[:, 0:D]) * y_h + _sigmoid(gt_ref[:, D:2 * D]) * y_a
    x1 = x_ref[...] + mod_ref[2:3, :] * _dot(merged.astype(BF16), wout_ref[...])
    x1_ref[...] = x1
    ms = jnp.mean(x1 * x1, axis=-1, keepdims=True)
    h2 = x1 * lax.rsqrt(ms + EPS) * g2_ref[...] * (1.0 + mod_ref[4:5, :]) + mod_ref[3:4, :]
    _store_row_tiles(h2_ref, h2)
    logits = _dot(h2.astype(BF16), wr_ref[...]) + br_ref[...]
    E = logits.shape[1]
    lane = lax.broadcasted_iota(I32, (tm, E), 1).astype(F32)
    vals, idxs = [], []
    l = logits
    for _ in range(TOP_K):
        m = jnp.max(l, axis=-1, keepdims=True)
        ik = jnp.min(jnp.where(l == m, lane, float(E)), axis=-1, keepdims=True)
        vals.append(m)
        idxs.append(ik)
        l = jnp.where(lane == ik, NEG_INF, l)
    ex = [jnp.exp(v - vals[0]) for v in vals]
    den = ex[0]
    for e in ex[1:]:
        den = den + e
    onehot = jnp.zeros((tm, E), F32)
    for ik in idxs:
        onehot = onehot + (lane == ik).astype(F32)
    cum = _dot(tri_ref[...], onehot.astype(BF16))
    run = run_ref[0:1, 0:E]
    excl = cum - onehot + run
    lane_k = lax.broadcasted_iota(I32, (tm, TOP_K), 1)
    idx_o = jnp.zeros((tm, TOP_K), I32)
    gate_o = jnp.zeros((tm, TOP_K), F32)
    rank_o = jnp.zeros((tm, TOP_K), I32)
    for kk in range(TOP_K):
        rk = jnp.sum(jnp.where(lane == idxs[kk], excl, 0.0), axis=-1, keepdims=True)
        idx_o = jnp.where(lane_k == kk, idxs[kk].astype(I32), idx_o)
        gate_o = jnp.where(lane_k == kk, ex[kk] / den, gate_o)
        rank_o = jnp.where(lane_k == kk, rk.astype(I32), rank_o)
    idx_ref[...] = idx_o
    gate_ref[...] = gate_o
    rank_ref[...] = rank_o
    new_run = run + cum[tm - 1:tm, :]
    run_ref[0:1, 0:E] = new_run
    cnt_ref[...] = jnp.broadcast_to(new_run, cnt_ref.shape)


def _merge_router(x2, hg_o, at_o, gates, mod3, whg, wat, wout, g2, wr, br, S):
    T, D = x2.shape
    tm = TM_PROJ
    per_b = S // tm
    E = wr.shape[1]
    tri = jnp.asarray(np.tril(np.ones((tm, tm), np.float32)), dtype=BF16)
    row = lambda i: (i, 0)
    const = lambda i: (0, 0)
    return pl.pallas_call(
        _merge_router_kernel,
        grid=(T // tm,),
        in_specs=[pl.BlockSpec((tm, D), row),
                  pl.BlockSpec((tm, HG_W), row),
                  pl.BlockSpec((tm, ATT_Q_W), row),
                  pl.BlockSpec((tm, 2 * D), row),
                  pl.BlockSpec((None, 6, D), lambda i: (i // per_b, 0, 0)),
                  pl.BlockSpec(whg.shape, const),
                  pl.BlockSpec(wat.shape, const),
                  pl.BlockSpec(wout.shape, const),
                  pl.BlockSpec((1, D), const),
                  pl.BlockSpec(wr.shape, const),
                  pl.BlockSpec((1, E), const),
                  pl.BlockSpec((tm, tm), const)],
        out_specs=[pl.BlockSpec((tm, D), row),
                   pl.BlockSpec((tm * ROW_TILE, V7X_LANES), row),
                   pl.BlockSpec((tm, TOP_K), row),
                   pl.BlockSpec((tm, TOP_K), row),
                   pl.BlockSpec((tm, TOP_K), row),
                   pl.BlockSpec((V7X_SUBLANES, E), const)],
        out_shape=[jax.ShapeDtypeStruct((T, D), F32),
                   jax.ShapeDtypeStruct((T * ROW_TILE, V7X_LANES), F32),
                   jax.ShapeDtypeStruct((T, TOP_K), I32),
                   jax.ShapeDtypeStruct((T, TOP_K), F32),
                   jax.ShapeDtypeStruct((T, TOP_K), I32),
                   jax.ShapeDtypeStruct((V7X_SUBLANES, E), F32)],
        scratch_shapes=[pltpu.VMEM((V7X_SUBLANES, V7X_LANES), F32)],
        compiler_params=_cparams(("arbitrary",)),
        name="merge_router",
    )(x2, hg_o, at_o, gates, mod3, whg, wat, wout, g2.reshape(1, D), wr, br.reshape(1, E), tri)


def _dest_kernel(idx_ref, rank_ref, ps_ref, o_ref):
    idx = idx_ref[...]
    tm = idx.shape[0]
    E = ps_ref.shape[1]
    lane = lax.broadcasted_iota(I32, (tm, E), 1)
    lane_k = lax.broadcasted_iota(I32, (tm, TOP_K), 1)
    ps = ps_ref[...].astype(F32)
    out = rank_ref[...]
    for kk in range(TOP_K):
        start = jnp.sum(jnp.where(lane == idx[:, kk:kk + 1], ps, 0.0), axis=-1, keepdims=True)
        out = out + jnp.where(lane_k == kk, start.astype(I32), 0)
    o_ref[...] = out


def _dest(idx, rank, pad_start):
    T = idx.shape[0]
    tm = 1024
    E = pad_start.shape[0]
    row = lambda i: (i, 0)
    return pl.pallas_call(
        _dest_kernel,
        grid=(T // tm,),
        in_specs=[pl.BlockSpec((tm, TOP_K), row), pl.BlockSpec((tm, TOP_K), row),
                  pl.BlockSpec((1, E), lambda i: (0, 0))],
        out_specs=pl.BlockSpec((tm, TOP_K), row),
        out_shape=jax.ShapeDtypeStruct((T, TOP_K), I32),
        compiler_params=_cparams(("parallel",)),
        name="dest",
    )(idx, rank, pad_start.reshape(1, E))


DISPATCH_UNROLL = 4


def _row(ref, r):
    return ref.at[pl.ds(pl.multiple_of(r * ROW_TILE, ROW_TILE), ROW_TILE)]


def _dispatch_kernel(fill_start_ref, fill_n_ref, tail_ref, dest_ref, h_ref, xbuf_ref, zero_ref, sem):
    tm = h_ref.shape[0] // ROW_TILE
    zrows = zero_ref.shape[0]

    def zero_row_copy(r):
        return pltpu.make_async_copy(_row(zero_ref, 0), _row(xbuf_ref, r), sem.at[1])

    def zero_block_copy(b):
        dst = xbuf_ref.at[pl.ds(pl.multiple_of(b * zrows, zrows), zrows)]
        return pltpu.make_async_copy(zero_ref, dst, sem.at[2])

    @pl.when(pl.program_id(0) == 0)
    def _():
        zero_ref[...] = jnp.zeros_like(zero_ref)

        def per_expert(start):
            def body(e, c):
                def rows(r, c2):
                    cp = zero_row_copy(fill_start_ref[e] + r)
                    cp.start() if start else cp.wait()
                    return c2
                return lax.fori_loop(0, fill_n_ref[e], rows, c)
            return body

        def tail(start):
            def body(b, c):
                cp = zero_block_copy(tail_ref[0] + b)
                cp.start() if start else cp.wait()
                return c
            return body

        lax.fori_loop(0, N_EXPERTS, per_expert(True), 0)
        lax.fori_loop(0, tail_ref[1], tail(True), 0)
        lax.fori_loop(0, N_EXPERTS, per_expert(False), 0)
        lax.fori_loop(0, tail_ref[1], tail(False), 0)

    def issue(tb, c):
        for u in range(DISPATCH_UNROLL):
            t = tb * DISPATCH_UNROLL + u
            src = _row(h_ref, t)
            for kk in range(TOP_K):
                d = dest_ref[t * TOP_K + kk]
                pltpu.make_async_copy(src, _row(xbuf_ref, d), sem.at[0]).start(priority=kk % 2)
        return c

    lax.fori_loop(0, tm // DISPATCH_UNROLL, issue, 0)

    def drain(tb, c):
        for _ in range(DISPATCH_UNROLL * TOP_K):
            pltpu.make_async_copy(_row(h_ref, 0), _row(xbuf_ref, 0), sem.at[0]).wait()
        return c

    lax.fori_loop(0, tm // DISPATCH_UNROLL, drain, 0)


def _dispatch(fill_start, fill_n, tail, dest_flat, h2t, P):
    T = h2t.shape[0] // ROW_TILE
    tm = TM_DISPATCH
    grid_spec = pltpu.PrefetchScalarGridSpec(
        num_scalar_prefetch=3,
        grid=(T // tm,),
        in_specs=[pl.BlockSpec((tm * TOP_K,), lambda i, *_: (i,), memory_space=pltpu.SMEM),
                  pl.BlockSpec((tm * ROW_TILE, V7X_LANES), lambda i, *_: (i, 0))],
        out_specs=pl.BlockSpec(memory_space=pl.ANY),
        scratch_shapes=[pltpu.VMEM((MOE_BM * ROW_TILE, V7X_LANES), F32), pltpu.SemaphoreType.DMA((3,))],
    )
    return pl.pallas_call(
        _dispatch_kernel,
        grid_spec=grid_spec,
        out_shape=jax.ShapeDtypeStruct((P * ROW_TILE, V7X_LANES), F32),
        compiler_params=_cparams(("arbitrary",)),
        name="dispatch",
    )(fill_start, fill_n, tail, dest_flat, h2t)


FFN_PREP_COLS = 256


def _ffn_kernel(be_ref, nu_ref, x_ref, wu_ref, bg_ref, bl_ref, wd_ref, bd_ref, y_ref,
                t_ref, wgt_ref, wlt_ref, wdb_ref):
    i = pl.program_id(0)
    bm = x_ref.shape[0] // ROW_TILE
    D, De2 = wu_ref.shape
    n_slab = D // V7X_LANES
    half = FFN_PREP_COLS // 2
    used = i < nu_ref[0]
    new_expert = (i == 0) | (be_ref[i] != be_ref[jnp.maximum(i - 1, 0)])

    @pl.when(used & new_expert)
    def _():
        for c in range(De2 // FFN_PREP_COLS):
            tt = wu_ref[:, c * FFN_PREP_COLS:(c + 1) * FFN_PREP_COLS].T
            for s in range(n_slab):
                t_ref[s] = tt[:, s * V7X_LANES:(s + 1) * V7X_LANES]
            for s in range(n_slab):
                ls = slice(s * V7X_LANES, (s + 1) * V7X_LANES)
                wgt_ref[c * half:(c + 1) * half, ls] = t_ref[s, pl.ds(0, half, stride=2), :].astype(BF16)
                wlt_ref[c * half:(c + 1) * half, ls] = t_ref[s, pl.ds(1, half, stride=2), :].astype(BF16)
        wdb_ref[...] = wd_ref[...].astype(BF16)

    @pl.when(used)
    def _():
        x = _load_row_tiles(x_ref, bm).astype(BF16)
        glu = _dot_nt(x, wgt_ref[...]) + bg_ref[...]
        lin = _dot_nt(x, wlt_ref[...]) + bl_ref[...]
        glu = jnp.minimum(glu, SWIGLU_LIMIT)
        lin = jnp.clip(lin, -SWIGLU_LIMIT, SWIGLU_LIMIT)
        act = glu * _sigmoid(SWIGLU_ALPHA * glu) * (lin + 1.0)
        _store_row_tiles(y_ref, _dot(act.astype(BF16), wdb_ref[...]) + bd_ref[...])

    @pl.when(jnp.logical_not(used))
    def _():
        y_ref[...] = jnp.zeros_like(y_ref)


def _ffn(block_expert, n_used, xbuf, w_up, bg, bl, w_down, bd):
    P = xbuf.shape[0] // ROW_TILE
    bm = MOE_BM
    _, D, De2 = w_up.shape
    De = De2 // 2
    wmap = lambda i, be, nu: (be[i], 0, 0)
    rows = lambda i, be, nu: (i, 0)
    grid_spec = pltpu.PrefetchScalarGridSpec(
        num_scalar_prefetch=2,
        grid=(P // bm,),
        in_specs=[pl.BlockSpec((bm * ROW_TILE, V7X_LANES), rows),
                  pl.BlockSpec((None, D, De2), wmap),
                  pl.BlockSpec((None, 1, De), wmap),
                  pl.BlockSpec((None, 1, De), wmap),
                  pl.BlockSpec((None, De, D), wmap),
                  pl.BlockSpec((None, 1, D), wmap)],
        out_specs=pl.BlockSpec((bm * ROW_TILE, V7X_LANES), rows),
        scratch_shapes=[pltpu.VMEM((D // V7X_LANES, FFN_PREP_COLS, V7X_LANES), F32),
                        pltpu.VMEM((De, D), BF16),
                        pltpu.VMEM((De, D), BF16),
                        pltpu.VMEM((De, D), BF16)],
    )
    return pl.pallas_call(
        _ffn_kernel,
        grid_spec=grid_spec,
        out_shape=jax.ShapeDtypeStruct((P * ROW_TILE, V7X_LANES), F32),
        compiler_params=_cparams(("arbitrary",)),
        name="expert_ffn",
    )(block_expert, n_used, xbuf, w_up, bg, bl, w_down, bd)


def _combine_kernel(dcur_ref, dnext_ref, gate_ref, x1_ref, mod_ref, y_hbm, o_ref, buf, sem):
    i = pl.program_id(0)
    n = pl.num_programs(0)
    tm = x1_ref.shape[0]
    slot = i % 2

    def issue(dref, s):
        def body(tb, c):
            for u in range(DISPATCH_UNROLL):
                t = tb * DISPATCH_UNROLL + u
                for kk in range(TOP_K):
                    d = dref[t * TOP_K + kk]
                    pltpu.make_async_copy(_row(y_hbm, d), _row(buf.at[s, kk], t), sem.at[s]).start(priority=kk % 2)
            return c
        lax.fori_loop(0, tm // DISPATCH_UNROLL, body, 0)

    @pl.when(i == 0)
    def _():
        issue(dcur_ref, 0)

    @pl.when(i + 1 < n)
    def _():
        issue(dnext_ref, 1 - slot)

    def drain(tb, c):
        for _ in range(DISPATCH_UNROLL * TOP_K):
            pltpu.make_async_copy(_row(y_hbm, 0), _row(buf.at[slot, 0], 0), sem.at[slot]).wait()
        return c

    lax.fori_loop(0, tm // DISPATCH_UNROLL, drain, 0)
    gate = gate_ref[...]
    acc = gate[:, 0:1] * _load_row_tiles(buf.at[slot, 0], tm)
    for kk in range(1, TOP_K):
        acc = acc + gate[:, kk:kk + 1] * _load_row_tiles(buf.at[slot, kk], tm)
    o_ref[...] = x1_ref[...] + mod_ref[5:6, :] * acc


def _combine(dest_flat, gate, x1, mod3, ybuf, S):
    T, D = x1.shape
    tm = TM_COMBINE
    per_b = S // tm
    nt = T // tm
    return pl.pallas_call(
        _combine_kernel,
        grid=(nt,),
        in_specs=[pl.BlockSpec((tm * TOP_K,), lambda i: (i,), memory_space=pltpu.SMEM),
                  pl.BlockSpec((tm * TOP_K,), lambda i: (jnp.minimum(i + 1, nt - 1),), memory_space=pltpu.SMEM),
                  pl.BlockSpec((tm, TOP_K), lambda i: (i, 0)),
                  pl.BlockSpec((tm, D), lambda i: (i, 0)),
                  pl.BlockSpec((None, 6, D), lambda i: (i // per_b, 0, 0)),
                  pl.BlockSpec(memory_space=pl.ANY)],
        out_specs=pl.BlockSpec((tm, D), lambda i: (i, 0)),
        out_shape=jax.ShapeDtypeStruct((T, D), F32),
        scratch_shapes=[pltpu.VMEM((2, TOP_K, tm * ROW_TILE, V7X_LANES), F32), pltpu.SemaphoreType.DMA((2,))],
        compiler_params=_cparams(("arbitrary",)),
        name="combine",
    )(dest_flat, dest_flat, gate, x1, mod3, ybuf)


def kernel(x, c, positions, w_ada, b_ada, norm1_gain, w_in, lower_bound_logits, hg_norm_gain, w_hg_branch,
           q_norm_gain, k_norm_gain, attn_sinks, w_attn_branch, w_out, norm2_gain, w_router, b_router,
           w_up, b_up, w_down, b_down):
    B, S, D = x.shape
    T = B * S
    assert w_ada.shape[0] == 1, "one layer"
    x2 = x.reshape(T, D)

    mod = _ada(c, w_ada[0], b_ada[0])
    mod3 = mod.reshape(B, 6, D)

    hg_in, at_in, gates, f_min = _inproj(x2, mod3, norm1_gain[0], w_in[0].astype(BF16), S)
    f_min = f_min[::V7X_SUBLANES, 0].reshape(B, S // TM_PROJ)
    hg_o = _hgrn(hg_in, f_min, lower_bound_logits, hg_norm_gain[0], B, S)
    at_o = _swa(at_in, positions, q_norm_gain[0], k_norm_gain[0], attn_sinks[0], B, S)

    x1, h2, idx, gate, rank, cnt = _merge_router(
        x2, hg_o, at_o, gates, mod3, w_hg_branch[0].astype(BF16), w_attn_branch[0].astype(BF16),
        w_out[0].astype(BF16), norm2_gain[0], w_router[0].astype(BF16), b_router[0], S)

    bm = MOE_BM
    counts = cnt[0].astype(I32)
    padded = (counts + bm - 1) // bm * bm
    pad_end = jnp.cumsum(padded)
    pad_start = pad_end - padded
    P = T * TOP_K + N_EXPERTS * bm
    n_blocks = P // bm
    block_start = jnp.arange(n_blocks, dtype=I32) * bm
    block_expert = jnp.minimum(
        jnp.sum((pad_end[None, :] <= block_start[:, None]).astype(I32), axis=1), N_EXPERTS - 1).astype(I32)
    n_used = (pad_end[-1:] // bm).astype(I32)
    tail = jnp.concatenate([n_used, n_blocks - n_used]).astype(I32)

    dest = _dest(idx, rank, pad_start)
    dest_flat = dest.reshape(T * TOP_K)
    xbuf = _dispatch((pad_start + counts).astype(I32), (padded - counts).astype(I32), tail, dest_flat, h2, P)

    ybuf = _ffn(block_expert, n_used, xbuf, w_up[0],
                b_up[0][:, None, 0::2], b_up[0][:, None, 1::2],
                w_down[0], b_down[0][:, None, :])

    out = _combine(dest_flat, gate, x1, mod3, ybuf, S)
    return out.reshape(B, S, D)
```

```python
import functools

import numpy as np
import jax
import jax.numpy as jnp
from jax import lax
from jax.experimental import pallas as pl
from jax.experimental.pallas import tpu as pltpu

F32 = jnp.float32
BF16 = jnp.bfloat16
I32 = jnp.int32

HG_HEADS = 4
HG_DIM = 128
HG_W = HG_HEADS * HG_DIM
ATT_Q_HEADS = 8
ATT_KV_HEADS = 2
ATT_GROUP = ATT_Q_HEADS // ATT_KV_HEADS
ATT_DH = 64
ATT_Q_W = ATT_Q_HEADS * ATT_DH
ATT_KV_W = ATT_KV_HEADS * ATT_DH
ATT_BLOCK = 128
ROPE_THETA = 500000.0
ROPE_DIM = ATT_DH // 4
N_EXPERTS = 32
TOP_K = 4
SWIGLU_ALPHA = 1.702
SWIGLU_LIMIT = 7.0
EPS = 1e-6

V7X_LANES = 128
V7X_SUBLANES = 8
V7X_VMEM_LIMIT_BYTES = 56 * 1024 * 1024

TM_PROJ = 256
TM_MERGE = 512
HG_TILE = 512
HG_CHUNK = 128
HG_SUB = 8
MOE_BM = 512
TM_DISPATCH = 256
TM_COMBINE = 128

NEG_INF = float("-inf")


def _cparams(sem, vmem=V7X_VMEM_LIMIT_BYTES):
    return pltpu.CompilerParams(dimension_semantics=sem, vmem_limit_bytes=vmem)


def _sigmoid(x):
    return 1.0 / (1.0 + jnp.exp(-x))


def _dot(a, b):
    return jnp.dot(a, b, preferred_element_type=F32)


def _dot_nt(a, b):
    return lax.dot_general(a, b, (((1,), (1,)), ((), ())), preferred_element_type=F32)


def _dot_tn(a, b):
    return lax.dot_general(a, b, (((0,), (0,)), ((), ())), preferred_element_type=F32)


ROW_TILE = V7X_SUBLANES


def _store_row_tiles(ref, val):
    rows = val.shape[0]
    for g in range(ROW_TILE):
        ref[pl.ds(g, rows, stride=ROW_TILE), :] = val[:, g * V7X_LANES:(g + 1) * V7X_LANES]


def _load_row_tiles(ref, rows):
    return jnp.concatenate([ref[pl.ds(g, rows, stride=ROW_TILE), :] for g in range(ROW_TILE)], axis=1)


def _dot_exact01(m01, x):
    hi = x.astype(BF16)
    r1 = x - hi.astype(F32)
    mid = r1.astype(BF16)
    lo = (r1 - mid.astype(F32)).astype(BF16)
    return _dot(m01, hi) + _dot(m01, mid) + _dot(m01, lo)


def _ada_kernel(c_ref, w_ref, b_ref, o_ref):
    c = c_ref[...]
    cond = c * _sigmoid(c)
    o_ref[...] = _dot(cond.astype(BF16), w_ref[...].astype(BF16)) + b_ref[...]


def _ada(c, w_ada, b_ada):
    B, D = c.shape
    N = w_ada.shape[1]
    tn = D
    return pl.pallas_call(
        _ada_kernel,
        grid=(N // tn,),
        in_specs=[pl.BlockSpec((B, D), lambda j: (0, 0)),
                  pl.BlockSpec((D, tn), lambda j: (0, j)),
                  pl.BlockSpec((1, tn), lambda j: (0, j))],
        out_specs=pl.BlockSpec((B, tn), lambda j: (0, j)),
        out_shape=jax.ShapeDtypeStruct((B, N), F32),
        compiler_params=_cparams(("parallel",)),
        name="ada",
    )(c, w_ada, b_ada.reshape(1, N))


def _inproj_kernel(x_ref, mod_ref, g_ref, w_ref, hg_ref, at_ref, gt_ref, fmin_ref):
    x = x_ref[...]
    ms = jnp.mean(x * x, axis=-1, keepdims=True)
    y = x * lax.rsqrt(ms + EPS) * g_ref[...]
    h = (y * (1.0 + mod_ref[1:2, :]) + mod_ref[0:1, :]).astype(BF16)
    n_hg = hg_ref.shape[1]
    n_at = at_ref.shape[1]
    n_gt = gt_ref.shape[1]
    hg = _dot(h, w_ref[:, 0:n_hg])
    hg_ref[...] = hg
    f_pre = hg[:, HG_W:2 * HG_W]
    f_min = jnp.min(jnp.min(f_pre, axis=-1, keepdims=True), axis=0, keepdims=True)
    fmin_ref[...] = jnp.broadcast_to(f_min, fmin_ref.shape)
    at_ref[...] = _dot(h, w_ref[:, n_hg:n_hg + n_at])
    gt_ref[...] = _dot(h, w_ref[:, n_hg + n_at:n_hg + n_at + n_gt])


def _inproj(x2, mod3, gain, w_in_bf, S):
    T, D = x2.shape
    tm = TM_PROJ
    n_hg = 4 * HG_W
    n_at = ATT_Q_W + 2 * ATT_KV_W
    n_gt = 2 * D
    assert w_in_bf.shape[1] == n_hg + n_at + n_gt
    per_b = S // tm
    return pl.pallas_call(
        _inproj_kernel,
        grid=(T // tm,),
        in_specs=[pl.BlockSpec((tm, D), lambda i: (i, 0)),
                  pl.BlockSpec((None, 6, D), lambda i: (i // per_b, 0, 0)),
                  pl.BlockSpec((1, D), lambda i: (0, 0)),
                  pl.BlockSpec(w_in_bf.shape, lambda i: (0, 0))],
        out_specs=[pl.BlockSpec((tm, n_hg), lambda i: (i, 0)),
                   pl.BlockSpec((tm, n_at), lambda i: (i, 0)),
                   pl.BlockSpec((tm, n_gt), lambda i: (i, 0)),
                   pl.BlockSpec((V7X_SUBLANES, V7X_LANES), lambda i: (i, 0))],
        out_shape=[jax.ShapeDtypeStruct((T, n_hg), F32),
                   jax.ShapeDtypeStruct((T, n_at), F32),
                   jax.ShapeDtypeStruct((T, n_gt), F32),
                   jax.ShapeDtypeStruct((T // tm * V7X_SUBLANES, V7X_LANES), F32)],
        compiler_params=_cparams(("parallel",)),
        name="inproj",
    )(x2, mod3, gain.reshape(1, D), w_in_bf)


def _hgrn_constants():
    C = HG_CHUNK
    tri = (np.arange(C)[None, :] <= np.arange(C)[:, None]).astype(np.float32)
    halves = []
    lh = C // 2
    while lh >= HG_SUB:
        halves.append(lh)
        lh //= 2
    code = np.zeros((C, C), np.int32)
    ii, jj = np.meshgrid(np.arange(C), np.arange(C), indexing="ij")
    for li, lh in enumerate(halves):
        sel = ((ii // (2 * lh)) == (jj // (2 * lh))) & ((ii % (2 * lh)) >= lh) & ((jj % (2 * lh)) < lh)
        code[sel] = li + 1
    return tri, code, halves


HG_MILD_FMIN = -7.0


def _hgrn_kernel(mild_ref, in_ref, lbl_ref, gain_ref, tri_ref, code_ref, wsum_ref, o_ref, st_ref, *, halves):
    C = HG_CHUNK
    W = HG_W
    n_chunks = in_ref.shape[1] // C
    step_is_mild = mild_ref[pl.program_id(0) * pl.num_programs(1) + pl.program_id(1)] == 1

    @pl.when(pl.program_id(1) == 0)
    def _():
        st_ref[...] = jnp.zeros_like(st_ref)

    l0 = lbl_ref[0:1, :]
    l1 = lbl_ref[1:2, :]
    mx = jnp.maximum(l0, l1)
    e0 = jnp.exp(l0 - mx)
    lb = e0 / (e0 + jnp.exp(l1 - mx))
    gain = gain_ref[...]
    tri = tri_ref[...].astype(BF16)
    code = code_ref[...]
    wsum = wsum_ref[...]
    row = lax.broadcasted_iota(I32, (C, C), 0)
    col = lax.broadcasted_iota(I32, (C, C), 1)
    row_in_sub = lax.broadcasted_iota(I32, (C, HG_DIM), 0) % HG_SUB
    same_sub = (row // HG_SUB) == (col // HG_SUB)
    sub_causal = same_sub & (col <= row)

    def group_row(a, group, r):
        a3 = a.reshape(C // group, group, a.shape[1])
        return jnp.broadcast_to(a3[:, r:r + 1, :], a3.shape).reshape(a.shape)

    def bcast_sub(a, j):
        return group_row(a, HG_SUB, j)

    def chunk(mild, c, carry):
        r0 = pl.multiple_of(c * C, C)
        rows = pl.ds(r0, C)
        for bb in range(in_ref.shape[0]):
            qp = in_ref[bb, rows, 0:W]
            fp = in_ref[bb, rows, W:2 * W]
            vv = in_ref[bb, rows, 2 * W:3 * W]
            gp = in_ref[bb, rows, 3 * W:4 * W]
            forget = lb + (1.0 - lb) * _sigmoid(fp)
            q = qp * _sigmoid(qp)
            k = 1.0 - forget
            lf = jnp.log(forget)
            b = _dot_exact01(tri, lf)
            e_b = jnp.exp(b)
            e_st = jnp.exp(group_row(b, C, C - 1) - b)
            e_lv = [jnp.exp(-jnp.abs(b - group_row(b, 2 * lh, lh - 1))) for lh in halves]
            if mild:
                x_sub = group_row(b, HG_SUB, HG_SUB - 1) - b
                e_subk = jnp.exp(x_sub)
                e_subq = jnp.exp(-x_sub)
            outs = []
            for h in range(HG_HEADS):
                ls = slice(h * HG_DIM, (h + 1) * HG_DIM)
                qh, kh, vh = q[:, ls], k[:, ls], vv[:, ls]
                bh = b[:, ls]
                st = st_ref[bb, h]
                o = _dot_nt((qh * e_b[:, ls]).astype(BF16), st.astype(BF16))
                if mild:
                    s = _dot_nt((qh * e_subq[:, ls]).astype(BF16), (kh * e_subk[:, ls]).astype(BF16))
                    scores = jnp.where(sub_causal, s, 0.0)
                else:
                    ps = []
                    for j in range(HG_SUB):
                        d = jnp.where(row_in_sub >= j, bh - bcast_sub(bh, j), NEG_INF)
                        ps.append((qh * bcast_sub(kh, j) * jnp.exp(d)).astype(BF16))
                    scores = jnp.where(same_sub, _dot(jnp.concatenate(ps, axis=1), wsum), 0.0)
                for li in range(len(halves)):
                    e = e_lv[li][:, ls]
                    s = _dot_nt((qh * e).astype(BF16), (kh * e).astype(BF16))
                    scores = jnp.where(code == li + 1, s, scores)
                o = o + _dot(scores.astype(BF16), vh.astype(BF16))
                kst = (kh * e_st[:, ls]).astype(BF16)
                st_ref[bb, h] = st * e_b[C - 1:C, ls] + _dot_tn(vh.astype(BF16), kst)
                ms = jnp.mean(o * o, axis=-1, keepdims=True)
                on = o * lax.rsqrt(ms + EPS) * gain
                gh = gp[:, ls]
                outs.append(on * (gh * _sigmoid(gh)))
            o_ref[bb, rows, :] = jnp.concatenate(outs, axis=-1).astype(o_ref.dtype)
        return carry

    @pl.when(step_is_mild)
    def _():
        lax.fori_loop(0, n_chunks, functools.partial(chunk, True), 0)

    @pl.when(jnp.logical_not(step_is_mild))
    def _():
        lax.fori_loop(0, n_chunks, functools.partial(chunk, False), 0)


HG_SEQS = 4


def _hgrn(hg_in, f_min, lb_logits, gain, B, S):
    T = hg_in.shape[0]
    lt = HG_TILE
    tri, code, halves = _hgrn_constants()
    nseq = HG_SEQS if B % HG_SEQS == 0 else 1
    step_min = jnp.min(f_min.reshape(B // nseq, nseq, S // lt, lt // TM_PROJ), axis=(1, 3))
    mild = (step_min >= HG_MILD_FMIN).astype(I32).reshape(-1)
    wsum = (np.arange(HG_SUB * HG_DIM)[:, None] // HG_DIM == np.arange(HG_CHUNK)[None, :] % HG_SUB).astype(np.float32)
    const = lambda b, s, m: (0, 0)
    grid_spec = pltpu.PrefetchScalarGridSpec(
        num_scalar_prefetch=1,
        grid=(B // nseq, S // lt),
        in_specs=[pl.BlockSpec((nseq, lt, 4 * HG_W), lambda b, s, m: (b, s, 0)),
                  pl.BlockSpec(lb_logits.shape, const),
                  pl.BlockSpec((1, HG_DIM), const),
                  pl.BlockSpec(tri.shape, const),
                  pl.BlockSpec(code.shape, const),
                  pl.BlockSpec(wsum.shape, const)],
        out_specs=pl.BlockSpec((nseq, lt, HG_W), lambda b, s, m: (b, s, 0)),
        scratch_shapes=[pltpu.VMEM((nseq, HG_HEADS, HG_DIM, HG_DIM), F32)],
    )
    out = pl.pallas_call(
        functools.partial(_hgrn_kernel, halves=tuple(halves)),
        grid_spec=grid_spec,
        out_shape=jax.ShapeDtypeStruct((B, S, HG_W), BF16),
        compiler_params=_cparams(("parallel", "arbitrary")),
        name="hgrn2",
    )(mild, hg_in.reshape(B, S, 4 * HG_W), lb_logits, gain.reshape(1, HG_DIM), jnp.asarray(tri),
      jnp.asarray(code), jnp.asarray(wsum, dtype=BF16))
    return out.reshape(T, HG_W)


SWA_QBLOCKS = 4
ROPE_ROWS = 16


def _swa_constants():
    lane = np.arange(V7X_LANES)
    c = lane % ATT_DH
    half = ROPE_DIM // 2
    inv_freq = ROPE_THETA ** (-jnp.arange(half, dtype=F32) / half)
    freq_rows = jnp.broadcast_to(
        jnp.concatenate([inv_freq, jnp.zeros((ROPE_ROWS - half,), F32)])[:, None], (ROPE_ROWS, V7X_LANES))
    sel = ((np.arange(ROPE_ROWS)[:, None] == (c % half)[None, :]) & (c < ROPE_DIM)[None, :]).astype(np.float32)
    sign = np.where(c < half, -1.0, np.where(c < ROPE_DIM, 1.0, 0.0)).astype(np.float32)
    first = (c < half).astype(np.float32)
    tab = np.stack([sign, first], axis=0)
    g = (lane[:, None] // ATT_DH == lane[None, :] // ATT_DH).astype(np.float32) / ATT_DH
    return freq_rows, sel, tab, g


def _swa_kernel(sink_ref, cur_ref, prev_ref, pcur_ref, pprev_ref, qg_ref, kg_ref, freq_ref, sel_ref, tab_ref, g_ref,
                o_ref):
    Bq = ATT_BLOCK
    n = pl.program_id(1)
    tab = tab_ref[...]
    sign, first = tab[0:1, :], tab[1:2, :]
    gmat = g_ref[...].astype(BF16)
    sel = sel_ref[...].astype(BF16)
    freq_rows = freq_ref[...]
    half = ROPE_DIM // 2
    scale = ATT_DH ** -0.5

    def group_ms(x):
        sq = x * x
        hi = sq.astype(BF16)
        lo = (sq - hi.astype(F32)).astype(BF16)
        return _dot(hi, gmat) + _dot(lo, gmat)

    def spread(a):
        hi = a.astype(BF16)
        r1 = a - hi.astype(F32)
        mid = r1.astype(BF16)
        lo = (r1 - mid.astype(F32)).astype(BF16)
        return _dot_tn(hi, sel) + _dot_tn(mid, sel) + _dot_tn(lo, sel)

    def rope_table(pos_row):
        ang = freq_rows * pos_row
        return 1.0 + spread(jnp.cos(ang) - 1.0), spread(jnp.sin(ang)) * sign

    def norm_rope(x, gain, cs_sn):
        y = x * lax.rsqrt(group_ms(x) + EPS) * gain
        partner = jnp.where(first > 0.5, pltpu.roll(y, V7X_LANES - half, 1), pltpu.roll(y, half, 1))
        return y * cs_sn[0] + partner * cs_sn[1]

    kq = ATT_Q_W
    lane = lax.broadcasted_iota(I32, (Bq, V7X_LANES), 1)
    lo_half = lane < ATT_DH

    def pad_variants(a):
        r = pltpu.roll(a, ATT_DH, 1)
        z = jnp.zeros_like(a)
        return [[jnp.where(lo_half, a, z).astype(BF16), jnp.where(lo_half, z, r).astype(BF16)],
                [jnp.where(lo_half, r, z).astype(BF16), jnp.where(lo_half, z, a).astype(BF16)]]

    tables = [rope_table(pprev_ref[0].astype(F32))]
    kblocks = [pad_variants(norm_rope(prev_ref[:, 0:ATT_KV_W], kg_ref[...], tables[0]))]
    vblocks = [pad_variants(prev_ref[:, ATT_KV_W:2 * ATT_KV_W])]
    for j in range(SWA_QBLOCKS):
        rows = slice(j * Bq, (j + 1) * Bq)
        tables.append(rope_table(pcur_ref[j].astype(F32)))
        kblocks.append(pad_variants(norm_rope(cur_ref[rows, kq:kq + ATT_KV_W], kg_ref[...], tables[j + 1])))
        vblocks.append(pad_variants(cur_ref[rows, kq + ATT_KV_W:kq + 2 * ATT_KV_W]))

    qi = lax.broadcasted_iota(I32, (Bq, 2 * Bq), 0)
    kj = lax.broadcasted_iota(I32, (Bq, 2 * Bq), 1)
    in_band = ((kj < Bq) & (kj > qi)) | ((kj >= Bq) & ((kj - Bq) <= qi))
    first_of_seq = (jnp.zeros((Bq, 2 * Bq), I32) + n) == 0
    for j in range(SWA_QBLOCKS):
        rows = slice(j * Bq, (j + 1) * Bq)
        mask = (in_band & jnp.logical_not(first_of_seq & (kj < Bq))) if j == 0 else in_band
        for t in range(ATT_Q_W // V7X_LANES):
            ls = slice(t * V7X_LANES, (t + 1) * V7X_LANES)
            qt = (norm_rope(cur_ref[rows, ls], qg_ref[...], tables[j + 1]) * scale).astype(BF16)
            acc = jnp.zeros((Bq, V7X_LANES), F32)
            for u in range(2):
                head = 2 * t + u
                kvh = head // ATT_GROUP
                kcat = jnp.concatenate([kblocks[j][kvh][u], kblocks[j + 1][kvh][u]], axis=0)
                vcat = jnp.concatenate([vblocks[j][kvh][u], vblocks[j + 1][kvh][u]], axis=0)
                s = jnp.where(mask, _dot_nt(qt, kcat), NEG_INF)
                sink = sink_ref[head]
                m = jnp.maximum(jnp.max(s, axis=-1, keepdims=True), sink)
                p = jnp.exp(s - m)
                denom = jnp.sum(p, axis=-1, keepdims=True) + jnp.exp(sink - m)
                acc = acc + _dot(p.astype(BF16), vcat) * (1.0 / denom)
            o_ref[rows, ls] = acc.astype(o_ref.dtype)


def _swa(at_in, positions, q_gain, k_gain, sinks, B, S):
    T = at_in.shape[0]
    nb = S // ATT_BLOCK
    qb = SWA_QBLOCKS
    assert nb % qb == 0
    steps = nb // qb
    freq_rows, sel, tab, g = _swa_constants()
    qg = jnp.tile(q_gain.reshape(1, ATT_DH), (1, V7X_LANES // ATT_DH))
    kg = jnp.tile(k_gain.reshape(1, ATT_DH), (1, V7X_LANES // ATT_DH))
    pos3 = positions.reshape(B * nb, 1, ATT_BLOCK)
    n_at = at_in.shape[1]
    kv_blk = 2 * ATT_KV_W
    assert ATT_Q_W % kv_blk == 0
    prev_blk = lambda b, n: b * nb + jnp.maximum(qb * n - 1, 0)
    const = lambda b, n: (0, 0)
    return pl.pallas_call(
        _swa_kernel,
        grid=(B, steps),
        in_specs=[pl.BlockSpec(memory_space=pltpu.SMEM),
                  pl.BlockSpec((qb * ATT_BLOCK, n_at), lambda b, n: (b * steps + n, 0)),
                  pl.BlockSpec((ATT_BLOCK, kv_blk), lambda b, n: (prev_blk(b, n), ATT_Q_W // kv_blk)),
                  pl.BlockSpec((qb, 1, ATT_BLOCK), lambda b, n: (b * steps + n, 0, 0)),
                  pl.BlockSpec((1, 1, ATT_BLOCK), lambda b, n: (prev_blk(b, n), 0, 0)),
                  pl.BlockSpec((1, V7X_LANES), const),
                  pl.BlockSpec((1, V7X_LANES), const),
                  pl.BlockSpec(freq_rows.shape, const),
                  pl.BlockSpec(sel.shape, const),
                  pl.BlockSpec(tab.shape, const),
                  pl.BlockSpec(g.shape, const)],
        out_specs=pl.BlockSpec((qb * ATT_BLOCK, ATT_Q_W), lambda b, n: (b * steps + n, 0)),
        out_shape=jax.ShapeDtypeStruct((T, ATT_Q_W), BF16),
        compiler_params=_cparams(("parallel", "parallel")),
        name="swa",
    )(sinks, at_in, at_in, pos3, pos3, qg, kg, freq_rows, jnp.asarray(sel), jnp.asarray(tab), jnp.asarray(g))


def _merge_router_kernel(x_ref, hg_ref, at_ref, gt_ref, mod_ref, whg_ref, wat_ref, wout_ref, g2_ref,
                         wr_ref, br_ref, tri_ref,
                         x1_ref, h2_ref, idx_ref, gate_ref, rank_ref, cnt_ref, run_ref):
    i = pl.program_id(0)
    D = x_ref.shape[1]
    tm = x_ref.shape[0]

    @pl.when(i == 0)
    def _():
        run_ref[...] = jnp.zeros_like(run_ref)

    y_h = _dot(hg_ref[...], whg_ref[...])
    y_a = _dot(at_ref[...], wat_ref[...])
    merged = _sigmoid(gt_ref[:, 0:D]) * y_h + _sigmoid(gt_ref[:, D:2 * D]) * y_a
    x1 = x_ref[...] + mod_ref[2:3, :] * _dot(merged.astype(BF16), wout_ref[...])
    x1_ref[...] = x1
    ms = jnp.mean(x1 * x1, axis=-1, keepdims=True)
    h2 = x1 * lax.rsqrt(ms + EPS) * g2_ref[...] * (1.0 + mod_ref[4:5, :]) + mod_ref[3:4, :]
    _store_row_tiles(h2_ref, h2)
    logits = _dot(h2.astype(BF16), wr_ref[...]) + br_ref[...]
    E = logits.shape[1]
    lane = lax.broadcasted_iota(I32, (tm, E), 1).astype(F32)
    vals, idxs = [], []
    l = logits
    for _ in range(TOP_K):
        m = jnp.max(l, axis=-1, keepdims=True)
        ik = jnp.min(jnp.where(l == m, lane, float(E)), axis=-1, keepdims=True)
        vals.append(m)
        idxs.append(ik)
        l = jnp.where(lane == ik, NEG_INF, l)
    ex = [jnp.exp(v - vals[0]) for v in vals]
    den = ex[0]
    for e in ex[1:]:
        den = den + e
    onehot = jnp.zeros((tm, E), F32)
    for ik in idxs:
        onehot = onehot + (lane == ik).astype(F32)
    cum = _dot(tri_ref[...], onehot.astype(BF16))
    run = run_ref[0:1, 0:E]
    excl = cum - onehot + run
    lane_k = lax.broadcasted_iota(I32, (tm, TOP_K), 1)
    idx_o = jnp.zeros((tm, TOP_K), I32)
    gate_o = jnp.zeros((tm, TOP_K), F32)
    rank_o = jnp.zeros((tm, TOP_K), I32)
    for kk in range(TOP_K):
        rk = jnp.sum(jnp.where(lane == idxs[kk], excl, 0.0), axis=-1, keepdims=True)
        idx_o = jnp.where(lane_k == kk, idxs[kk].astype(I32), idx_o)
        gate_o = jnp.where(lane_k == kk, ex[kk] / den, gate_o)
        rank_o = jnp.where(lane_k == kk, rk.astype(I32), rank_o)
    idx_ref[...] = idx_o
    gate_ref[...] = gate_o
    rank_ref[...] = rank_o
    new_run = run + cum[tm - 1:tm, :]
    run_ref[0:1, 0:E] = new_run
    cnt_ref[...] = jnp.broadcast_to(new_run, cnt_ref.shape)


def _merge_router(x2, hg_o, at_o, gates, mod3, whg, wat, wout, g2, wr, br, S):
    T, D = x2.shape
    tm = TM_MERGE
    per_b = S // tm
    E = wr.shape[1]
    tri = jnp.asarray(np.tril(np.ones((tm, tm), np.float32)), dtype=BF16)
    row = lambda i: (i, 0)
    const = lambda i: (0, 0)
    return pl.pallas_call(
        _merge_router_kernel,
        grid=(T // tm,),
        in_specs=[pl.BlockSpec((tm, D), row),
                  pl.BlockSpec((tm, HG_W), row),
                  pl.BlockSpec((tm, ATT_Q_W), row),
                  pl.BlockSpec((tm, 2 * D), row),
                  pl.BlockSpec((None, 6, D), lambda i: (i // per_b, 0, 0)),
                  pl.BlockSpec(whg.shape, const),
                  pl.BlockSpec(wat.shape, const),
                  pl.BlockSpec(wout.shape, const),
                  pl.BlockSpec((1, D), const),
                  pl.BlockSpec(wr.shape, const),
                  pl.BlockSpec((1, E), const),
                  pl.BlockSpec((tm, tm), const)],
        out_specs=[pl.BlockSpec((tm, D), row),
                   pl.BlockSpec((tm * ROW_TILE, V7X_LANES), row),
                   pl.BlockSpec((tm, TOP_K), row),
                   pl.BlockSpec((tm, TOP_K), row),
                   pl.BlockSpec((tm, TOP_K), row),
                   pl.BlockSpec((V7X_SUBLANES, E), const)],
        out_shape=[jax.ShapeDtypeStruct((T, D), F32),
                   jax.ShapeDtypeStruct((T * ROW_TILE, V7X_LANES), F32),
                   jax.ShapeDtypeStruct((T, TOP_K), I32),
                   jax.ShapeDtypeStruct((T, TOP_K), F32),
                   jax.ShapeDtypeStruct((T, TOP_K), I32),
                   jax.ShapeDtypeStruct((V7X_SUBLANES, E), F32)],
        scratch_shapes=[pltpu.VMEM((V7X_SUBLANES, V7X_LANES), F32)],
        compiler_params=_cparams(("arbitrary",)),
        name="merge_router",
    )(x2, hg_o, at_o, gates, mod3, whg, wat, wout, g2.reshape(1, D), wr, br.reshape(1, E), tri)


def _dest_kernel(idx_ref, rank_ref, ps_ref, o_ref):
    idx = idx_ref[...]
    tm = idx.shape[0]
    E = ps_ref.shape[1]
    lane = lax.broadcasted_iota(I32, (tm, E), 1)
    lane_k = lax.broadcasted_iota(I32, (tm, TOP_K), 1)
    ps = ps_ref[...].astype(F32)
    out = rank_ref[...]
    for kk in range(TOP_K):
        start = jnp.sum(jnp.where(lane == idx[:, kk:kk + 1], ps, 0.0), axis=-1, keepdims=True)
        out = out + jnp.where(lane_k == kk, start.astype(I32), 0)
    o_ref[...] = out


def _dest(idx, rank, pad_start):
    T = idx.shape[0]
    tm = 1024
    E = pad_start.shape[0]
    row = lambda i: (i, 0)
    return pl.pallas_call(
        _dest_kernel,
        grid=(T // tm,),
        in_specs=[pl.BlockSpec((tm, TOP_K), row), pl.BlockSpec((tm, TOP_K), row),
                  pl.BlockSpec((1, E), lambda i: (0, 0))],
        out_specs=pl.BlockSpec((tm, TOP_K), row),
        out_shape=jax.ShapeDtypeStruct((T, TOP_K), I32),
        compiler_params=_cparams(("parallel",)),
        name="dest",
    )(idx, rank, pad_start.reshape(1, E))


DISPATCH_UNROLL = 4


def _row(ref, r):
    return ref.at[pl.ds(pl.multiple_of(r * ROW_TILE, ROW_TILE), ROW_TILE)]


def _dispatch_kernel(fill_start_ref, fill_n_ref, tail_ref, dest_ref, h_ref, xbuf_ref, zero_ref, sem):
    tm = h_ref.shape[0] // ROW_TILE
    zrows = zero_ref.shape[0]

    def zero_row_copy(r):
        return pltpu.make_async_copy(_row(zero_ref, 0), _row(xbuf_ref, r), sem.at[1])

    def zero_block_copy(b):
        dst = xbuf_ref.at[pl.ds(pl.multiple_of(b * zrows, zrows), zrows)]
        return pltpu.make_async_copy(zero_ref, dst, sem.at[2])

    @pl.when(pl.program_id(0) == 0)
    def _():
        zero_ref[...] = jnp.zeros_like(zero_ref)

        def per_expert(start):
            def body(e, c):
                def rows(r, c2):
                    cp = zero_row_copy(fill_start_ref[e] + r)
                    cp.start() if start else cp.wait()
                    return c2
                return lax.fori_loop(0, fill_n_ref[e], rows, c)
            return body

        def tail(start):
            def body(b, c):
                cp = zero_block_copy(tail_ref[0] + b)
                cp.start() if start else cp.wait()
                return c
            return body

        lax.fori_loop(0, N_EXPERTS, per_expert(True), 0)
        lax.fori_loop(0, tail_ref[1], tail(True), 0)
        lax.fori_loop(0, N_EXPERTS, per_expert(False), 0)
        lax.fori_loop(0, tail_ref[1], tail(False), 0)

    def issue(tb, c):
        for u in range(DISPATCH_UNROLL):
            t = tb * DISPATCH_UNROLL + u
            src = _row(h_ref, t)
            for kk in range(TOP_K):
                d = dest_ref[t * TOP_K + kk]
                pltpu.make_async_copy(src, _row(xbuf_ref, d), sem.at[0]).start(priority=kk % 2)
        return c

    lax.fori_loop(0, tm // DISPATCH_UNROLL, issue, 0)

    def drain(tb, c):
        for _ in range(DISPATCH_UNROLL * TOP_K):
            pltpu.make_async_copy(_row(h_ref, 0), _row(xbuf_ref, 0), sem.at[0]).wait()
        return c

    lax.fori_loop(0, tm // DISPATCH_UNROLL, drain, 0)


def _dispatch(fill_start, fill_n, tail, dest_flat, h2t, P):
    T = h2t.shape[0] // ROW_TILE
    tm = TM_DISPATCH
    grid_spec = pltpu.PrefetchScalarGridSpec(
        num_scalar_prefetch=3,
        grid=(T // tm,),
        in_specs=[pl.BlockSpec((tm * TOP_K,), lambda i, *_: (i,), memory_space=pltpu.SMEM),
                  pl.BlockSpec((tm * ROW_TILE, V7X_LANES), lambda i, *_: (i, 0))],
        out_specs=pl.BlockSpec(memory_space=pl.ANY),
        scratch_shapes=[pltpu.VMEM((MOE_BM * ROW_TILE, V7X_LANES), F32), pltpu.SemaphoreType.DMA((3,))],
    )
    return pl.pallas_call(
        _dispatch_kernel,
        grid_spec=grid_spec,
        out_shape=jax.ShapeDtypeStruct((P * ROW_TILE, V7X_LANES), F32),
        compiler_params=_cparams(("arbitrary",)),
        name="dispatch",
    )(fill_start, fill_n, tail, dest_flat, h2t)


FFN_PREP_COLS = 256


def _ffn_kernel(be_ref, nu_ref, x_ref, wu_ref, bg_ref, bl_ref, wd_ref, bd_ref, y_ref,
                t_ref, wgt_ref, wlt_ref, wdb_ref):
    i = pl.program_id(0)
    bm = x_ref.shape[0] // ROW_TILE
    D, De2 = wu_ref.shape
    n_slab = D // V7X_LANES
    half = FFN_PREP_COLS // 2
    used = i < nu_ref[0]
    new_expert = (i == 0) | (be_ref[i] != be_ref[jnp.maximum(i - 1, 0)])

    @pl.when(used & new_expert)
    def _():
        for c in range(De2 // FFN_PREP_COLS):
            tt = wu_ref[:, c * FFN_PREP_COLS:(c + 1) * FFN_PREP_COLS].T
            for s in range(n_slab):
                t_ref[s] = tt[:, s * V7X_LANES:(s + 1) * V7X_LANES]
            for s in range(n_slab):
                ls = slice(s * V7X_LANES, (s + 1) * V7X_LANES)
                wgt_ref[c * half:(c + 1) * half, ls] = t_ref[s, pl.ds(0, half, stride=2), :].astype(BF16)
                wlt_ref[c * half:(c + 1) * half, ls] = t_ref[s, pl.ds(1, half, stride=2), :].astype(BF16)
        wdb_ref[...] = wd_ref[...].astype(BF16)

    @pl.when(used)
    def _():
        x = _load_row_tiles(x_ref, bm).astype(BF16)
        glu = _dot_nt(x, wgt_ref[...]) + bg_ref[...]
        lin = _dot_nt(x, wlt_ref[...]) + bl_ref[...]
        glu = jnp.minimum(glu, SWIGLU_LIMIT)
        lin = jnp.clip(lin, -SWIGLU_LIMIT, SWIGLU_LIMIT)
        act = glu * _sigmoid(SWIGLU_ALPHA * glu) * (lin + 1.0)
        _store_row_tiles(y_ref, _dot(act.astype(BF16), wdb_ref[...]) + bd_ref[...])

    @pl.when(jnp.logical_not(used))
    def _():
        y_ref[...] = jnp.zeros_like(y_ref)


def _ffn(block_expert, n_used, xbuf, w_up, bg, bl, w_down, bd):
    P = xbuf.shape[0] // ROW_TILE
    bm = MOE_BM
    _, D, De2 = w_up.shape
    De = De2 // 2
    wmap = lambda i, be, nu: (be[i], 0, 0)
    rows = lambda i, be, nu: (i, 0)
    grid_spec = pltpu.PrefetchScalarGridSpec(
        num_scalar_prefetch=2,
        grid=(P // bm,),
        in_specs=[pl.BlockSpec((bm * ROW_TILE, V7X_LANES), rows),
                  pl.BlockSpec((None, D, De2), wmap),
                  pl.BlockSpec((None, 1, De), wmap),
                  pl.BlockSpec((None, 1, De), wmap),
                  pl.BlockSpec((None, De, D), wmap),
                  pl.BlockSpec((None, 1, D), wmap)],
        out_specs=pl.BlockSpec((bm * ROW_TILE, V7X_LANES), rows),
        scratch_shapes=[pltpu.VMEM((D // V7X_LANES, FFN_PREP_COLS, V7X_LANES), F32),
                        pltpu.VMEM((De, D), BF16),
                        pltpu.VMEM((De, D), BF16),
                        pltpu.VMEM((De, D), BF16)],
    )
    return pl.pallas_call(
        _ffn_kernel,
        grid_spec=grid_spec,
        out_shape=jax.ShapeDtypeStruct((P * ROW_TILE, V7X_LANES), F32),
        compiler_params=_cparams(("arbitrary",)),
        name="expert_ffn",
    )(block_expert, n_used, xbuf, w_up, bg, bl, w_down, bd)


def _combine_kernel(dcur_ref, dnext_ref, gate_ref, x1_ref, mod_ref, y_hbm, o_ref, buf, sem):
    i = pl.program_id(0)
    n = pl.num_programs(0)
    tm = x1_ref.shape[0]
    slot = i % 2

    def issue(dref, s):
        def body(tb, c):
            for u in range(DISPATCH_UNROLL):
                t = tb * DISPATCH_UNROLL + u
                for kk in range(TOP_K):
                    d = dref[t * TOP_K + kk]
                    pltpu.make_async_copy(_row(y_hbm, d), _row(buf.at[s, kk], t), sem.at[s]).start(priority=kk % 2)
            return c
        lax.fori_loop(0, tm // DISPATCH_UNROLL, body, 0)

    @pl.when(i == 0)
    def _():
        issue(dcur_ref, 0)

    @pl.when(i + 1 < n)
    def _():
        issue(dnext_ref, 1 - slot)

    def drain(tb, c):
        for _ in range(DISPATCH_UNROLL * TOP_K):
            pltpu.make_async_copy(_row(y_hbm, 0), _row(buf.at[slot, 0], 0), sem.at[slot]).wait()
        return c

    lax.fori_loop(0, tm // DISPATCH_UNROLL, drain, 0)
    gate = gate_ref[...]
    acc = gate[:, 0:1] * _load_row_tiles(buf.at[slot, 0], tm)
    for kk in range(1, TOP_K):
        acc = acc + gate[:, kk:kk + 1] * _load_row_tiles(buf.at[slot, kk], tm)
    o_ref[...] = x1_ref[...] + mod_ref[5:6, :] * acc


def _combine(dest_flat, gate, x1, mod3, ybuf, S):
    T, D = x1.shape
    tm = TM_COMBINE
    per_b = S // tm
    nt = T // tm
    return pl.pallas_call(
        _combine_kernel,
        grid=(nt,),
        in_specs=[pl.BlockSpec((tm * TOP_K,), lambda i: (i,), memory_space=pltpu.SMEM),
                  pl.BlockSpec((tm * TOP_K,), lambda i: (jnp.minimum(i + 1, nt - 1),), memory_space=pltpu.SMEM),
                  pl.BlockSpec((tm, TOP_K), lambda i: (i, 0)),
                  pl.BlockSpec((tm, D), lambda i: (i, 0)),
                  pl.BlockSpec((None, 6, D), lambda i: (i // per_b, 0, 0)),
                  pl.BlockSpec(memory_space=pl.ANY)],
        out_specs=pl.BlockSpec((tm, D), lambda i: (i, 0)),
        out_shape=jax.ShapeDtypeStruct((T, D), F32),
        scratch_shapes=[pltpu.VMEM((2, TOP_K, tm * ROW_TILE, V7X_LANES), F32), pltpu.SemaphoreType.DMA((2,))],
        compiler_params=_cparams(("arbitrary",)),
        name="combine",
    )(dest_flat, dest_flat, gate, x1, mod3, ybuf)


def kernel(x, c, positions, w_ada, b_ada, norm1_gain, w_in, lower_bound_logits, hg_norm_gain, w_hg_branch,
           q_norm_gain, k_norm_gain, attn_sinks, w_attn_branch, w_out, norm2_gain, w_router, b_router,
           w_up, b_up, w_down, b_down):
    B, S, D = x.shape
    T = B * S
    assert w_ada.shape[0] == 1, "one layer"
    x2 = x.reshape(T, D)

    mod = _ada(c, w_ada[0], b_ada[0])
    mod3 = mod.reshape(B, 6, D)

    hg_in, at_in, gates, f_min = _inproj(x2, mod3, norm1_gain[0], w_in[0].astype(BF16), S)
    f_min = f_min[::V7X_SUBLANES, 0].reshape(B, S // TM_PROJ)
    hg_o = _hgrn(hg_in, f_min, lower_bound_logits, hg_norm_gain[0], B, S)
    at_o = _swa(at_in, positions, q_norm_gain[0], k_norm_gain[0], attn_sinks[0], B, S)

    x1, h2, idx, gate, rank, cnt = _merge_router(
        x2, hg_o, at_o, gates, mod3, w_hg_branch[0].astype(BF16), w_attn_branch[0].astype(BF16),
        w_out[0].astype(BF16), norm2_gain[0], w_router[0].astype(BF16), b_router[0], S)

    bm = MOE_BM
    counts = cnt[0].astype(I32)
    padded = (counts + bm - 1) // bm * bm
    pad_end = jnp.cumsum(padded)
    pad_start = pad_end - padded
    P = T * TOP_K + N_EXPERTS * bm
    n_blocks = P // bm
    block_start = jnp.arange(n_blocks, dtype=I32) * bm
    block_expert = jnp.minimum(
        jnp.sum((pad_end[None, :] <= block_start[:, None]).astype(I32), axis=1), N_EXPERTS - 1).astype(I32)
    n_used = (pad_end[-1:] // bm).astype(I32)
    tail = jnp.concatenate([n_used, n_blocks - n_used]).astype(I32)

    dest = _dest(idx, rank, pad_start)
    dest_flat = dest.reshape(T * TOP_K)
    xbuf = _dispatch((pad_start + counts).astype(I32), (padded - counts).astype(I32), tail, dest_flat, h2, P)

    ybuf = _ffn(block_expert, n_used, xbuf, w_up[0],
                b_up[0][:, None, 0::2], b_up[0][:, None, 1::2],
                w_down[0], b_down[0][:, None, :])

    out = _combine(dest_flat, gate, x1, mod3, ybuf, S)
    return out.reshape(B, S, D)
```

```python
import functools

import numpy as np
import jax
import jax.numpy as jnp
from jax import lax
from jax.experimental import pallas as pl
from jax.experimental.pallas import tpu as pltpu

F32 = jnp.float32
BF16 = jnp.bfloat16
I32 = jnp.int32

HG_HEADS = 4
HG_DIM = 128
HG_W = HG_HEADS * HG_DIM
ATT_Q_HEADS = 8
ATT_KV_HEADS = 2
ATT_GROUP = ATT_Q_HEADS // ATT_KV_HEADS
ATT_DH = 64
ATT_Q_W = ATT_Q_HEADS * ATT_DH
ATT_KV_W = ATT_KV_HEADS * ATT_DH
ATT_BLOCK = 128
ROPE_THETA = 500000.0
ROPE_DIM = ATT_DH // 4
N_EXPERTS = 32
TOP_K = 4
SWIGLU_ALPHA = 1.702
SWIGLU_LIMIT = 7.0
EPS = 1e-6

V7X_LANES = 128
V7X_SUBLANES = 8
V7X_VMEM_LIMIT_BYTES = 56 * 1024 * 1024

TM_PROJ = 256
TM_MERGE = 512
HG_TILE = 512
HG_CHUNK = 128
HG_SUB = 8
MOE_BM = 512
TM_DISPATCH = 256
TM_COMBINE = 128

NEG_INF = float("-inf")


def _cparams(sem, vmem=V7X_VMEM_LIMIT_BYTES):
    return pltpu.CompilerParams(dimension_semantics=sem, vmem_limit_bytes=vmem)


def _sigmoid(x):
    return 1.0 / (1.0 + jnp.exp(-x))


def _dot(a, b):
    return jnp.dot(a, b, preferred_element_type=F32)


def _dot_nt(a, b):
    return lax.dot_general(a, b, (((1,), (1,)), ((), ())), preferred_element_type=F32)


def _dot_tn(a, b):
    return lax.dot_general(a, b, (((0,), (0,)), ((), ())), preferred_element_type=F32)


ROW_TILE = V7X_SUBLANES


def _store_row_tiles(ref, val):
    rows = val.shape[0]
    for g in range(ROW_TILE):
        ref[pl.ds(g, rows, stride=ROW_TILE), :] = val[:, g * V7X_LANES:(g + 1) * V7X_LANES]


def _load_row_tiles(ref, rows):
    return jnp.concatenate([ref[pl.ds(g, rows, stride=ROW_TILE), :] for g in range(ROW_TILE)], axis=1)


def _dot_exact01(m01, x):
    hi = x.astype(BF16)
    r1 = x - hi.astype(F32)
    mid = r1.astype(BF16)
    lo = (r1 - mid.astype(F32)).astype(BF16)
    return _dot(m01, hi) + _dot(m01, mid) + _dot(m01, lo)


def _ada_kernel(c_ref, w_ref, b_ref, o_ref):
    c = c_ref[...]
    cond = c * _sigmoid(c)
    o_ref[...] = _dot(cond.astype(BF16), w_ref[...].astype(BF16)) + b_ref[...]


def _ada(c, w_ada, b_ada):
    B, D = c.shape
    N = w_ada.shape[1]
    tn = D
    return pl.pallas_call(
        _ada_kernel,
        grid=(N // tn,),
        in_specs=[pl.BlockSpec((B, D), lambda j: (0, 0)),
                  pl.BlockSpec((D, tn), lambda j: (0, j)),
                  pl.BlockSpec((1, tn), lambda j: (0, j))],
        out_specs=pl.BlockSpec((B, tn), lambda j: (0, j)),
        out_shape=jax.ShapeDtypeStruct((B, N), F32),
        compiler_params=_cparams(("parallel",)),
        name="ada",
    )(c, w_ada, b_ada.reshape(1, N))


def _inproj_kernel(x_ref, mod_ref, g_ref, w_ref, hg_ref, at_ref, gt_ref, fmin_ref):
    x = x_ref[...]
    ms = jnp.mean(x * x, axis=-1, keepdims=True)
    y = x * lax.rsqrt(ms + EPS) * g_ref[...]
    h = (y * (1.0 + mod_ref[1:2, :]) + mod_ref[0:1, :]).astype(BF16)
    n_hg = hg_ref.shape[1]
    n_at = at_ref.shape[1]
    n_gt = gt_ref.shape[1]
    hg = _dot(h, w_ref[:, 0:n_hg])
    hg_ref[...] = hg
    f_pre = hg[:, HG_W:2 * HG_W]
    f_min = jnp.min(jnp.min(f_pre, axis=-1, keepdims=True), axis=0, keepdims=True)
    fmin_ref[...] = jnp.broadcast_to(f_min, fmin_ref.shape)
    at_ref[...] = _dot(h, w_ref[:, n_hg:n_hg + n_at])
    gt_ref[...] = _dot(h, w_ref[:, n_hg + n_at:n_hg + n_at + n_gt])


def _inproj(x2, mod3, gain, w_in_bf, S):
    T, D = x2.shape
    tm = TM_PROJ
    n_hg = 4 * HG_W
    n_at = ATT_Q_W + 2 * ATT_KV_W
    n_gt = 2 * D
    assert w_in_bf.shape[1] == n_hg + n_at + n_gt
    per_b = S // tm
    return pl.pallas_call(
        _inproj_kernel,
        grid=(T // tm,),
        in_specs=[pl.BlockSpec((tm, D), lambda i: (i, 0)),
                  pl.BlockSpec((None, 6, D), lambda i: (i // per_b, 0, 0)),
                  pl.BlockSpec((1, D), lambda i: (0, 0)),
                  pl.BlockSpec(w_in_bf.shape, lambda i: (0, 0))],
        out_specs=[pl.BlockSpec((tm, n_hg), lambda i: (i, 0)),
                   pl.BlockSpec((tm, n_at), lambda i: (i, 0)),
                   pl.BlockSpec((tm, n_gt), lambda i: (i, 0)),
                   pl.BlockSpec((V7X_SUBLANES, V7X_LANES), lambda i: (i, 0))],
        out_shape=[jax.ShapeDtypeStruct((T, n_hg), F32),
                   jax.ShapeDtypeStruct((T, n_at), F32),
                   jax.ShapeDtypeStruct((T, n_gt), F32),
                   jax.ShapeDtypeStruct((T // tm * V7X_SUBLANES, V7X_LANES), F32)],
        compiler_params=_cparams(("parallel",)),
        name="inproj",
    )(x2, mod3, gain.reshape(1, D), w_in_bf)


def _hgrn_constants():
    C = HG_CHUNK
    tri = (np.arange(C)[None, :] <= np.arange(C)[:, None]).astype(np.float32)
    halves = []
    lh = C // 2
    while lh >= HG_SUB:
        halves.append(lh)
        lh //= 2
    code = np.zeros((C, C), np.int32)
    ii, jj = np.meshgrid(np.arange(C), np.arange(C), indexing="ij")
    for li, lh in enumerate(halves):
        sel = ((ii // (2 * lh)) == (jj // (2 * lh))) & ((ii % (2 * lh)) >= lh) & ((jj % (2 * lh)) < lh)
        code[sel] = li + 1
    return tri, code, halves


HG_MILD_FMIN = -7.0


def _hgrn_kernel(mild_ref, in_ref, lbl_ref, gain_ref, tri_ref, code_ref, wsum_ref, o_ref, st_ref, *, halves):
    C = HG_CHUNK
    W = HG_W
    n_chunks = in_ref.shape[1] // C
    step_is_mild = mild_ref[pl.program_id(0) * pl.num_programs(1) + pl.program_id(1)] == 1

    @pl.when(pl.program_id(1) == 0)
    def _():
        st_ref[...] = jnp.zeros_like(st_ref)

    l0 = lbl_ref[0:1, :]
    l1 = lbl_ref[1:2, :]
    mx = jnp.maximum(l0, l1)
    e0 = jnp.exp(l0 - mx)
    lb = e0 / (e0 + jnp.exp(l1 - mx))
    gain = gain_ref[...]
    tri = tri_ref[...].astype(BF16)
    code = code_ref[...]
    wsum = wsum_ref[...]
    row = lax.broadcasted_iota(I32, (C, C), 0)
    col = lax.broadcasted_iota(I32, (C, C), 1)
    row_in_sub = lax.broadcasted_iota(I32, (C, HG_DIM), 0) % HG_SUB
    same_sub = (row // HG_SUB) == (col // HG_SUB)
    sub_causal = same_sub & (col <= row)

    def group_row(a, group, r):
        a3 = a.reshape(C // group, group, a.shape[1])
        return jnp.broadcast_to(a3[:, r:r + 1, :], a3.shape).reshape(a.shape)

    def bcast_sub(a, j):
        return group_row(a, HG_SUB, j)

    def chunk(mild, c, carry):
        r0 = pl.multiple_of(c * C, C)
        rows = pl.ds(r0, C)
        for bb in range(in_ref.shape[0]):
            qp = in_ref[bb, rows, 0:W]
            fp = in_ref[bb, rows, W:2 * W]
            vv = in_ref[bb, rows, 2 * W:3 * W]
            gp = in_ref[bb, rows, 3 * W:4 * W]
            forget = lb + (1.0 - lb) * _sigmoid(fp)
            q = qp * _sigmoid(qp)
            k = 1.0 - forget
            lf = jnp.log(forget)
            b = _dot_exact01(tri, lf)
            e_b = jnp.exp(b)
            e_st = jnp.exp(group_row(b, C, C - 1) - b)
            e_lv = [jnp.exp(-jnp.abs(b - group_row(b, 2 * lh, lh - 1))) for lh in halves]
            if mild:
                x_sub = group_row(b, HG_SUB, HG_SUB - 1) - b
                e_subk = jnp.exp(x_sub)
                e_subq = jnp.exp(-x_sub)
            outs = []
            for h in range(HG_HEADS):
                ls = slice(h * HG_DIM, (h + 1) * HG_DIM)
                qh, kh, vh = q[:, ls], k[:, ls], vv[:, ls]
                bh = b[:, ls]
                st = st_ref[bb, h]
                o = _dot_nt((qh * e_b[:, ls]).astype(BF16), st.astype(BF16))
                if mild:
                    s = _dot_nt((qh * e_subq[:, ls]).astype(BF16), (kh * e_subk[:, ls]).astype(BF16))
                    scores = jnp.where(sub_causal, s, 0.0)
                else:
                    ps = []
                    for j in range(HG_SUB):
                        d = jnp.where(row_in_sub >= j, bh - bcast_sub(bh, j), NEG_INF)
                        ps.append((qh * bcast_sub(kh, j) * jnp.exp(d)).astype(BF16))
                    scores = jnp.where(same_sub, _dot(jnp.concatenate(ps, axis=1), wsum), 0.0)
                for li in range(len(halves)):
                    e = e_lv[li][:, ls]
                    s = _dot_nt((qh * e).astype(BF16), (kh * e).astype(BF16))
                    scores = jnp.where(code == li + 1, s, scores)
                o = o + _dot(scores.astype(BF16), vh.astype(BF16))
                kst = (kh * e_st[:, ls]).astype(BF16)
                st_ref[bb, h] = st * e_b[C - 1:C, ls] + _dot_tn(vh.astype(BF16), kst)
                ms = jnp.mean(o * o, axis=-1, keepdims=True)
                on = o * lax.rsqrt(ms + EPS) * gain
                gh = gp[:, ls]
                outs.append(on * (gh * _sigmoid(gh)))
            o_ref[bb, rows, :] = jnp.concatenate(outs, axis=-1).astype(o_ref.dtype)
        return carry

    @pl.when(step_is_mild)
    def _():
        lax.fori_loop(0, n_chunks, functools.partial(chunk, True), 0)

    @pl.when(jnp.logical_not(step_is_mild))
    def _():
        lax.fori_loop(0, n_chunks, functools.partial(chunk, False), 0)


HG_SEQS = 4


def _hgrn(hg_in, f_min, lb_logits, gain, B, S):
    T = hg_in.shape[0]
    lt = HG_TILE
    tri, code, halves = _hgrn_constants()
    nseq = HG_SEQS if B % HG_SEQS == 0 else 1
    step_min = jnp.min(f_min.reshape(B // nseq, nseq, S // lt, lt // TM_PROJ), axis=(1, 3))
    mild = (step_min >= HG_MILD_FMIN).astype(I32).reshape(-1)
    wsum = (np.arange(HG_SUB * HG_DIM)[:, None] // HG_DIM == np.arange(HG_CHUNK)[None, :] % HG_SUB).astype(np.float32)
    const = lambda b, s, m: (0, 0)
    grid_spec = pltpu.PrefetchScalarGridSpec(
        num_scalar_prefetch=1,
        grid=(B // nseq, S // lt),
        in_specs=[pl.BlockSpec((nseq, lt, 4 * HG_W), lambda b, s, m: (b, s, 0)),
                  pl.BlockSpec(lb_logits.shape, const),
                  pl.BlockSpec((1, HG_DIM), const),
                  pl.BlockSpec(tri.shape, const),
                  pl.BlockSpec(code.shape, const),
                  pl.BlockSpec(wsum.shape, const)],
        out_specs=pl.BlockSpec((nseq, lt, HG_W), lambda b, s, m: (b, s, 0)),
        scratch_shapes=[pltpu.VMEM((nseq, HG_HEADS, HG_DIM, HG_DIM), F32)],
    )
    out = pl.pallas_call(
        functools.partial(_hgrn_kernel, halves=tuple(halves)),
        grid_spec=grid_spec,
        out_shape=jax.ShapeDtypeStruct((B, S, HG_W), BF16),
        compiler_params=_cparams(("parallel", "arbitrary")),
        name="hgrn2",
    )(mild, hg_in.reshape(B, S, 4 * HG_W), lb_logits, gain.reshape(1, HG_DIM), jnp.asarray(tri),
      jnp.asarray(code), jnp.asarray(wsum, dtype=BF16))
    return out.reshape(T, HG_W)


SWA_QBLOCKS = 4
ROPE_ROWS = 16


def _swa_constants():
    lane = np.arange(V7X_LANES)
    c = lane % ATT_DH
    half = ROPE_DIM // 2
    inv_freq = ROPE_THETA ** (-jnp.arange(half, dtype=F32) / half)
    freq_rows = jnp.broadcast_to(
        jnp.concatenate([inv_freq, jnp.zeros((ROPE_ROWS - half,), F32)])[:, None], (ROPE_ROWS, V7X_LANES))
    sel = ((np.arange(ROPE_ROWS)[:, None] == (c % half)[None, :]) & (c < ROPE_DIM)[None, :]).astype(np.float32)
    sign = np.where(c < half, -1.0, np.where(c < ROPE_DIM, 1.0, 0.0)).astype(np.float32)
    first = (c < half).astype(np.float32)
    tab = np.stack([sign, first], axis=0)
    g = (lane[:, None] // ATT_DH == lane[None, :] // ATT_DH).astype(np.float32) / ATT_DH
    return freq_rows, sel, tab, g


def _swa_kernel(sink_ref, cur_ref, prev_ref, pcur_ref, pprev_ref, qg_ref, kg_ref, freq_ref, sel_ref, tab_ref, g_ref,
                o_ref):
    Bq = ATT_BLOCK
    n = pl.program_id(1)
    tab = tab_ref[...]
    sign, first = tab[0:1, :], tab[1:2, :]
    gmat = g_ref[...].astype(BF16)
    sel = sel_ref[...].astype(BF16)
    freq_rows = freq_ref[...]
    half = ROPE_DIM // 2
    scale = ATT_DH ** -0.5

    def group_ms(x):
        sq = x * x
        hi = sq.astype(BF16)
        lo = (sq - hi.astype(F32)).astype(BF16)
        return _dot(hi, gmat) + _dot(lo, gmat)

    def spread(a):
        hi = a.astype(BF16)
        r1 = a - hi.astype(F32)
        mid = r1.astype(BF16)
        lo = (r1 - mid.astype(F32)).astype(BF16)
        return _dot_tn(hi, sel) + _dot_tn(mid, sel) + _dot_tn(lo, sel)

    def rope_table(pos_row):
        ang = freq_rows * pos_row
        return 1.0 + spread(jnp.cos(ang) - 1.0), spread(jnp.sin(ang)) * sign

    def norm_rope(x, gain, cs_sn):
        y = x * lax.rsqrt(group_ms(x) + EPS) * gain
        partner = jnp.where(first > 0.5, pltpu.roll(y, V7X_LANES - half, 1), pltpu.roll(y, half, 1))
        return y * cs_sn[0] + partner * cs_sn[1]

    kq = ATT_Q_W
    lane = lax.broadcasted_iota(I32, (Bq, V7X_LANES), 1)
    lo_half = lane < ATT_DH

    def pad_variants(a):
        r = pltpu.roll(a, ATT_DH, 1)
        z = jnp.zeros_like(a)
        return [[jnp.where(lo_half, a, z).astype(BF16), jnp.where(lo_half, z, r).astype(BF16)],
                [jnp.where(lo_half, r, z).astype(BF16), jnp.where(lo_half, z, a).astype(BF16)]]

    tables = [rope_table(pprev_ref[0].astype(F32))]
    kblocks = [pad_variants(norm_rope(prev_ref[:, 0:ATT_KV_W], kg_ref[...], tables[0]))]
    vblocks = [pad_variants(prev_ref[:, ATT_KV_W:2 * ATT_KV_W])]
    for j in range(SWA_QBLOCKS):
        rows = slice(j * Bq, (j + 1) * Bq)
        tables.append(rope_table(pcur_ref[j].astype(F32)))
        kblocks.append(pad_variants(norm_rope(cur_ref[rows, kq:kq + ATT_KV_W], kg_ref[...], tables[j + 1])))
        vblocks.append(pad_variants(cur_ref[rows, kq + ATT_KV_W:kq + 2 * ATT_KV_W]))

    qi = lax.broadcasted_iota(I32, (Bq, 2 * Bq), 0)
    kj = lax.broadcasted_iota(I32, (Bq, 2 * Bq), 1)
    in_band = ((kj < Bq) & (kj > qi)) | ((kj >= Bq) & ((kj - Bq) <= qi))
    first_of_seq = (jnp.zeros((Bq, 2 * Bq), I32) + n) == 0
    for j in range(SWA_QBLOCKS):
        rows = slice(j * Bq, (j + 1) * Bq)
        mask = (in_band & jnp.logical_not(first_of_seq & (kj < Bq))) if j == 0 else in_band
        for t in range(ATT_Q_W // V7X_LANES):
            ls = slice(t * V7X_LANES, (t + 1) * V7X_LANES)
            qt = (norm_rope(cur_ref[rows, ls], qg_ref[...], tables[j + 1]) * scale).astype(BF16)
            acc = jnp.zeros((Bq, V7X_LANES), F32)
            for u in range(2):
                head = 2 * t + u
                kvh = head // ATT_GROUP
                kcat = jnp.concatenate([kblocks[j][kvh][u], kblocks[j + 1][kvh][u]], axis=0)
                vcat = jnp.concatenate([vblocks[j][kvh][u], vblocks[j + 1][kvh][u]], axis=0)
                s = jnp.where(mask, _dot_nt(qt, kcat), NEG_INF)
                sink = sink_ref[head]
                m = jnp.maximum(jnp.max(s, axis=-1, keepdims=True), sink)
                p = jnp.exp(s - m)
                denom = jnp.sum(p, axis=-1, keepdims=True) + jnp.exp(sink - m)
                acc = acc + _dot(p.astype(BF16), vcat) * (1.0 / denom)
            o_ref[rows, ls] = acc.astype(o_ref.dtype)


def _swa(at_in, positions, q_gain, k_gain, sinks, B, S):
    T = at_in.shape[0]
    nb = S // ATT_BLOCK
    qb = SWA_QBLOCKS
    assert nb % qb == 0
    steps = nb // qb
    freq_rows, sel, tab, g = _swa_constants()
    qg = jnp.tile(q_gain.reshape(1, ATT_DH), (1, V7X_LANES // ATT_DH))
    kg = jnp.tile(k_gain.reshape(1, ATT_DH), (1, V7X_LANES // ATT_DH))
    pos3 = positions.reshape(B * nb, 1, ATT_BLOCK)
    n_at = at_in.shape[1]
    kv_blk = 2 * ATT_KV_W
    assert ATT_Q_W % kv_blk == 0
    prev_blk = lambda b, n: b * nb + jnp.maximum(qb * n - 1, 0)
    const = lambda b, n: (0, 0)
    return pl.pallas_call(
        _swa_kernel,
        grid=(B, steps),
        in_specs=[pl.BlockSpec(memory_space=pltpu.SMEM),
                  pl.BlockSpec((qb * ATT_BLOCK, n_at), lambda b, n: (b * steps + n, 0)),
                  pl.BlockSpec((ATT_BLOCK, kv_blk), lambda b, n: (prev_blk(b, n), ATT_Q_W // kv_blk)),
                  pl.BlockSpec((qb, 1, ATT_BLOCK), lambda b, n: (b * steps + n, 0, 0)),
                  pl.BlockSpec((1, 1, ATT_BLOCK), lambda b, n: (prev_blk(b, n), 0, 0)),
                  pl.BlockSpec((1, V7X_LANES), const),
                  pl.BlockSpec((1, V7X_LANES), const),
                  pl.BlockSpec(freq_rows.shape, const),
                  pl.BlockSpec(sel.shape, const),
                  pl.BlockSpec(tab.shape, const),
                  pl.BlockSpec(g.shape, const)],
        out_specs=pl.BlockSpec((qb * ATT_BLOCK, ATT_Q_W), lambda b, n: (b * steps + n, 0)),
        out_shape=jax.ShapeDtypeStruct((T, ATT_Q_W), BF16),
        compiler_params=_cparams(("parallel", "parallel")),
        name="swa",
    )(sinks, at_in, at_in, pos3, pos3, qg, kg, freq_rows, jnp.asarray(sel), jnp.asarray(tab), jnp.asarray(g))


def _merge_router_kernel(x_ref, hg_ref, at_ref, gt_ref, mod_ref, whg_ref, wat_ref, wout_ref, g2_ref,
                         wr_ref, br_ref, tri_ref,
                         x1_ref, h2_ref, idx_ref, gate_ref, rank_ref, cnt_ref, run_ref):
    i = pl.program_id(0)
    D = x_ref.shape[1]
    tm = x_ref.shape[0]

    @pl.when(i == 0)
    def _():
        run_ref[...] = jnp.zeros_like(run_ref)

    y_h = _dot(hg_ref[...], whg_ref[...])
    y_a = _dot(at_ref[...], wat_ref[...])
    merged = _sigmoid(gt_ref[:, 0:D]) * y_h + _sigmoid(gt_ref[:, D:2 * D]) * y_a
    x1 = x_ref[...] + mod_ref[2:3, :] * _dot(merged.astype(BF16), wout_ref[...])
    x1_ref[...] = x1
    ms = jnp.mean(x1 * x1, axis=-1, keepdims=True)
    h2 = x1 * lax.rsqrt(ms + EPS) * g2_ref[...] * (1.0 + mod_ref[4:5, :]) + mod_ref[3:4, :]
    _store_row_tiles(h2_ref, h2)
    logits = _dot(h2.astype(BF16), wr_ref[...]) + br_ref[...]
    E = logits.shape[1]
    lane = lax.broadcasted_iota(I32, (tm, E), 1).astype(F32)
    vals, idxs = [], []
    l = logits
    for _ in range(TOP_K):
        m = jnp.max(l, axis=-1, keepdims=True)
        ik = jnp.min(jnp.where(l == m, lane, float(E)), axis=-1, keepdims=True)
        vals.append(m)
        idxs.append(ik)
        l = jnp.where(lane == ik, NEG_INF, l)
    ex = [jnp.exp(v - vals[0]) for v in vals]
    den = ex[0]
    for e in ex[1:]:
        den = den + e
    onehot = jnp.zeros((tm, E), F32)
    for ik in idxs:
        onehot = onehot + (lane == ik).astype(F32)
    cum = _dot(tri_ref[...], onehot.astype(BF16))
    run = run_ref[0:1, 0:E]
    excl = cum - onehot + run
    lane_k = lax.broadcasted_iota(I32, (tm, TOP_K), 1)
    idx_o = jnp.zeros((tm, TOP_K), I32)
    gate_o = jnp.zeros((tm, TOP_K), F32)
    rank_o = jnp.zeros((tm, TOP_K), I32)
    for kk in range(TOP_K):
        rk = jnp.sum(jnp.where(lane == idxs[kk], excl, 0.0), axis=-1, keepdims=True)
        idx_o = jnp.where(lane_k == kk, idxs[kk].astype(I32), idx_o)
        gate_o = jnp.where(lane_k == kk, ex[kk] / den, gate_o)
        rank_o = jnp.where(lane_k == kk, rk.astype(I32), rank_o)
    idx_ref[...] = idx_o
    gate_ref[...] = gate_o
    rank_ref[...] = rank_o
    new_run = run + cum[tm - 1:tm, :]
    run_ref[0:1, 0:E] = new_run
    cnt_ref[...] = jnp.broadcast_to(new_run, cnt_ref.shape)


def _merge_router(x2, hg_o, at_o, gates, mod3, whg, wat, wout, g2, wr, br, S):
    T, D = x2.shape
    tm = TM_MERGE
    per_b = S // tm
    E = wr.shape[1]
    tri = jnp.asarray(np.tril(np.ones((tm, tm), np.float32)), dtype=BF16)
    row = lambda i: (i, 0)
    const = lambda i: (0, 0)
    return pl.pallas_call(
        _merge_router_kernel,
        grid=(T // tm,),
        in_specs=[pl.BlockSpec((tm, D), row),
                  pl.BlockSpec((tm, HG_W), row),
                  pl.BlockSpec((tm, ATT_Q_W), row),
                  pl.BlockSpec((tm, 2 * D), row),
                  pl.BlockSpec((None, 6, D), lambda i: (i // per_b, 0, 0)),
                  pl.BlockSpec(whg.shape, const),
                  pl.BlockSpec(wat.shape, const),
                  pl.BlockSpec(wout.shape, const),
                  pl.BlockSpec((1, D), const),
                  pl.BlockSpec(wr.shape, const),
                  pl.BlockSpec((1, E), const),
                  pl.BlockSpec((tm, tm), const)],
        out_specs=[pl.BlockSpec((tm, D), row),
                   pl.BlockSpec((tm * ROW_TILE, V7X_LANES), row),
                   pl.BlockSpec((tm, TOP_K), row),
                   pl.BlockSpec((tm, TOP_K), row),
                   pl.BlockSpec((tm, TOP_K), row),
                   pl.BlockSpec((V7X_SUBLANES, E), const)],
        out_shape=[jax.ShapeDtypeStruct((T, D), F32),
                   jax.ShapeDtypeStruct((T * ROW_TILE, V7X_LANES), F32),
                   jax.ShapeDtypeStruct((T, TOP_K), I32),
                   jax.ShapeDtypeStruct((T, TOP_K), F32),
                   jax.ShapeDtypeStruct((T, TOP_K), I32),
                   jax.ShapeDtypeStruct((V7X_SUBLANES, E), F32)],
        scratch_shapes=[pltpu.VMEM((V7X_SUBLANES, V7X_LANES), F32)],
        compiler_params=_cparams(("arbitrary",)),
        name="merge_router",
    )(x2, hg_o, at_o, gates, mod3, whg, wat, wout, g2.reshape(1, D), wr, br.reshape(1, E), tri)


def _dest_kernel(idx_ref, rank_ref, ps_ref, o_ref):
    idx = idx_ref[...]
    tm = idx.shape[0]
    E = ps_ref.shape[1]
    lane = lax.broadcasted_iota(I32, (tm, E), 1)
    lane_k = lax.broadcasted_iota(I32, (tm, TOP_K), 1)
    ps = ps_ref[...].astype(F32)
    out = rank_ref[...]
    for kk in range(TOP_K):
        start = jnp.sum(jnp.where(lane == idx[:, kk:kk + 1], ps, 0.0), axis=-1, keepdims=True)
        out = out + jnp.where(lane_k == kk, start.astype(I32), 0)
    o_ref[...] = out


def _dest(idx, rank, pad_start):
    T = idx.shape[0]
    tm = 1024
    E = pad_start.shape[0]
    row = lambda i: (i, 0)
    return pl.pallas_call(
        _dest_kernel,
        grid=(T // tm,),
        in_specs=[pl.BlockSpec((tm, TOP_K), row), pl.BlockSpec((tm, TOP_K), row),
                  pl.BlockSpec((1, E), lambda i: (0, 0))],
        out_specs=pl.BlockSpec((tm, TOP_K), row),
        out_shape=jax.ShapeDtypeStruct((T, TOP_K), I32),
        compiler_params=_cparams(("parallel",)),
        name="dest",
    )(idx, rank, pad_start.reshape(1, E))


DISPATCH_UNROLL = 4


def _row(ref, r):
    return ref.at[pl.ds(pl.multiple_of(r * ROW_TILE, ROW_TILE), ROW_TILE)]


def _dispatch_kernel(fill_start_ref, fill_n_ref, tail_ref, dest_ref, h_ref, xbuf_ref, zero_ref, sem):
    tm = h_ref.shape[0] // ROW_TILE
    zrows = zero_ref.shape[0]

    def zero_row_copy(r):
        return pltpu.make_async_copy(_row(zero_ref, 0), _row(xbuf_ref, r), sem.at[1])

    def zero_block_copy(b):
        dst = xbuf_ref.at[pl.ds(pl.multiple_of(b * zrows, zrows), zrows)]
        return pltpu.make_async_copy(zero_ref, dst, sem.at[2])

    @pl.when(pl.program_id(0) == 0)
    def _():
        zero_ref[...] = jnp.zeros_like(zero_ref)

        def per_expert(start):
            def body(e, c):
                def rows(r, c2):
                    cp = zero_row_copy(fill_start_ref[e] + r)
                    cp.start() if start else cp.wait()
                    return c2
                return lax.fori_loop(0, fill_n_ref[e], rows, c)
            return body

        def tail(start):
            def body(b, c):
                cp = zero_block_copy(tail_ref[0] + b)
                cp.start() if start else cp.wait()
                return c
            return body

        lax.fori_loop(0, N_EXPERTS, per_expert(True), 0)
        lax.fori_loop(0, tail_ref[1], tail(True), 0)
        lax.fori_loop(0, N_EXPERTS, per_expert(False), 0)
        lax.fori_loop(0, tail_ref[1], tail(False), 0)

    def issue(tb, c):
        for u in range(DISPATCH_UNROLL):
            t = tb * DISPATCH_UNROLL + u
            src = _row(h_ref, t)
            for kk in range(TOP_K):
                d = dest_ref[t * TOP_K + kk]
                pltpu.make_async_copy(src, _row(xbuf_ref, d), sem.at[0]).start(priority=kk % 2)
        return c

    lax.fori_loop(0, tm // DISPATCH_UNROLL, issue, 0)

    def drain(tb, c):
        for _ in range(DISPATCH_UNROLL * TOP_K):
            pltpu.make_async_copy(_row(h_ref, 0), _row(xbuf_ref, 0), sem.at[0]).wait()
        return c

    lax.fori_loop(0, tm // DISPATCH_UNROLL, drain, 0)


def _dispatch(fill_start, fill_n, tail, dest_flat, h2t, P):
    T = h2t.shape[0] // ROW_TILE
    tm = TM_DISPATCH
    grid_spec = pltpu.PrefetchScalarGridSpec(
        num_scalar_prefetch=3,
        grid=(T // tm,),
        in_specs=[pl.BlockSpec((tm * TOP_K,), lambda i, *_: (i,), memory_space=pltpu.SMEM),
                  pl.BlockSpec((tm * ROW_TILE, V7X_LANES), lambda i, *_: (i, 0))],
        out_specs=pl.BlockSpec(memory_space=pl.ANY),
        scratch_shapes=[pltpu.VMEM((MOE_BM * ROW_TILE, V7X_LANES), F32), pltpu.SemaphoreType.DMA((3,))],
    )
    return pl.pallas_call(
        _dispatch_kernel,
        grid_spec=grid_spec,
        out_shape=jax.ShapeDtypeStruct((P * ROW_TILE, V7X_LANES), F32),
        compiler_params=_cparams(("arbitrary",)),
        name="dispatch",
    )(fill_start, fill_n, tail, dest_flat, h2t)


FFN_PREP_COLS = 256


FFN_UNITS_PER_STEP = 2


def _ffn_kernel(cur_ref, src_ref, slot_ref, pos_ref, last_ref, used_ref,
                x_ref, wu_ref, wd_ref, bg_ref, bl_ref, bd_ref, y_ref,
                t_ref, wg0, wl0, wd0, wg1, wl1, wd1):
    del cur_ref, src_ref
    s = pl.program_id(0)
    bm = x_ref.shape[0] // ROW_TILE
    D, De2 = wu_ref.shape
    De = De2 // 2
    n_slab = D // V7X_LANES
    half = FFN_PREP_COLS // 2
    n_units = De2 // FFN_PREP_COLS
    drows = De // n_units
    slot = slot_ref[s]
    used = used_ref[s] == 1
    stages = ((wg0, wl0, wd0), (wg1, wl1, wd1))

    def stage_unit(u, dst, tbuf):
        wg, wl, wdb = dst
        c0 = pl.multiple_of(u * FFN_PREP_COLS, FFN_PREP_COLS)
        tt = wu_ref[:, pl.ds(c0, FFN_PREP_COLS)].T
        for sl in range(n_slab):
            t_ref[tbuf, sl] = tt[:, sl * V7X_LANES:(sl + 1) * V7X_LANES]
        r0 = pl.multiple_of(u * half, half)
        for sl in range(n_slab):
            ls = slice(sl * V7X_LANES, (sl + 1) * V7X_LANES)
            wg[pl.ds(r0, half), ls] = t_ref[tbuf, sl, pl.ds(0, half, stride=2), :].astype(BF16)
            wl[pl.ds(r0, half), ls] = t_ref[tbuf, sl, pl.ds(1, half, stride=2), :].astype(BF16)
        d0 = pl.multiple_of(u * drows, drows)
        wdb[pl.ds(d0, drows), :] = wd_ref[pl.ds(d0, drows), :].astype(BF16)

    @pl.when(s == 0)
    def _():
        for u in range(n_units):
            stage_unit(u, stages[0], u % 2)

    done = (pos_ref[s] + 1) * FFN_UNITS_PER_STEP

    def block(cur, nxt):
        @pl.when(used)
        def _():
            u0 = jnp.minimum(pos_ref[s] * FFN_UNITS_PER_STEP, n_units - FFN_UNITS_PER_STEP)
            for j in range(FFN_UNITS_PER_STEP):
                stage_unit(u0 + j, nxt, j % 2)
            wg, wl, wdb = cur
            x = _load_row_tiles(x_ref, bm).astype(BF16)
            glu = _dot_nt(x, wg[...]) + bg_ref[...]
            lin = _dot_nt(x, wl[...]) + bl_ref[...]
            glu = jnp.minimum(glu, SWIGLU_LIMIT)
            lin = jnp.clip(lin, -SWIGLU_LIMIT, SWIGLU_LIMIT)
            act = glu * _sigmoid(SWIGLU_ALPHA * glu) * (lin + 1.0)
            _store_row_tiles(y_ref, _dot(act.astype(BF16), wdb[...]) + bd_ref[...])

        @pl.when(used & (last_ref[s] == 1) & (done < n_units))
        def _():
            def body(u, c):
                stage_unit(u, nxt, 0)
                return c
            lax.fori_loop(done, n_units, body, 0)

    @pl.when(slot == 0)
    def _():
        block(stages[0], stages[1])

    @pl.when(slot == 1)
    def _():
        block(stages[1], stages[0])

    @pl.when(jnp.logical_not(used) & (s > 0))
    def _():
        y_ref[...] = jnp.zeros_like(y_ref)


def _ffn_schedule(block_expert, n_used):
    n = block_expert.shape[0]
    idx = jnp.arange(n, dtype=I32)
    be = block_expert
    first = jnp.concatenate([jnp.ones((1,), bool), be[1:] != be[:-1]])
    run_start = lax.cummax(jnp.where(first, idx, 0))
    ordinal = jnp.cumsum(first.astype(I32)) - 1
    is_last = jnp.concatenate([first[1:], jnp.ones((1,), bool)])
    next_first = lax.cummin(jnp.where(first, idx, n), reverse=True)
    next_start = jnp.concatenate([next_first[1:], jnp.full((1,), n, I32)])
    next_e = be[jnp.minimum(next_start, n - 1)]
    blk = jnp.maximum(jnp.arange(n + 1, dtype=I32) - 1, 0)
    step = jnp.arange(n + 1, dtype=I32)
    cur = be[blk]
    src = jnp.where(step == 0, be[0], next_e[blk])
    slot = ordinal[blk] % 2
    pos = blk - run_start[blk]
    last = is_last[blk].astype(I32)
    used = ((step >= 1) & (blk < n_used[0])).astype(I32)
    return [a.astype(I32) for a in (cur, src, slot, pos, last, used)]


def _ffn(block_expert, n_used, xbuf, w_up, bg, bl, w_down, bd):
    P = xbuf.shape[0] // ROW_TILE
    bm = MOE_BM
    _, D, De2 = w_up.shape
    De = De2 // 2
    assert (De2 // FFN_PREP_COLS) % FFN_UNITS_PER_STEP == 0
    sched = _ffn_schedule(block_expert, n_used)
    rows = lambda s, *_: (jnp.maximum(s - 1, 0), 0)
    wmap = lambda s, cur, src, *_: (src[s], 0, 0)
    bmap = lambda s, cur, *_: (cur[s], 0, 0)
    grid_spec = pltpu.PrefetchScalarGridSpec(
        num_scalar_prefetch=len(sched),
        grid=(P // bm + 1,),
        in_specs=[pl.BlockSpec((bm * ROW_TILE, V7X_LANES), rows),
                  pl.BlockSpec((None, D, De2), wmap),
                  pl.BlockSpec((None, De, D), wmap),
                  pl.BlockSpec((None, 1, De), bmap),
                  pl.BlockSpec((None, 1, De), bmap),
                  pl.BlockSpec((None, 1, D), bmap)],
        out_specs=pl.BlockSpec((bm * ROW_TILE, V7X_LANES), rows),
        scratch_shapes=[pltpu.VMEM((2, D // V7X_LANES, FFN_PREP_COLS, V7X_LANES), F32)]
        + [pltpu.VMEM((De, D), BF16)] * 6,
    )
    return pl.pallas_call(
        _ffn_kernel,
        grid_spec=grid_spec,
        out_shape=jax.ShapeDtypeStruct((P * ROW_TILE, V7X_LANES), F32),
        compiler_params=_cparams(("arbitrary",)),
        name="expert_ffn",
    )(*sched, xbuf, w_up, w_down, bg, bl, bd)


def _combine_kernel(dcur_ref, dnext_ref, gate_ref, x1_ref, mod_ref, y_hbm, o_ref, buf, sem):
    i = pl.program_id(0)
    n = pl.num_programs(0)
    tm = x1_ref.shape[0]
    slot = i % 2

    def issue(dref, s):
        def body(tb, c):
            for u in range(DISPATCH_UNROLL):
                t = tb * DISPATCH_UNROLL + u
                for kk in range(TOP_K):
                    d = dref[t * TOP_K + kk]
                    pltpu.make_async_copy(_row(y_hbm, d), _row(buf.at[s, kk], t), sem.at[s]).start(priority=kk % 2)
            return c
        lax.fori_loop(0, tm // DISPATCH_UNROLL, body, 0)

    @pl.when(i == 0)
    def _():
        issue(dcur_ref, 0)

    @pl.when(i + 1 < n)
    def _():
        issue(dnext_ref, 1 - slot)

    def drain(tb, c):
        for _ in range(DISPATCH_UNROLL * TOP_K):
            pltpu.make_async_copy(_row(y_hbm, 0), _row(buf.at[slot, 0], 0), sem.at[slot]).wait()
        return c

    lax.fori_loop(0, tm // DISPATCH_UNROLL, drain, 0)
    gate = gate_ref[...]
    acc = gate[:, 0:1] * _load_row_tiles(buf.at[slot, 0], tm)
    for kk in range(1, TOP_K):
        acc = acc + gate[:, kk:kk + 1] * _load_row_tiles(buf.at[slot, kk], tm)
    o_ref[...] = x1_ref[...] + mod_ref[5:6, :] * acc


def _combine(dest_flat, gate, x1, mod3, ybuf, S):
    T, D = x1.shape
    tm = TM_COMBINE
    per_b = S // tm
    nt = T // tm
    return pl.pallas_call(
        _combine_kernel,
        grid=(nt,),
        in_specs=[pl.BlockSpec((tm * TOP_K,), lambda i: (i,), memory_space=pltpu.SMEM),
                  pl.BlockSpec((tm * TOP_K,), lambda i: (jnp.minimum(i + 1, nt - 1),), memory_space=pltpu.SMEM),
                  pl.BlockSpec((tm, TOP_K), lambda i: (i, 0)),
                  pl.BlockSpec((tm, D), lambda i: (i, 0)),
                  pl.BlockSpec((None, 6, D), lambda i: (i // per_b, 0, 0)),
                  pl.BlockSpec(memory_space=pl.ANY)],
        out_specs=pl.BlockSpec((tm, D), lambda i: (i, 0)),
        out_shape=jax.ShapeDtypeStruct((T, D), F32),
        scratch_shapes=[pltpu.VMEM((2, TOP_K, tm * ROW_TILE, V7X_LANES), F32), pltpu.SemaphoreType.DMA((2,))],
        compiler_params=_cparams(("arbitrary",)),
        name="combine",
    )(dest_flat, dest_flat, gate, x1, mod3, ybuf)


def kernel(x, c, positions, w_ada, b_ada, norm1_gain, w_in, lower_bound_logits, hg_norm_gain, w_hg_branch,
           q_norm_gain, k_norm_gain, attn_sinks, w_attn_branch, w_out, norm2_gain, w_router, b_router,
           w_up, b_up, w_down, b_down):
    B, S, D = x.shape
    T = B * S
    assert w_ada.shape[0] == 1, "one layer"
    x2 = x.reshape(T, D)

    mod = _ada(c, w_ada[0], b_ada[0])
    mod3 = mod.reshape(B, 6, D)

    hg_in, at_in, gates, f_min = _inproj(x2, mod3, norm1_gain[0], w_in[0].astype(BF16), S)
    f_min = f_min[::V7X_SUBLANES, 0].reshape(B, S // TM_PROJ)
    hg_o = _hgrn(hg_in, f_min, lower_bound_logits, hg_norm_gain[0], B, S)
    at_o = _swa(at_in, positions, q_norm_gain[0], k_norm_gain[0], attn_sinks[0], B, S)

    x1, h2, idx, gate, rank, cnt = _merge_router(
        x2, hg_o, at_o, gates, mod3, w_hg_branch[0].astype(BF16), w_attn_branch[0].astype(BF16),
        w_out[0].astype(BF16), norm2_gain[0], w_router[0].astype(BF16), b_router[0], S)

    bm = MOE_BM
    counts = cnt[0].astype(I32)
    padded = (counts + bm - 1) // bm * bm
    pad_end = jnp.cumsum(padded)
    pad_start = pad_end - padded
    P = T * TOP_K + N_EXPERTS * bm
    n_blocks = P // bm
    block_start = jnp.arange(n_blocks, dtype=I32) * bm
    block_expert = jnp.minimum(
        jnp.sum((pad_end[None, :] <= block_start[:, None]).astype(I32), axis=1), N_EXPERTS - 1).astype(I32)
    n_used = (pad_end[-1:] // bm).astype(I32)
    tail = jnp.concatenate([n_used, n_blocks - n_used]).astype(I32)

    dest = _dest(idx, rank, pad_start)
    dest_flat = dest.reshape(T * TOP_K)
    xbuf = _dispatch((pad_start + counts).astype(I32), (padded - counts).astype(I32), tail, dest_flat, h2, P)

    ybuf = _ffn(block_expert, n_used, xbuf, w_up[0],
                b_up[0][:, None, 0::2], b_up[0][:, None, 1::2],
                w_down[0], b_down[0][:, None, :])

    out = _combine(dest_flat, gate, x1, mod3, ybuf, S)
    return out.reshape(B, S, D)
```

```python
import functools

import numpy as np
import jax
import jax.numpy as jnp
from jax import lax
from jax.experimental import pallas as pl
from jax.experimental.pallas import tpu as pltpu

F32 = jnp.float32
BF16 = jnp.bfloat16
I32 = jnp.int32

HG_HEADS = 4
HG_DIM = 128
HG_W = HG_HEADS * HG_DIM
ATT_Q_HEADS = 8
ATT_KV_HEADS = 2
ATT_GROUP = ATT_Q_HEADS // ATT_KV_HEADS
ATT_DH = 64
ATT_Q_W = ATT_Q_HEADS * ATT_DH
ATT_KV_W = ATT_KV_HEADS * ATT_DH
ATT_BLOCK = 128
ROPE_THETA = 500000.0
ROPE_DIM = ATT_DH // 4
N_EXPERTS = 32
TOP_K = 4
SWIGLU_ALPHA = 1.702
SWIGLU_LIMIT = 7.0
EPS = 1e-6

V7X_LANES = 128
V7X_SUBLANES = 8
V7X_VMEM_LIMIT_BYTES = 56 * 1024 * 1024

TM_PROJ = 256
TM_MERGE = 512
HG_TILE = 512
HG_CHUNK = 128
HG_SUB = 8
MOE_BM = 512
TM_DISPATCH = 256
TM_COMBINE = 128

NEG_INF = float("-inf")


def _cparams(sem, vmem=V7X_VMEM_LIMIT_BYTES):
    return pltpu.CompilerParams(dimension_semantics=sem, vmem_limit_bytes=vmem)


def _sigmoid(x):
    return 1.0 / (1.0 + jnp.exp(-x))


def _dot(a, b):
    return jnp.dot(a, b, preferred_element_type=F32)


def _dot_nt(a, b):
    return lax.dot_general(a, b, (((1,), (1,)), ((), ())), preferred_element_type=F32)


def _dot_tn(a, b):
    return lax.dot_general(a, b, (((0,), (0,)), ((), ())), preferred_element_type=F32)


ROW_TILE = V7X_SUBLANES


def _store_row_tiles(ref, val):
    rows = val.shape[0]
    for g in range(ROW_TILE):
        ref[pl.ds(g, rows, stride=ROW_TILE), :] = val[:, g * V7X_LANES:(g + 1) * V7X_LANES]


def _load_row_tiles(ref, rows):
    return jnp.concatenate([ref[pl.ds(g, rows, stride=ROW_TILE), :] for g in range(ROW_TILE)], axis=1)


def _dot_exact01(m01, x):
    hi = x.astype(BF16)
    r1 = x - hi.astype(F32)
    mid = r1.astype(BF16)
    lo = (r1 - mid.astype(F32)).astype(BF16)
    return _dot(m01, hi) + _dot(m01, mid) + _dot(m01, lo)


def _ada_kernel(c_ref, w_ref, b_ref, o_ref):
    c = c_ref[...]
    cond = c * _sigmoid(c)
    o_ref[...] = _dot(cond.astype(BF16), w_ref[...].astype(BF16)) + b_ref[...]


def _ada(c, w_ada, b_ada):
    B, D = c.shape
    N = w_ada.shape[1]
    tn = D
    return pl.pallas_call(
        _ada_kernel,
        grid=(N // tn,),
        in_specs=[pl.BlockSpec((B, D), lambda j: (0, 0)),
                  pl.BlockSpec((D, tn), lambda j: (0, j)),
                  pl.BlockSpec((1, tn), lambda j: (0, j))],
        out_specs=pl.BlockSpec((B, tn), lambda j: (0, j)),
        out_shape=jax.ShapeDtypeStruct((B, N), F32),
        compiler_params=_cparams(("parallel",)),
        name="ada",
    )(c, w_ada, b_ada.reshape(1, N))


def _inproj_kernel(x_ref, mod_ref, g_ref, w_ref, hg_ref, at_ref, gt_ref, fmin_ref):
    x = x_ref[...]
    ms = jnp.mean(x * x, axis=-1, keepdims=True)
    y = x * lax.rsqrt(ms + EPS) * g_ref[...]
    h = (y * (1.0 + mod_ref[1:2, :]) + mod_ref[0:1, :]).astype(BF16)
    n_hg = hg_ref.shape[1]
    n_at = at_ref.shape[1]
    n_gt = gt_ref.shape[1]
    hg = _dot(h, w_ref[:, 0:n_hg])
    hg_ref[...] = hg
    f_pre = hg[:, HG_W:2 * HG_W]
    f_min = jnp.min(jnp.min(f_pre, axis=-1, keepdims=True), axis=0, keepdims=True)
    fmin_ref[...] = jnp.broadcast_to(f_min, fmin_ref.shape)
    at_ref[...] = _dot(h, w_ref[:, n_hg:n_hg + n_at])
    gt_ref[...] = _dot(h, w_ref[:, n_hg + n_at:n_hg + n_at + n_gt])


def _inproj(x2, mod3, gain, w_in_bf, S):
    T, D = x2.shape
    tm = TM_PROJ
    n_hg = 4 * HG_W
    n_at = ATT_Q_W + 2 * ATT_KV_W
    n_gt = 2 * D
    assert w_in_bf.shape[1] == n_hg + n_at + n_gt
    per_b = S // tm
    return pl.pallas_call(
        _inproj_kernel,
        grid=(T // tm,),
        in_specs=[pl.BlockSpec((tm, D), lambda i: (i, 0)),
                  pl.BlockSpec((None, 6, D), lambda i: (i // per_b, 0, 0)),
                  pl.BlockSpec((1, D), lambda i: (0, 0)),
                  pl.BlockSpec(w_in_bf.shape, lambda i: (0, 0))],
        out_specs=[pl.BlockSpec((tm, n_hg), lambda i: (i, 0)),
                   pl.BlockSpec((tm, n_at), lambda i: (i, 0)),
                   pl.BlockSpec((tm, n_gt), lambda i: (i, 0)),
                   pl.BlockSpec((V7X_SUBLANES, V7X_LANES), lambda i: (i, 0))],
        out_shape=[jax.ShapeDtypeStruct((T, n_hg), F32),
                   jax.ShapeDtypeStruct((T, n_at), F32),
                   jax.ShapeDtypeStruct((T, n_gt), F32),
                   jax.ShapeDtypeStruct((T // tm * V7X_SUBLANES, V7X_LANES), F32)],
        compiler_params=_cparams(("parallel",)),
        name="inproj",
    )(x2, mod3, gain.reshape(1, D), w_in_bf)


def _hgrn_constants():
    C = HG_CHUNK
    tri = (np.arange(C)[None, :] <= np.arange(C)[:, None]).astype(np.float32)
    halves = []
    lh = C // 2
    while lh >= HG_SUB:
        halves.append(lh)
        lh //= 2
    code = np.zeros((C, C), np.int32)
    ii, jj = np.meshgrid(np.arange(C), np.arange(C), indexing="ij")
    for li, lh in enumerate(halves):
        sel = ((ii // (2 * lh)) == (jj // (2 * lh))) & ((ii % (2 * lh)) >= lh) & ((jj % (2 * lh)) < lh)
        code[sel] = li + 1
    return tri, code, halves


HG_MILD_FMIN = -7.0


def _hgrn_kernel(mild_ref, in_ref, lbl_ref, gain_ref, tri_ref, code_ref, wsum_ref, o_ref, st_ref, *, halves):
    C = HG_CHUNK
    W = HG_W
    n_chunks = in_ref.shape[1] // C
    step_is_mild = mild_ref[pl.program_id(0) * pl.num_programs(1) + pl.program_id(1)] == 1

    @pl.when(pl.program_id(1) == 0)
    def _():
        st_ref[...] = jnp.zeros_like(st_ref)

    l0 = lbl_ref[0:1, :]
    l1 = lbl_ref[1:2, :]
    mx = jnp.maximum(l0, l1)
    e0 = jnp.exp(l0 - mx)
    lb = e0 / (e0 + jnp.exp(l1 - mx))
    gain = gain_ref[...]
    tri = tri_ref[...].astype(BF16)
    code = code_ref[...]
    wsum = wsum_ref[...]
    row = lax.broadcasted_iota(I32, (C, C), 0)
    col = lax.broadcasted_iota(I32, (C, C), 1)
    row_in_sub = lax.broadcasted_iota(I32, (C, HG_DIM), 0) % HG_SUB
    same_sub = (row // HG_SUB) == (col // HG_SUB)
    sub_causal = same_sub & (col <= row)

    def group_row(a, group, r):
        a3 = a.reshape(C // group, group, a.shape[1])
        return jnp.broadcast_to(a3[:, r:r + 1, :], a3.shape).reshape(a.shape)

    def bcast_sub(a, j):
        return group_row(a, HG_SUB, j)

    def chunk(mild, c, carry):
        r0 = pl.multiple_of(c * C, C)
        rows = pl.ds(r0, C)
        for bb in range(in_ref.shape[0]):
            qp = in_ref[bb, rows, 0:W]
            fp = in_ref[bb, rows, W:2 * W]
            vv = in_ref[bb, rows, 2 * W:3 * W]
            gp = in_ref[bb, rows, 3 * W:4 * W]
            forget = lb + (1.0 - lb) * _sigmoid(fp)
            q = qp * _sigmoid(qp)
            k = 1.0 - forget
            lf = jnp.log(forget)
            b = _dot_exact01(tri, lf)
            e_b = jnp.exp(b)
            e_st = jnp.exp(group_row(b, C, C - 1) - b)
            e_lv = [jnp.exp(-jnp.abs(b - group_row(b, 2 * lh, lh - 1))) for lh in halves]
            if mild:
                x_sub = group_row(b, HG_SUB, HG_SUB - 1) - b
                e_subk = jnp.exp(x_sub)
                e_subq = jnp.exp(-x_sub)
            outs = []
            for h in range(HG_HEADS):
                ls = slice(h * HG_DIM, (h + 1) * HG_DIM)
                qh, kh, vh = q[:, ls], k[:, ls], vv[:, ls]
                bh = b[:, ls]
                st = st_ref[bb, h]
                o = _dot_nt((qh * e_b[:, ls]).astype(BF16), st.astype(BF16))
                if mild:
                    s = _dot_nt((qh * e_subq[:, ls]).astype(BF16), (kh * e_subk[:, ls]).astype(BF16))
                    scores = jnp.where(sub_causal, s, 0.0)
                else:
                    ps = []
                    for j in range(HG_SUB):
                        d = jnp.where(row_in_sub >= j, bh - bcast_sub(bh, j), NEG_INF)
                        ps.append((qh * bcast_sub(kh, j) * jnp.exp(d)).astype(BF16))
                    scores = jnp.where(same_sub, _dot(jnp.concatenate(ps, axis=1), wsum), 0.0)
                for li in range(len(halves)):
                    e = e_lv[li][:, ls]
                    s = _dot_nt((qh * e).astype(BF16), (kh * e).astype(BF16))
                    scores = jnp.where(code == li + 1, s, scores)
                o = o + _dot(scores.astype(BF16), vh.astype(BF16))
                kst = (kh * e_st[:, ls]).astype(BF16)
                st_ref[bb, h] = st * e_b[C - 1:C, ls] + _dot_tn(vh.astype(BF16), kst)
                ms = jnp.mean(o * o, axis=-1, keepdims=True)
                on = o * lax.rsqrt(ms + EPS) * gain
                gh = gp[:, ls]
                outs.append(on * (gh * _sigmoid(gh)))
            o_ref[bb, rows, :] = jnp.concatenate(outs, axis=-1).astype(o_ref.dtype)
        return carry

    @pl.when(step_is_mild)
    def _():
        lax.fori_loop(0, n_chunks, functools.partial(chunk, True), 0)

    @pl.when(jnp.logical_not(step_is_mild))
    def _():
        lax.fori_loop(0, n_chunks, functools.partial(chunk, False), 0)


HG_SEQS = 4


def _hgrn(hg_in, f_min, lb_logits, gain, B, S):
    T = hg_in.shape[0]
    lt = HG_TILE
    tri, code, halves = _hgrn_constants()
    nseq = HG_SEQS if B % HG_SEQS == 0 else 1
    step_min = jnp.min(f_min.reshape(B // nseq, nseq, S // lt, lt // TM_PROJ), axis=(1, 3))
    mild = (step_min >= HG_MILD_FMIN).astype(I32).reshape(-1)
    wsum = (np.arange(HG_SUB * HG_DIM)[:, None] // HG_DIM == np.arange(HG_CHUNK)[None, :] % HG_SUB).astype(np.float32)
    const = lambda b, s, m: (0, 0)
    grid_spec = pltpu.PrefetchScalarGridSpec(
        num_scalar_prefetch=1,
        grid=(B // nseq, S // lt),
        in_specs=[pl.BlockSpec((nseq, lt, 4 * HG_W), lambda b, s, m: (b, s, 0)),
                  pl.BlockSpec(lb_logits.shape, const),
                  pl.BlockSpec((1, HG_DIM), const),
                  pl.BlockSpec(tri.shape, const),
                  pl.BlockSpec(code.shape, const),
                  pl.BlockSpec(wsum.shape, const)],
        out_specs=pl.BlockSpec((nseq, lt, HG_W), lambda b, s, m: (b, s, 0)),
        scratch_shapes=[pltpu.VMEM((nseq, HG_HEADS, HG_DIM, HG_DIM), F32)],
    )
    out = pl.pallas_call(
        functools.partial(_hgrn_kernel, halves=tuple(halves)),
        grid_spec=grid_spec,
        out_shape=jax.ShapeDtypeStruct((B, S, HG_W), BF16),
        compiler_params=_cparams(("parallel", "arbitrary")),
        name="hgrn2",
    )(mild, hg_in.reshape(B, S, 4 * HG_W), lb_logits, gain.reshape(1, HG_DIM), jnp.asarray(tri),
      jnp.asarray(code), jnp.asarray(wsum, dtype=BF16))
    return out.reshape(T, HG_W)


SWA_QBLOCKS = 4
ROPE_ROWS = 16


def _swa_constants():
    lane = np.arange(V7X_LANES)
    c = lane % ATT_DH
    half = ROPE_DIM // 2
    inv_freq = ROPE_THETA ** (-jnp.arange(half, dtype=F32) / half)
    freq_rows = jnp.broadcast_to(
        jnp.concatenate([inv_freq, jnp.zeros((ROPE_ROWS - half,), F32)])[:, None], (ROPE_ROWS, V7X_LANES))
    sel = ((np.arange(ROPE_ROWS)[:, None] == (c % half)[None, :]) & (c < ROPE_DIM)[None, :]).astype(np.float32)
    sign = np.where(c < half, -1.0, np.where(c < ROPE_DIM, 1.0, 0.0)).astype(np.float32)
    first = (c < half).astype(np.float32)
    tab = np.stack([sign, first], axis=0)
    g = (lane[:, None] // ATT_DH == lane[None, :] // ATT_DH).astype(np.float32) / ATT_DH
    return freq_rows, sel, tab, g


def _swa_kernel(sink_ref, cur_ref, prev_ref, pcur_ref, pprev_ref, qg_ref, kg_ref, freq_ref, sel_ref, tab_ref, g_ref,
                o_ref):
    Bq = ATT_BLOCK
    n = pl.program_id(1)
    tab = tab_ref[...]
    sign, first = tab[0:1, :], tab[1:2, :]
    gmat = g_ref[...].astype(BF16)
    sel = sel_ref[...].astype(BF16)
    freq_rows = freq_ref[...]
    half = ROPE_DIM // 2
    scale = ATT_DH ** -0.5

    def group_ms(x):
        sq = x * x
        hi = sq.astype(BF16)
        lo = (sq - hi.astype(F32)).astype(BF16)
        return _dot(hi, gmat) + _dot(lo, gmat)

    def spread(a):
        hi = a.astype(BF16)
        r1 = a - hi.astype(F32)
        mid = r1.astype(BF16)
        lo = (r1 - mid.astype(F32)).astype(BF16)
        return _dot_tn(hi, sel) + _dot_tn(mid, sel) + _dot_tn(lo, sel)

    def rope_table(pos_row):
        ang = freq_rows * pos_row
        return 1.0 + spread(jnp.cos(ang) - 1.0), spread(jnp.sin(ang)) * sign

    def norm_rope(x, gain, cs_sn):
        y = x * lax.rsqrt(group_ms(x) + EPS) * gain
        partner = jnp.where(first > 0.5, pltpu.roll(y, V7X_LANES - half, 1), pltpu.roll(y, half, 1))
        return y * cs_sn[0] + partner * cs_sn[1]

    kq = ATT_Q_W
    lane = lax.broadcasted_iota(I32, (Bq, V7X_LANES), 1)
    lo_half = lane < ATT_DH

    def pad_variants(a):
        r = pltpu.roll(a, ATT_DH, 1)
        z = jnp.zeros_like(a)
        return [[jnp.where(lo_half, a, z).astype(BF16), jnp.where(lo_half, z, r).astype(BF16)],
                [jnp.where(lo_half, r, z).astype(BF16), jnp.where(lo_half, z, a).astype(BF16)]]

    tables = [rope_table(pprev_ref[0].astype(F32))]
    kblocks = [pad_variants(norm_rope(prev_ref[:, 0:ATT_KV_W], kg_ref[...], tables[0]))]
    vblocks = [pad_variants(prev_ref[:, ATT_KV_W:2 * ATT_KV_W])]
    for j in range(SWA_QBLOCKS):
        rows = slice(j * Bq, (j + 1) * Bq)
        tables.append(rope_table(pcur_ref[j].astype(F32)))
        kblocks.append(pad_variants(norm_rope(cur_ref[rows, kq:kq + ATT_KV_W], kg_ref[...], tables[j + 1])))
        vblocks.append(pad_variants(cur_ref[rows, kq + ATT_KV_W:kq + 2 * ATT_KV_W]))

    qi = lax.broadcasted_iota(I32, (Bq, 2 * Bq), 0)
    kj = lax.broadcasted_iota(I32, (Bq, 2 * Bq), 1)
    in_band = ((kj < Bq) & (kj > qi)) | ((kj >= Bq) & ((kj - Bq) <= qi))
    first_of_seq = (jnp.zeros((Bq, 2 * Bq), I32) + n) == 0
    for j in range(SWA_QBLOCKS):
        rows = slice(j * Bq, (j + 1) * Bq)
        mask = (in_band & jnp.logical_not(first_of_seq & (kj < Bq))) if j == 0 else in_band
        for t in range(ATT_Q_W // V7X_LANES):
            ls = slice(t * V7X_LANES, (t + 1) * V7X_LANES)
            qt = (norm_rope(cur_ref[rows, ls], qg_ref[...], tables[j + 1]) * scale).astype(BF16)
            acc = jnp.zeros((Bq, V7X_LANES), F32)
            for u in range(2):
                head = 2 * t + u
                kvh = head // ATT_GROUP
                kcat = jnp.concatenate([kblocks[j][kvh][u], kblocks[j + 1][kvh][u]], axis=0)
                vcat = jnp.concatenate([vblocks[j][kvh][u], vblocks[j + 1][kvh][u]], axis=0)
                s = jnp.where(mask, _dot_nt(qt, kcat), NEG_INF)
                sink = sink_ref[head]
                m = jnp.maximum(jnp.max(s, axis=-1, keepdims=True), sink)
                p = jnp.exp(s - m)
                denom = jnp.sum(p, axis=-1, keepdims=True) + jnp.exp(sink - m)
                acc = acc + _dot(p.astype(BF16), vcat) * (1.0 / denom)
            o_ref[rows, ls] = acc.astype(o_ref.dtype)


def _swa(at_in, positions, q_gain, k_gain, sinks, B, S):
    T = at_in.shape[0]
    nb = S // ATT_BLOCK
    qb = SWA_QBLOCKS
    assert nb % qb == 0
    steps = nb // qb
    freq_rows, sel, tab, g = _swa_constants()
    qg = jnp.tile(q_gain.reshape(1, ATT_DH), (1, V7X_LANES // ATT_DH))
    kg = jnp.tile(k_gain.reshape(1, ATT_DH), (1, V7X_LANES // ATT_DH))
    pos3 = positions.reshape(B * nb, 1, ATT_BLOCK)
    n_at = at_in.shape[1]
    kv_blk = 2 * ATT_KV_W
    assert ATT_Q_W % kv_blk == 0
    prev_blk = lambda b, n: b * nb + jnp.maximum(qb * n - 1, 0)
    const = lambda b, n: (0, 0)
    return pl.pallas_call(
        _swa_kernel,
        grid=(B, steps),
        in_specs=[pl.BlockSpec(memory_space=pltpu.SMEM),
                  pl.BlockSpec((qb * ATT_BLOCK, n_at), lambda b, n: (b * steps + n, 0)),
                  pl.BlockSpec((ATT_BLOCK, kv_blk), lambda b, n: (prev_blk(b, n), ATT_Q_W // kv_blk)),
                  pl.BlockSpec((qb, 1, ATT_BLOCK), lambda b, n: (b * steps + n, 0, 0)),
                  pl.BlockSpec((1, 1, ATT_BLOCK), lambda b, n: (prev_blk(b, n), 0, 0)),
                  pl.BlockSpec((1, V7X_LANES), const),
                  pl.BlockSpec((1, V7X_LANES), const),
                  pl.BlockSpec(freq_rows.shape, const),
                  pl.BlockSpec(sel.shape, const),
                  pl.BlockSpec(tab.shape, const),
                  pl.BlockSpec(g.shape, const)],
        out_specs=pl.BlockSpec((qb * ATT_BLOCK, ATT_Q_W), lambda b, n: (b * steps + n, 0)),
        out_shape=jax.ShapeDtypeStruct((T, ATT_Q_W), BF16),
        compiler_params=_cparams(("parallel", "parallel")),
        name="swa",
    )(sinks, at_in, at_in, pos3, pos3, qg, kg, freq_rows, jnp.asarray(sel), jnp.asarray(tab), jnp.asarray(g))


def _merge_router_kernel(x_ref, hg_ref, at_ref, gt_ref, mod_ref, whg_ref, wat_ref, wout_ref, g2_ref,
                         wr_ref, br_ref, tri_ref,
                         x1_ref, h2_ref, idx_ref, gate_ref, rank_ref, cnt_ref, run_ref):
    i = pl.program_id(0)
    D = x_ref.shape[1]
    tm = x_ref.shape[0]

    @pl.when(i == 0)
    def _():
        run_ref[...] = jnp.zeros_like(run_ref)

    y_h = _dot(hg_ref[...], whg_ref[...])
    y_a = _dot(at_ref[...], wat_ref[...])
    merged = _sigmoid(gt_ref[:, 0:D]) * y_h + _sigmoid(gt_ref[:, D:2 * D]) * y_a
    x1 = x_ref[...] + mod_ref[2:3, :] * _dot(merged.astype(BF16), wout_ref[...])
    x1_ref[...] = x1
    ms = jnp.mean(x1 * x1, axis=-1, keepdims=True)
    h2 = x1 * lax.rsqrt(ms + EPS) * g2_ref[...] * (1.0 + mod_ref[4:5, :]) + mod_ref[3:4, :]
    _store_row_tiles(h2_ref, h2)
    logits = _dot(h2.astype(BF16), wr_ref[...]) + br_ref[...]
    E = logits.shape[1]
    lane = lax.broadcasted_iota(I32, (tm, E), 1).astype(F32)
    vals, idxs = [], []
    l = logits
    for _ in range(TOP_K):
        m = jnp.max(l, axis=-1, keepdims=True)
        ik = jnp.min(jnp.where(l == m, lane, float(E)), axis=-1, keepdims=True)
        vals.append(m)
        idxs.append(ik)
        l = jnp.where(lane == ik, NEG_INF, l)
    ex = [jnp.exp(v - vals[0]) for v in vals]
    den = ex[0]
    for e in ex[1:]:
        den = den + e
    onehot = jnp.zeros((tm, E), F32)
    for ik in idxs:
        onehot = onehot + (lane == ik).astype(F32)
    cum = _dot(tri_ref[...], onehot.astype(BF16))
    run = run_ref[0:1, 0:E]
    excl = cum - onehot + run
    lane_k = lax.broadcasted_iota(I32, (tm, TOP_K), 1)
    idx_o = jnp.zeros((tm, TOP_K), I32)
    gate_o = jnp.zeros((tm, TOP_K), F32)
    rank_o = jnp.zeros((tm, TOP_K), I32)
    for kk in range(TOP_K):
        rk = jnp.sum(jnp.where(lane == idxs[kk], excl, 0.0), axis=-1, keepdims=True)
        idx_o = jnp.where(lane_k == kk, idxs[kk].astype(I32), idx_o)
        gate_o = jnp.where(lane_k == kk, ex[kk] / den, gate_o)
        rank_o = jnp.where(lane_k == kk, rk.astype(I32), rank_o)
    idx_ref[...] = idx_o
    gate_ref[...] = gate_o
    rank_ref[...] = rank_o
    new_run = run + cum[tm - 1:tm, :]
    run_ref[0:1, 0:E] = new_run
    cnt_ref[...] = jnp.broadcast_to(new_run, cnt_ref.shape)


def _merge_router(x2, hg_o, at_o, gates, mod3, whg, wat, wout, g2, wr, br, S):
    T, D = x2.shape
    tm = TM_MERGE
    per_b = S // tm
    E = wr.shape[1]
    tri = jnp.asarray(np.tril(np.ones((tm, tm), np.float32)), dtype=BF16)
    row = lambda i: (i, 0)
    const = lambda i: (0, 0)
    return pl.pallas_call(
        _merge_router_kernel,
        grid=(T // tm,),
        in_specs=[pl.BlockSpec((tm, D), row),
                  pl.BlockSpec((tm, HG_W), row),
                  pl.BlockSpec((tm, ATT_Q_W), row),
                  pl.BlockSpec((tm, 2 * D), row),
                  pl.BlockSpec((None, 6, D), lambda i: (i // per_b, 0, 0)),
                  pl.BlockSpec(whg.shape, const),
                  pl.BlockSpec(wat.shape, const),
                  pl.BlockSpec(wout.shape, const),
                  pl.BlockSpec((1, D), const),
                  pl.BlockSpec(wr.shape, const),
                  pl.BlockSpec((1, E), const),
                  pl.BlockSpec((tm, tm), const)],
        out_specs=[pl.BlockSpec((tm, D), row),
                   pl.BlockSpec((tm * ROW_TILE, V7X_LANES), row),
                   pl.BlockSpec((tm, TOP_K), row),
                   pl.BlockSpec((tm, TOP_K), row),
                   pl.BlockSpec((tm, TOP_K), row),
                   pl.BlockSpec((V7X_SUBLANES, E), const)],
        out_shape=[jax.ShapeDtypeStruct((T, D), F32),
                   jax.ShapeDtypeStruct((T * ROW_TILE, V7X_LANES), F32),
                   jax.ShapeDtypeStruct((T, TOP_K), I32),
                   jax.ShapeDtypeStruct((T, TOP_K), F32),
                   jax.ShapeDtypeStruct((T, TOP_K), I32),
                   jax.ShapeDtypeStruct((V7X_SUBLANES, E), F32)],
        scratch_shapes=[pltpu.VMEM((V7X_SUBLANES, V7X_LANES), F32)],
        compiler_params=_cparams(("arbitrary",)),
        name="merge_router",
    )(x2, hg_o, at_o, gates, mod3, whg, wat, wout, g2.reshape(1, D), wr, br.reshape(1, E), tri)


def _dest_kernel(idx_ref, rank_ref, ps_ref, o_ref):
    idx = idx_ref[...]
    tm = idx.shape[0]
    E = ps_ref.shape[1]
    lane = lax.broadcasted_iota(I32, (tm, E), 1)
    lane_k = lax.broadcasted_iota(I32, (tm, TOP_K), 1)
    ps = ps_ref[...].astype(F32)
    out = rank_ref[...]
    for kk in range(TOP_K):
        start = jnp.sum(jnp.where(lane == idx[:, kk:kk + 1], ps, 0.0), axis=-1, keepdims=True)
        out = out + jnp.where(lane_k == kk, start.astype(I32), 0)
    o_ref[...] = out


def _dest(idx, rank, pad_start):
    T = idx.shape[0]
    tm = 1024
    E = pad_start.shape[0]
    row = lambda i: (i, 0)
    return pl.pallas_call(
        _dest_kernel,
        grid=(T // tm,),
        in_specs=[pl.BlockSpec((tm, TOP_K), row), pl.BlockSpec((tm, TOP_K), row),
                  pl.BlockSpec((1, E), lambda i: (0, 0))],
        out_specs=pl.BlockSpec((tm, TOP_K), row),
        out_shape=jax.ShapeDtypeStruct((T, TOP_K), I32),
        compiler_params=_cparams(("parallel",)),
        name="dest",
    )(idx, rank, pad_start.reshape(1, E))


DISPATCH_UNROLL = 4


def _row(ref, r):
    return ref.at[pl.ds(pl.multiple_of(r * ROW_TILE, ROW_TILE), ROW_TILE)]


def _dispatch_kernel(fill_start_ref, fill_n_ref, tail_ref, dest_ref, h_ref, xbuf_ref, zero_ref, sem):
    tm = h_ref.shape[0] // ROW_TILE
    zrows = zero_ref.shape[0]

    def zero_row_copy(r):
        return pltpu.make_async_copy(_row(zero_ref, 0), _row(xbuf_ref, r), sem.at[1])

    def zero_block_copy(b):
        dst = xbuf_ref.at[pl.ds(pl.multiple_of(b * zrows, zrows), zrows)]
        return pltpu.make_async_copy(zero_ref, dst, sem.at[2])

    @pl.when(pl.program_id(0) == 0)
    def _():
        zero_ref[...] = jnp.zeros_like(zero_ref)

        def per_expert(start):
            def body(e, c):
                def rows(r, c2):
                    cp = zero_row_copy(fill_start_ref[e] + r)
                    cp.start() if start else cp.wait()
                    return c2
                return lax.fori_loop(0, fill_n_ref[e], rows, c)
            return body

        def tail(start):
            def body(b, c):
                cp = zero_block_copy(tail_ref[0] + b)
                cp.start() if start else cp.wait()
                return c
            return body

        lax.fori_loop(0, N_EXPERTS, per_expert(True), 0)
        lax.fori_loop(0, tail_ref[1], tail(True), 0)
        lax.fori_loop(0, N_EXPERTS, per_expert(False), 0)
        lax.fori_loop(0, tail_ref[1], tail(False), 0)

    def issue(tb, c):
        for u in range(DISPATCH_UNROLL):
            t = tb * DISPATCH_UNROLL + u
            src = _row(h_ref, t)
            for kk in range(TOP_K):
                d = dest_ref[t * TOP_K + kk]
                pltpu.make_async_copy(src, _row(xbuf_ref, d), sem.at[0]).start(priority=kk % 2)
        return c

    lax.fori_loop(0, tm // DISPATCH_UNROLL, issue, 0)

    def drain(tb, c):
        for _ in range(DISPATCH_UNROLL * TOP_K):
            pltpu.make_async_copy(_row(h_ref, 0), _row(xbuf_ref, 0), sem.at[0]).wait()
        return c

    lax.fori_loop(0, tm // DISPATCH_UNROLL, drain, 0)


def _dispatch(fill_start, fill_n, tail, dest_flat, h2t, P):
    T = h2t.shape[0] // ROW_TILE
    tm = TM_DISPATCH
    grid_spec = pltpu.PrefetchScalarGridSpec(
        num_scalar_prefetch=3,
        grid=(T // tm,),
        in_specs=[pl.BlockSpec((tm * TOP_K,), lambda i, *_: (i,), memory_space=pltpu.SMEM),
                  pl.BlockSpec((tm * ROW_TILE, V7X_LANES), lambda i, *_: (i, 0))],
        out_specs=pl.BlockSpec(memory_space=pl.ANY),
        scratch_shapes=[pltpu.VMEM((MOE_BM * ROW_TILE, V7X_LANES), F32), pltpu.SemaphoreType.DMA((3,))],
    )
    return pl.pallas_call(
        _dispatch_kernel,
        grid_spec=grid_spec,
        out_shape=jax.ShapeDtypeStruct((P * ROW_TILE, V7X_LANES), F32),
        compiler_params=_cparams(("arbitrary",)),
        name="dispatch",
    )(fill_start, fill_n, tail, dest_flat, h2t)


FFN_PREP_COLS = 256
FFN_UNITS_PER_STEP = 1


def _ffn_kernel(cur_ref, src_ref, slot_ref, pos_ref, last_ref, used_ref,
                x_ref, wu_ref, wd_ref, bg_ref, bl_ref, bd_ref, perm_ref, y_ref,
                wg0, wl0, wd0, wg1, wl1, wd1):
    del cur_ref, src_ref
    s = pl.program_id(0)
    bm = x_ref.shape[0] // ROW_TILE
    D, De2 = wu_ref.shape
    De = De2 // 2
    half = FFN_PREP_COLS // 2
    n_units = De2 // FFN_PREP_COLS
    drows = De // n_units
    slot = slot_ref[s]
    used = used_ref[s] == 1
    stages = ((wg0, wl0, wd0), (wg1, wl1, wd1))
    perm = perm_ref[...]

    def stage_unit(u, dst):
        wg, wl, wdb = dst
        c0 = pl.multiple_of(u * FFN_PREP_COLS, FFN_PREP_COLS)
        p = _dot(wu_ref[:, pl.ds(c0, FFN_PREP_COLS)].astype(BF16), perm)
        r0 = pl.multiple_of(u * half, half)
        wg[:, pl.ds(r0, half)] = p[:, 0:half].astype(BF16)
        wl[:, pl.ds(r0, half)] = p[:, half:2 * half].astype(BF16)
        d0 = pl.multiple_of(u * drows, drows)
        wdb[pl.ds(d0, drows), :] = wd_ref[pl.ds(d0, drows), :].astype(BF16)

    @pl.when(s == 0)
    def _():
        for u in range(n_units):
            stage_unit(u, stages[0])

    done = (pos_ref[s] + 1) * FFN_UNITS_PER_STEP

    def block(cur, nxt):
        @pl.when(used)
        def _():
            u0 = jnp.minimum(pos_ref[s] * FFN_UNITS_PER_STEP, n_units - FFN_UNITS_PER_STEP)
            for j in range(FFN_UNITS_PER_STEP):
                stage_unit(u0 + j, nxt)
            wg, wl, wdb = cur
            x = _load_row_tiles(x_ref, bm).astype(BF16)
            glu = _dot(x, wg[...]) + bg_ref[...]
            lin = _dot(x, wl[...]) + bl_ref[...]
            glu = jnp.minimum(glu, SWIGLU_LIMIT)
            lin = jnp.clip(lin, -SWIGLU_LIMIT, SWIGLU_LIMIT)
            act = glu * _sigmoid(SWIGLU_ALPHA * glu) * (lin + 1.0)
            _store_row_tiles(y_ref, _dot(act.astype(BF16), wdb[...]) + bd_ref[...])

        @pl.when(used & (last_ref[s] == 1) & (done < n_units))
        def _():
            def body(u, c):
                stage_unit(u, nxt)
                return c
            lax.fori_loop(done, n_units, body, 0)

    @pl.when(slot == 0)
    def _():
        block(stages[0], stages[1])

    @pl.when(slot == 1)
    def _():
        block(stages[1], stages[0])

    @pl.when(jnp.logical_not(used) & (s > 0))
    def _():
        y_ref[...] = jnp.zeros_like(y_ref)


def _ffn_schedule(block_expert, n_used):
    n = block_expert.shape[0]
    idx = jnp.arange(n, dtype=I32)
    be = block_expert
    first = jnp.concatenate([jnp.ones((1,), bool), be[1:] != be[:-1]])
    run_start = lax.cummax(jnp.where(first, idx, 0))
    ordinal = jnp.cumsum(first.astype(I32)) - 1
    is_last = jnp.concatenate([first[1:], jnp.ones((1,), bool)])
    next_first = lax.cummin(jnp.where(first, idx, n), reverse=True)
    next_start = jnp.concatenate([next_first[1:], jnp.full((1,), n, I32)])
    next_e = be[jnp.minimum(next_start, n - 1)]
    blk = jnp.maximum(jnp.arange(n + 1, dtype=I32) - 1, 0)
    step = jnp.arange(n + 1, dtype=I32)
    cur = be[blk]
    src = jnp.where(step == 0, be[0], next_e[blk])
    slot = ordinal[blk] % 2
    pos = blk - run_start[blk]
    last = is_last[blk].astype(I32)
    used = ((step >= 1) & (blk < n_used[0])).astype(I32)
    return [a.astype(I32) for a in (cur, src, slot, pos, last, used)]


def _ffn(block_expert, n_used, xbuf, w_up, bg, bl, w_down, bd):
    P = xbuf.shape[0] // ROW_TILE
    bm = MOE_BM
    _, D, De2 = w_up.shape
    De = De2 // 2
    assert (De2 // FFN_PREP_COLS) % FFN_UNITS_PER_STEP == 0
    sched = _ffn_schedule(block_expert, n_used)
    cc = np.arange(FFN_PREP_COLS)
    perm = (cc[:, None] == np.where(cc < FFN_PREP_COLS // 2, 2 * cc, 2 * (cc - FFN_PREP_COLS // 2) + 1)[None, :])
    rows = lambda s, *_: (jnp.maximum(s - 1, 0), 0)
    wmap = lambda s, cur, src, *_: (src[s], 0, 0)
    bmap = lambda s, cur, *_: (cur[s], 0, 0)
    grid_spec = pltpu.PrefetchScalarGridSpec(
        num_scalar_prefetch=len(sched),
        grid=(P // bm + 1,),
        in_specs=[pl.BlockSpec((bm * ROW_TILE, V7X_LANES), rows),
                  pl.BlockSpec((None, D, De2), wmap),
                  pl.BlockSpec((None, De, D), wmap),
                  pl.BlockSpec((None, 1, De), bmap),
                  pl.BlockSpec((None, 1, De), bmap),
                  pl.BlockSpec((None, 1, D), bmap),
                  pl.BlockSpec(perm.shape, lambda s, *_: (0, 0))],
        out_specs=pl.BlockSpec((bm * ROW_TILE, V7X_LANES), rows),
        scratch_shapes=[pltpu.VMEM((D, De), BF16), pltpu.VMEM((D, De), BF16), pltpu.VMEM((De, D), BF16)] * 2,
    )
    return pl.pallas_call(
        _ffn_kernel,
        grid_spec=grid_spec,
        out_shape=jax.ShapeDtypeStruct((P * ROW_TILE, V7X_LANES), F32),
        compiler_params=_cparams(("arbitrary",)),
        name="expert_ffn",
    )(*sched, xbuf, w_up, w_down, bg, bl, bd, jnp.asarray(perm, dtype=BF16))


def _combine_kernel(dcur_ref, dnext_ref, gate_ref, x1_ref, mod_ref, y_hbm, o_ref, buf, sem):
    i = pl.program_id(0)
    n = pl.num_programs(0)
    tm = x1_ref.shape[0]
    slot = i % 2

    def issue(dref, s):
        def body(tb, c):
            for u in range(DISPATCH_UNROLL):
                t = tb * DISPATCH_UNROLL + u
                for kk in range(TOP_K):
                    d = dref[t * TOP_K + kk]
                    pltpu.make_async_copy(_row(y_hbm, d), _row(buf.at[s, kk], t), sem.at[s]).start(priority=kk % 2)
            return c
        lax.fori_loop(0, tm // DISPATCH_UNROLL, body, 0)

    @pl.when(i == 0)
    def _():
        issue(dcur_ref, 0)

    @pl.when(i + 1 < n)
    def _():
        issue(dnext_ref, 1 - slot)

    def drain(tb, c):
        for _ in range(DISPATCH_UNROLL * TOP_K):
            pltpu.make_async_copy(_row(y_hbm, 0), _row(buf.at[slot, 0], 0), sem.at[slot]).wait()
        return c

    lax.fori_loop(0, tm // DISPATCH_UNROLL, drain, 0)
    gate = gate_ref[...]
    acc = gate[:, 0:1] * _load_row_tiles(buf.at[slot, 0], tm)
    for kk in range(1, TOP_K):
        acc = acc + gate[:, kk:kk + 1] * _load_row_tiles(buf.at[slot, kk], tm)
    o_ref[...] = x1_ref[...] + mod_ref[5:6, :] * acc


def _combine(dest_flat, gate, x1, mod3, ybuf, S):
    T, D = x1.shape
    tm = TM_COMBINE
    per_b = S // tm
    nt = T // tm
    return pl.pallas_call(
        _combine_kernel,
        grid=(nt,),
        in_specs=[pl.BlockSpec((tm * TOP_K,), lambda i: (i,), memory_space=pltpu.SMEM),
                  pl.BlockSpec((tm * TOP_K,), lambda i: (jnp.minimum(i + 1, nt - 1),), memory_space=pltpu.SMEM),
                  pl.BlockSpec((tm, TOP_K), lambda i: (i, 0)),
                  pl.BlockSpec((tm, D), lambda i: (i, 0)),
                  pl.BlockSpec((None, 6, D), lambda i: (i // per_b, 0, 0)),
                  pl.BlockSpec(memory_space=pl.ANY)],
        out_specs=pl.BlockSpec((tm, D), lambda i: (i, 0)),
        out_shape=jax.ShapeDtypeStruct((T, D), F32),
        scratch_shapes=[pltpu.VMEM((2, TOP_K, tm * ROW_TILE, V7X_LANES), F32), pltpu.SemaphoreType.DMA((2,))],
        compiler_params=_cparams(("arbitrary",)),
        name="combine",
    )(dest_flat, dest_flat, gate, x1, mod3, ybuf)


def kernel(x, c, positions, w_ada, b_ada, norm1_gain, w_in, lower_bound_logits, hg_norm_gain, w_hg_branch,
           q_norm_gain, k_norm_gain, attn_sinks, w_attn_branch, w_out, norm2_gain, w_router, b_router,
           w_up, b_up, w_down, b_down):
    B, S, D = x.shape
    T = B * S
    assert w_ada.shape[0] == 1, "one layer"
    x2 = x.reshape(T, D)

    mod = _ada(c, w_ada[0], b_ada[0])
    mod3 = mod.reshape(B, 6, D)

    hg_in, at_in, gates, f_min = _inproj(x2, mod3, norm1_gain[0], w_in[0].astype(BF16), S)
    f_min = f_min[::V7X_SUBLANES, 0].reshape(B, S // TM_PROJ)
    hg_o = _hgrn(hg_in, f_min, lower_bound_logits, hg_norm_gain[0], B, S)
    at_o = _swa(at_in, positions, q_norm_gain[0], k_norm_gain[0], attn_sinks[0], B, S)

    x1, h2, idx, gate, rank, cnt = _merge_router(
        x2, hg_o, at_o, gates, mod3, w_hg_branch[0].astype(BF16), w_attn_branch[0].astype(BF16),
        w_out[0].astype(BF16), norm2_gain[0], w_router[0].astype(BF16), b_router[0], S)

    bm = MOE_BM
    counts = cnt[0].astype(I32)
    padded = (counts + bm - 1) // bm * bm
    pad_end = jnp.cumsum(padded)
    pad_start = pad_end - padded
    P = T * TOP_K + N_EXPERTS * bm
    n_blocks = P // bm
    block_start = jnp.arange(n_blocks, dtype=I32) * bm
    block_expert = jnp.minimum(
        jnp.sum((pad_end[None, :] <= block_start[:, None]).astype(I32), axis=1), N_EXPERTS - 1).astype(I32)
    n_used = (pad_end[-1:] // bm).astype(I32)
    tail = jnp.concatenate([n_used, n_blocks - n_used]).astype(I32)

    dest = _dest(idx, rank, pad_start)
    dest_flat = dest.reshape(T * TOP_K)
    xbuf = _dispatch((pad_start + counts).astype(I32), (padded - counts).astype(I32), tail, dest_flat, h2, P)

    ybuf = _ffn(block_expert, n_used, xbuf, w_up[0],
                b_up[0][:, None, 0::2], b_up[0][:, None, 1::2],
                w_down[0], b_down[0][:, None, :])

    out = _combine(dest_flat, gate, x1, mod3, ybuf, S)
    return out.reshape(B, S, D)
```

```python
import functools

import numpy as np
import jax
import jax.numpy as jnp
from jax import lax
from jax.experimental import pallas as pl
from jax.experimental.pallas import tpu as pltpu

F32 = jnp.float32
BF16 = jnp.bfloat16
I32 = jnp.int32

HG_HEADS = 4
HG_DIM = 128
HG_W = HG_HEADS * HG_DIM
ATT_Q_HEADS = 8
ATT_KV_HEADS = 2
ATT_GROUP = ATT_Q_HEADS // ATT_KV_HEADS
ATT_DH = 64
ATT_Q_W = ATT_Q_HEADS * ATT_DH
ATT_KV_W = ATT_KV_HEADS * ATT_DH
ATT_BLOCK = 128
ROPE_THETA = 500000.0
ROPE_DIM = ATT_DH // 4
N_EXPERTS = 32
TOP_K = 4
SWIGLU_ALPHA = 1.702
SWIGLU_LIMIT = 7.0
EPS = 1e-6

V7X_LANES = 128
V7X_SUBLANES = 8
V7X_VMEM_LIMIT_BYTES = 56 * 1024 * 1024

TM_PROJ = 256
TM_MERGE = 512
HG_TILE = 512
HG_CHUNK = 128
HG_SUB = 8
MOE_BM = 512
TM_DISPATCH = 512
TM_COMBINE = 256

NEG_INF = float("-inf")


def _cparams(sem, vmem=V7X_VMEM_LIMIT_BYTES):
    return pltpu.CompilerParams(dimension_semantics=sem, vmem_limit_bytes=vmem)


def _sigmoid(x):
    return 1.0 / (1.0 + jnp.exp(-x))


def _dot(a, b):
    return jnp.dot(a, b, preferred_element_type=F32)


def _dot_nt(a, b):
    return lax.dot_general(a, b, (((1,), (1,)), ((), ())), preferred_element_type=F32)


def _dot_tn(a, b):
    return lax.dot_general(a, b, (((0,), (0,)), ((), ())), preferred_element_type=F32)


ROW_TILE = V7X_SUBLANES


def _store_row_tiles(ref, val):
    rows = val.shape[0]
    for g in range(ROW_TILE):
        ref[pl.ds(g, rows, stride=ROW_TILE), :] = val[:, g * V7X_LANES:(g + 1) * V7X_LANES]


def _load_row_tiles(ref, rows):
    return jnp.concatenate([ref[pl.ds(g, rows, stride=ROW_TILE), :] for g in range(ROW_TILE)], axis=1)


def _dot_exact01(m01, x):
    hi = x.astype(BF16)
    r1 = x - hi.astype(F32)
    mid = r1.astype(BF16)
    lo = (r1 - mid.astype(F32)).astype(BF16)
    return _dot(m01, hi) + _dot(m01, mid) + _dot(m01, lo)


def _ada_kernel(c_ref, w_ref, b_ref, o_ref):
    c = c_ref[...]
    cond = c * _sigmoid(c)
    o_ref[...] = _dot(cond.astype(BF16), w_ref[...].astype(BF16)) + b_ref[...]


def _ada(c, w_ada, b_ada):
    B, D = c.shape
    N = w_ada.shape[1]
    tn = D
    return pl.pallas_call(
        _ada_kernel,
        grid=(N // tn,),
        in_specs=[pl.BlockSpec((B, D), lambda j: (0, 0)),
                  pl.BlockSpec((D, tn), lambda j: (0, j)),
                  pl.BlockSpec((1, tn), lambda j: (0, j))],
        out_specs=pl.BlockSpec((B, tn), lambda j: (0, j)),
        out_shape=jax.ShapeDtypeStruct((B, N), F32),
        compiler_params=_cparams(("parallel",)),
        name="ada",
    )(c, w_ada, b_ada.reshape(1, N))


def _inproj_kernel(x_ref, mod_ref, g_ref, w_ref, hg_ref, at_ref, gt_ref, fmin_ref):
    x = x_ref[...]
    ms = jnp.mean(x * x, axis=-1, keepdims=True)
    y = x * lax.rsqrt(ms + EPS) * g_ref[...]
    h = (y * (1.0 + mod_ref[1:2, :]) + mod_ref[0:1, :]).astype(BF16)
    n_hg = hg_ref.shape[1]
    n_at = at_ref.shape[1]
    n_gt = gt_ref.shape[1]
    hg = _dot(h, w_ref[:, 0:n_hg])
    hg_ref[...] = hg
    f_pre = hg[:, HG_W:2 * HG_W]
    f_min = jnp.min(jnp.min(f_pre, axis=-1, keepdims=True), axis=0, keepdims=True)
    fmin_ref[...] = jnp.broadcast_to(f_min, fmin_ref.shape)
    at_ref[...] = _dot(h, w_ref[:, n_hg:n_hg + n_at])
    gt_ref[...] = _dot(h, w_ref[:, n_hg + n_at:n_hg + n_at + n_gt])


def _inproj(x2, mod3, gain, w_in_bf, S):
    T, D = x2.shape
    tm = TM_PROJ
    n_hg = 4 * HG_W
    n_at = ATT_Q_W + 2 * ATT_KV_W
    n_gt = 2 * D
    assert w_in_bf.shape[1] == n_hg + n_at + n_gt
    per_b = S // tm
    return pl.pallas_call(
        _inproj_kernel,
        grid=(T // tm,),
        in_specs=[pl.BlockSpec((tm, D), lambda i: (i, 0)),
                  pl.BlockSpec((None, 6, D), lambda i: (i // per_b, 0, 0)),
                  pl.BlockSpec((1, D), lambda i: (0, 0)),
                  pl.BlockSpec(w_in_bf.shape, lambda i: (0, 0))],
        out_specs=[pl.BlockSpec((tm, n_hg), lambda i: (i, 0)),
                   pl.BlockSpec((tm, n_at), lambda i: (i, 0)),
                   pl.BlockSpec((tm, n_gt), lambda i: (i, 0)),
                   pl.BlockSpec((V7X_SUBLANES, V7X_LANES), lambda i: (i, 0))],
        out_shape=[jax.ShapeDtypeStruct((T, n_hg), F32),
                   jax.ShapeDtypeStruct((T, n_at), F32),
                   jax.ShapeDtypeStruct((T, n_gt), F32),
                   jax.ShapeDtypeStruct((T // tm * V7X_SUBLANES, V7X_LANES), F32)],
        compiler_params=_cparams(("parallel",)),
        name="inproj",
    )(x2, mod3, gain.reshape(1, D), w_in_bf)


def _hgrn_constants():
    C = HG_CHUNK
    tri = (np.arange(C)[None, :] <= np.arange(C)[:, None]).astype(np.float32)
    halves = []
    lh = C // 2
    while lh >= HG_SUB:
        halves.append(lh)
        lh //= 2
    code = np.zeros((C, C), np.int32)
    ii, jj = np.meshgrid(np.arange(C), np.arange(C), indexing="ij")
    for li, lh in enumerate(halves):
        sel = ((ii // (2 * lh)) == (jj // (2 * lh))) & ((ii % (2 * lh)) >= lh) & ((jj % (2 * lh)) < lh)
        code[sel] = li + 1
    return tri, code, halves


HG_MILD_FMIN = -7.0


def _hgrn_kernel(mild_ref, in_ref, lbl_ref, gain_ref, tri_ref, code_ref, wsum_ref, o_ref, st_ref, *, halves):
    C = HG_CHUNK
    W = HG_W
    n_chunks = in_ref.shape[1] // C
    step_is_mild = mild_ref[pl.program_id(0) * pl.num_programs(1) + pl.program_id(1)] == 1

    @pl.when(pl.program_id(1) == 0)
    def _():
        st_ref[...] = jnp.zeros_like(st_ref)

    l0 = lbl_ref[0:1, :]
    l1 = lbl_ref[1:2, :]
    mx = jnp.maximum(l0, l1)
    e0 = jnp.exp(l0 - mx)
    lb = e0 / (e0 + jnp.exp(l1 - mx))
    gain = gain_ref[...]
    tri = tri_ref[...].astype(BF16)
    code = code_ref[...]
    wsum = wsum_ref[...]
    row = lax.broadcasted_iota(I32, (C, C), 0)
    col = lax.broadcasted_iota(I32, (C, C), 1)
    row_in_sub = lax.broadcasted_iota(I32, (C, HG_DIM), 0) % HG_SUB
    same_sub = (row // HG_SUB) == (col // HG_SUB)
    sub_causal = same_sub & (col <= row)

    def group_row(a, group, r):
        a3 = a.reshape(C // group, group, a.shape[1])
        return jnp.broadcast_to(a3[:, r:r + 1, :], a3.shape).reshape(a.shape)

    def bcast_sub(a, j):
        return group_row(a, HG_SUB, j)

    def chunk(mild, c, carry):
        r0 = pl.multiple_of(c * C, C)
        rows = pl.ds(r0, C)
        for bb in range(in_ref.shape[0]):
            qp = in_ref[bb, rows, 0:W]
            fp = in_ref[bb, rows, W:2 * W]
            vv = in_ref[bb, rows, 2 * W:3 * W]
            gp = in_ref[bb, rows, 3 * W:4 * W]
            forget = lb + (1.0 - lb) * _sigmoid(fp)
            q = qp * _sigmoid(qp)
            k = 1.0 - forget
            lf = jnp.log(forget)
            b = _dot_exact01(tri, lf)
            e_b = jnp.exp(b)
            e_st = jnp.exp(group_row(b, C, C - 1) - b)
            e_lv = [jnp.exp(-jnp.abs(b - group_row(b, 2 * lh, lh - 1))) for lh in halves]
            if mild:
                x_sub = group_row(b, HG_SUB, HG_SUB - 1) - b
                e_subk = jnp.exp(x_sub)
                e_subq = jnp.exp(-x_sub)
            outs = []
            for h in range(HG_HEADS):
                ls = slice(h * HG_DIM, (h + 1) * HG_DIM)
                qh, kh, vh = q[:, ls], k[:, ls], vv[:, ls]
                bh = b[:, ls]
                st = st_ref[bb, h]
                o = _dot_nt((qh * e_b[:, ls]).astype(BF16), st.astype(BF16))
                if mild:
                    s = _dot_nt((qh * e_subq[:, ls]).astype(BF16), (kh * e_subk[:, ls]).astype(BF16))
                    scores = jnp.where(sub_causal, s, 0.0)
                else:
                    ps = []
                    for j in range(HG_SUB):
                        d = jnp.where(row_in_sub >= j, bh - bcast_sub(bh, j), NEG_INF)
                        ps.append((qh * bcast_sub(kh, j) * jnp.exp(d)).astype(BF16))
                    scores = jnp.where(same_sub, _dot(jnp.concatenate(ps, axis=1), wsum), 0.0)
                for li in range(len(halves)):
                    e = e_lv[li][:, ls]
                    s = _dot_nt((qh * e).astype(BF16), (kh * e).astype(BF16))
                    scores = jnp.where(code == li + 1, s, scores)
                o = o + _dot(scores.astype(BF16), vh.astype(BF16))
                kst = (kh * e_st[:, ls]).astype(BF16)
                st_ref[bb, h] = st * e_b[C - 1:C, ls] + _dot_tn(vh.astype(BF16), kst)
                ms = jnp.mean(o * o, axis=-1, keepdims=True)
                on = o * lax.rsqrt(ms + EPS) * gain
                gh = gp[:, ls]
                outs.append(on * (gh * _sigmoid(gh)))
            o_ref[bb, rows, :] = jnp.concatenate(outs, axis=-1).astype(o_ref.dtype)
        return carry

    @pl.when(step_is_mild)
    def _():
        lax.fori_loop(0, n_chunks, functools.partial(chunk, True), 0)

    @pl.when(jnp.logical_not(step_is_mild))
    def _():
        lax.fori_loop(0, n_chunks, functools.partial(chunk, False), 0)


HG_SEQS = 4


def _hgrn(hg_in, f_min, lb_logits, gain, B, S):
    T = hg_in.shape[0]
    lt = HG_TILE
    tri, code, halves = _hgrn_constants()
    nseq = HG_SEQS if B % HG_SEQS == 0 else 1
    step_min = jnp.min(f_min.reshape(B // nseq, nseq, S // lt, lt // TM_PROJ), axis=(1, 3))
    mild = (step_min >= HG_MILD_FMIN).astype(I32).reshape(-1)
    wsum = (np.arange(HG_SUB * HG_DIM)[:, None] // HG_DIM == np.arange(HG_CHUNK)[None, :] % HG_SUB).astype(np.float32)
    const = lambda b, s, m: (0, 0)
    grid_spec = pltpu.PrefetchScalarGridSpec(
        num_scalar_prefetch=1,
        grid=(B // nseq, S // lt),
        in_specs=[pl.BlockSpec((nseq, lt, 4 * HG_W), lambda b, s, m: (b, s, 0)),
                  pl.BlockSpec(lb_logits.shape, const),
                  pl.BlockSpec((1, HG_DIM), const),
                  pl.BlockSpec(tri.shape, const),
                  pl.BlockSpec(code.shape, const),
                  pl.BlockSpec(wsum.shape, const)],
        out_specs=pl.BlockSpec((nseq, lt, HG_W), lambda b, s, m: (b, s, 0)),
        scratch_shapes=[pltpu.VMEM((nseq, HG_HEADS, HG_DIM, HG_DIM), F32)],
    )
    out = pl.pallas_call(
        functools.partial(_hgrn_kernel, halves=tuple(halves)),
        grid_spec=grid_spec,
        out_shape=jax.ShapeDtypeStruct((B, S, HG_W), BF16),
        compiler_params=_cparams(("parallel", "arbitrary")),
        name="hgrn2",
    )(mild, hg_in.reshape(B, S, 4 * HG_W), lb_logits, gain.reshape(1, HG_DIM), jnp.asarray(tri),
      jnp.asarray(code), jnp.asarray(wsum, dtype=BF16))
    return out.reshape(T, HG_W)


SWA_QBLOCKS = 8
ROPE_ROWS = 16


def _swa_constants():
    lane = np.arange(V7X_LANES)
    c = lane % ATT_DH
    half = ROPE_DIM // 2
    inv_freq = ROPE_THETA ** (-jnp.arange(half, dtype=F32) / half)
    freq_rows = jnp.broadcast_to(
        jnp.concatenate([inv_freq, jnp.zeros((ROPE_ROWS - half,), F32)])[:, None], (ROPE_ROWS, V7X_LANES))
    sel = ((np.arange(ROPE_ROWS)[:, None] == (c % half)[None, :]) & (c < ROPE_DIM)[None, :]).astype(np.float32)
    sign = np.where(c < half, -1.0, np.where(c < ROPE_DIM, 1.0, 0.0)).astype(np.float32)
    first = (c < half).astype(np.float32)
    tab = np.stack([sign, first], axis=0)
    g = (lane[:, None] // ATT_DH == lane[None, :] // ATT_DH).astype(np.float32) / ATT_DH
    return freq_rows, sel, tab, g


def _swa_kernel(sink_ref, cur_ref, prev_ref, pcur_ref, pprev_ref, qg_ref, kg_ref, freq_ref, sel_ref, tab_ref, g_ref,
                o_ref):
    Bq = ATT_BLOCK
    n = pl.program_id(1)
    tab = tab_ref[...]
    sign, first = tab[0:1, :], tab[1:2, :]
    gmat = g_ref[...].astype(BF16)
    sel = sel_ref[...].astype(BF16)
    freq_rows = freq_ref[...]
    half = ROPE_DIM // 2
    scale = ATT_DH ** -0.5

    def group_ms(x):
        sq = x * x
        hi = sq.astype(BF16)
        lo = (sq - hi.astype(F32)).astype(BF16)
        return _dot(hi, gmat) + _dot(lo, gmat)

    def spread(a):
        hi = a.astype(BF16)
        r1 = a - hi.astype(F32)
        mid = r1.astype(BF16)
        lo = (r1 - mid.astype(F32)).astype(BF16)
        return _dot_tn(hi, sel) + _dot_tn(mid, sel) + _dot_tn(lo, sel)

    def rope_table(pos_row):
        ang = freq_rows * pos_row
        return 1.0 + spread(jnp.cos(ang) - 1.0), spread(jnp.sin(ang)) * sign

    def norm_rope(x, gain, cs_sn):
        y = x * lax.rsqrt(group_ms(x) + EPS) * gain
        partner = jnp.where(first > 0.5, pltpu.roll(y, V7X_LANES - half, 1), pltpu.roll(y, half, 1))
        return y * cs_sn[0] + partner * cs_sn[1]

    kq = ATT_Q_W
    lane = lax.broadcasted_iota(I32, (Bq, V7X_LANES), 1)
    lo_half = lane < ATT_DH

    def pad_variants(a):
        r = pltpu.roll(a, ATT_DH, 1)
        z = jnp.zeros_like(a)
        return [[jnp.where(lo_half, a, z).astype(BF16), jnp.where(lo_half, z, r).astype(BF16)],
                [jnp.where(lo_half, r, z).astype(BF16), jnp.where(lo_half, z, a).astype(BF16)]]

    tables = [rope_table(pprev_ref[0].astype(F32))]
    kblocks = [pad_variants(norm_rope(prev_ref[:, 0:ATT_KV_W], kg_ref[...], tables[0]))]
    vblocks = [pad_variants(prev_ref[:, ATT_KV_W:2 * ATT_KV_W])]
    for j in range(SWA_QBLOCKS):
        rows = slice(j * Bq, (j + 1) * Bq)
        tables.append(rope_table(pcur_ref[j].astype(F32)))
        kblocks.append(pad_variants(norm_rope(cur_ref[rows, kq:kq + ATT_KV_W], kg_ref[...], tables[j + 1])))
        vblocks.append(pad_variants(cur_ref[rows, kq + ATT_KV_W:kq + 2 * ATT_KV_W]))

    qi = lax.broadcasted_iota(I32, (Bq, 2 * Bq), 0)
    kj = lax.broadcasted_iota(I32, (Bq, 2 * Bq), 1)
    in_band = ((kj < Bq) & (kj > qi)) | ((kj >= Bq) & ((kj - Bq) <= qi))
    first_of_seq = (jnp.zeros((Bq, 2 * Bq), I32) + n) == 0
    for j in range(SWA_QBLOCKS):
        rows = slice(j * Bq, (j + 1) * Bq)
        mask = (in_band & jnp.logical_not(first_of_seq & (kj < Bq))) if j == 0 else in_band
        for t in range(ATT_Q_W // V7X_LANES):
            ls = slice(t * V7X_LANES, (t + 1) * V7X_LANES)
            qt = (norm_rope(cur_ref[rows, ls], qg_ref[...], tables[j + 1]) * scale).astype(BF16)
            acc = jnp.zeros((Bq, V7X_LANES), F32)
            for u in range(2):
                head = 2 * t + u
                kvh = head // ATT_GROUP
                kcat = jnp.concatenate([kblocks[j][kvh][u], kblocks[j + 1][kvh][u]], axis=0)
                vcat = jnp.concatenate([vblocks[j][kvh][u], vblocks[j + 1][kvh][u]], axis=0)
                s = jnp.where(mask, _dot_nt(qt, kcat), NEG_INF)
                sink = sink_ref[head]
                m = jnp.maximum(jnp.max(s, axis=-1, keepdims=True), sink)
                p = jnp.exp(s - m)
                denom = jnp.sum(p, axis=-1, keepdims=True) + jnp.exp(sink - m)
                acc = acc + _dot(p.astype(BF16), vcat) * (1.0 / denom)
            o_ref[rows, ls] = acc.astype(o_ref.dtype)


def _swa(at_in, positions, q_gain, k_gain, sinks, B, S):
    T = at_in.shape[0]
    nb = S // ATT_BLOCK
    qb = SWA_QBLOCKS
    assert nb % qb == 0
    steps = nb // qb
    freq_rows, sel, tab, g = _swa_constants()
    qg = jnp.tile(q_gain.reshape(1, ATT_DH), (1, V7X_LANES // ATT_DH))
    kg = jnp.tile(k_gain.reshape(1, ATT_DH), (1, V7X_LANES // ATT_DH))
    pos3 = positions.reshape(B * nb, 1, ATT_BLOCK)
    n_at = at_in.shape[1]
    kv_blk = 2 * ATT_KV_W
    assert ATT_Q_W % kv_blk == 0
    prev_blk = lambda b, n: b * nb + jnp.maximum(qb * n - 1, 0)
    const = lambda b, n: (0, 0)
    return pl.pallas_call(
        _swa_kernel,
        grid=(B, steps),
        in_specs=[pl.BlockSpec(memory_space=pltpu.SMEM),
                  pl.BlockSpec((qb * ATT_BLOCK, n_at), lambda b, n: (b * steps + n, 0)),
                  pl.BlockSpec((ATT_BLOCK, kv_blk), lambda b, n: (prev_blk(b, n), ATT_Q_W // kv_blk)),
                  pl.BlockSpec((qb, 1, ATT_BLOCK), lambda b, n: (b * steps + n, 0, 0)),
                  pl.BlockSpec((1, 1, ATT_BLOCK), lambda b, n: (prev_blk(b, n), 0, 0)),
                  pl.BlockSpec((1, V7X_LANES), const),
                  pl.BlockSpec((1, V7X_LANES), const),
                  pl.BlockSpec(freq_rows.shape, const),
                  pl.BlockSpec(sel.shape, const),
                  pl.BlockSpec(tab.shape, const),
                  pl.BlockSpec(g.shape, const)],
        out_specs=pl.BlockSpec((qb * ATT_BLOCK, ATT_Q_W), lambda b, n: (b * steps + n, 0)),
        out_shape=jax.ShapeDtypeStruct((T, ATT_Q_W), BF16),
        compiler_params=_cparams(("parallel", "parallel")),
        name="swa",
    )(sinks, at_in, at_in, pos3, pos3, qg, kg, freq_rows, jnp.asarray(sel), jnp.asarray(tab), jnp.asarray(g))


def _merge_router_kernel(x_ref, hg_ref, at_ref, gt_ref, mod_ref, whg_ref, wat_ref, wout_ref, g2_ref,
                         wr_ref, br_ref, tri_ref,
                         x1_ref, h2_ref, idx_ref, gate_ref, rank_ref, cnt_ref, run_ref):
    i = pl.program_id(0)
    D = x_ref.shape[1]
    tm = x_ref.shape[0]

    @pl.when(i == 0)
    def _():
        run_ref[...] = jnp.zeros_like(run_ref)

    y_h = _dot(hg_ref[...], whg_ref[...])
    y_a = _dot(at_ref[...], wat_ref[...])
    merged = _sigmoid(gt_ref[:, 0:D]) * y_h + _sigmoid(gt_ref[:, D:2 * D]) * y_a
    x1 = x_ref[...] + mod_ref[2:3, :] * _dot(merged.astype(BF16), wout_ref[...])
    x1_ref[...] = x1
    ms = jnp.mean(x1 * x1, axis=-1, keepdims=True)
    h2 = x1 * lax.rsqrt(ms + EPS) * g2_ref[...] * (1.0 + mod_ref[4:5, :]) + mod_ref[3:4, :]
    _store_row_tiles(h2_ref, h2)
    logits = _dot(h2.astype(BF16), wr_ref[...]) + br_ref[...]
    E = logits.shape[1]
    lane = lax.broadcasted_iota(I32, (tm, E), 1).astype(F32)
    vals, idxs = [], []
    l = logits
    for _ in range(TOP_K):
        m = jnp.max(l, axis=-1, keepdims=True)
        ik = jnp.min(jnp.where(l == m, lane, float(E)), axis=-1, keepdims=True)
        vals.append(m)
        idxs.append(ik)
        l = jnp.where(lane == ik, NEG_INF, l)
    ex = [jnp.exp(v - vals[0]) for v in vals]
    den = ex[0]
    for e in ex[1:]:
        den = den + e
    onehot = jnp.zeros((tm, E), F32)
    for ik in idxs:
        onehot = onehot + (lane == ik).astype(F32)
    cum = _dot(tri_ref[...], onehot.astype(BF16))
    run = run_ref[0:1, 0:E]
    excl = cum - onehot + run
    lane_k = lax.broadcasted_iota(I32, (tm, TOP_K), 1)
    idx_o = jnp.zeros((tm, TOP_K), I32)
    gate_o = jnp.zeros((tm, TOP_K), F32)
    rank_o = jnp.zeros((tm, TOP_K), I32)
    for kk in range(TOP_K):
        rk = jnp.sum(jnp.where(lane == idxs[kk], excl, 0.0), axis=-1, keepdims=True)
        idx_o = jnp.where(lane_k == kk, idxs[kk].astype(I32), idx_o)
        gate_o = jnp.where(lane_k == kk, ex[kk] / den, gate_o)
        rank_o = jnp.where(lane_k == kk, rk.astype(I32), rank_o)
    idx_ref[...] = idx_o
    gate_ref[...] = gate_o
    rank_ref[...] = rank_o
    new_run = run + cum[tm - 1:tm, :]
    run_ref[0:1, 0:E] = new_run
    cnt_ref[...] = jnp.broadcast_to(new_run, cnt_ref.shape)


def _merge_router(x2, hg_o, at_o, gates, mod3, whg, wat, wout, g2, wr, br, S):
    T, D = x2.shape
    tm = TM_MERGE
    per_b = S // tm
    E = wr.shape[1]
    tri = jnp.asarray(np.tril(np.ones((tm, tm), np.float32)), dtype=BF16)
    row = lambda i: (i, 0)
    const = lambda i: (0, 0)
    return pl.pallas_call(
        _merge_router_kernel,
        grid=(T // tm,),
        in_specs=[pl.BlockSpec((tm, D), row),
                  pl.BlockSpec((tm, HG_W), row),
                  pl.BlockSpec((tm, ATT_Q_W), row),
                  pl.BlockSpec((tm, 2 * D), row),
                  pl.BlockSpec((None, 6, D), lambda i: (i // per_b, 0, 0)),
                  pl.BlockSpec(whg.shape, const),
                  pl.BlockSpec(wat.shape, const),
                  pl.BlockSpec(wout.shape, const),
                  pl.BlockSpec((1, D), const),
                  pl.BlockSpec(wr.shape, const),
                  pl.BlockSpec((1, E), const),
                  pl.BlockSpec((tm, tm), const)],
        out_specs=[pl.BlockSpec((tm, D), row),
                   pl.BlockSpec((tm * ROW_TILE, V7X_LANES), row),
                   pl.BlockSpec((tm, TOP_K), row),
                   pl.BlockSpec((tm, TOP_K), row),
                   pl.BlockSpec((tm, TOP_K), row),
                   pl.BlockSpec((V7X_SUBLANES, E), const)],
        out_shape=[jax.ShapeDtypeStruct((T, D), F32),
                   jax.ShapeDtypeStruct((T * ROW_TILE, V7X_LANES), F32),
                   jax.ShapeDtypeStruct((T, TOP_K), I32),
                   jax.ShapeDtypeStruct((T, TOP_K), F32),
                   jax.ShapeDtypeStruct((T, TOP_K), I32),
                   jax.ShapeDtypeStruct((V7X_SUBLANES, E), F32)],
        scratch_shapes=[pltpu.VMEM((V7X_SUBLANES, V7X_LANES), F32)],
        compiler_params=_cparams(("arbitrary",)),
        name="merge_router",
    )(x2, hg_o, at_o, gates, mod3, whg, wat, wout, g2.reshape(1, D), wr, br.reshape(1, E), tri)


def _dest_kernel(idx_ref, rank_ref, ps_ref, o_ref):
    idx = idx_ref[...]
    tm = idx.shape[0]
    E = ps_ref.shape[1]
    lane = lax.broadcasted_iota(I32, (tm, E), 1)
    lane_k = lax.broadcasted_iota(I32, (tm, TOP_K), 1)
    ps = ps_ref[...].astype(F32)
    out = rank_ref[...]
    for kk in range(TOP_K):
        start = jnp.sum(jnp.where(lane == idx[:, kk:kk + 1], ps, 0.0), axis=-1, keepdims=True)
        out = out + jnp.where(lane_k == kk, start.astype(I32), 0)
    o_ref[...] = out


def _dest(idx, rank, pad_start):
    T = idx.shape[0]
    tm = 1024
    E = pad_start.shape[0]
    row = lambda i: (i, 0)
    return pl.pallas_call(
        _dest_kernel,
        grid=(T // tm,),
        in_specs=[pl.BlockSpec((tm, TOP_K), row), pl.BlockSpec((tm, TOP_K), row),
                  pl.BlockSpec((1, E), lambda i: (0, 0))],
        out_specs=pl.BlockSpec((tm, TOP_K), row),
        out_shape=jax.ShapeDtypeStruct((T, TOP_K), I32),
        compiler_params=_cparams(("parallel",)),
        name="dest",
    )(idx, rank, pad_start.reshape(1, E))


DISPATCH_UNROLL = 4


def _row(ref, r):
    return ref.at[pl.ds(pl.multiple_of(r * ROW_TILE, ROW_TILE), ROW_TILE)]


def _dispatch_kernel(fill_start_ref, fill_n_ref, tail_ref, dest_ref, h_ref, xbuf_ref, zero_ref, sem):
    tm = h_ref.shape[0] // ROW_TILE
    zrows = zero_ref.shape[0]

    def zero_row_copy(r):
        return pltpu.make_async_copy(_row(zero_ref, 0), _row(xbuf_ref, r), sem.at[1])

    def zero_block_copy(b):
        dst = xbuf_ref.at[pl.ds(pl.multiple_of(b * zrows, zrows), zrows)]
        return pltpu.make_async_copy(zero_ref, dst, sem.at[2])

    @pl.when(pl.program_id(0) == 0)
    def _():
        zero_ref[...] = jnp.zeros_like(zero_ref)

        def per_expert(start):
            def body(e, c):
                def rows(r, c2):
                    cp = zero_row_copy(fill_start_ref[e] + r)
                    cp.start() if start else cp.wait()
                    return c2
                return lax.fori_loop(0, fill_n_ref[e], rows, c)
            return body

        def tail(start):
            def body(b, c):
                cp = zero_block_copy(tail_ref[0] + b)
                cp.start() if start else cp.wait()
                return c
            return body

        lax.fori_loop(0, N_EXPERTS, per_expert(True), 0)
        lax.fori_loop(0, tail_ref[1], tail(True), 0)
        lax.fori_loop(0, N_EXPERTS, per_expert(False), 0)
        lax.fori_loop(0, tail_ref[1], tail(False), 0)

    def issue(tb, c):
        for u in range(DISPATCH_UNROLL):
            t = tb * DISPATCH_UNROLL + u
            src = _row(h_ref, t)
            for kk in range(TOP_K):
                d = dest_ref[t * TOP_K + kk]
                pltpu.make_async_copy(src, _row(xbuf_ref, d), sem.at[0]).start(priority=kk % 2)
        return c

    lax.fori_loop(0, tm // DISPATCH_UNROLL, issue, 0)

    def drain(tb, c):
        for _ in range(DISPATCH_UNROLL * TOP_K):
            pltpu.make_async_copy(_row(h_ref, 0), _row(xbuf_ref, 0), sem.at[0]).wait()
        return c

    lax.fori_loop(0, tm // DISPATCH_UNROLL, drain, 0)


def _dispatch(fill_start, fill_n, tail, dest_flat, h2t, P):
    T = h2t.shape[0] // ROW_TILE
    tm = TM_DISPATCH
    grid_spec = pltpu.PrefetchScalarGridSpec(
        num_scalar_prefetch=3,
        grid=(T // tm,),
        in_specs=[pl.BlockSpec((tm * TOP_K,), lambda i, *_: (i,), memory_space=pltpu.SMEM),
                  pl.BlockSpec((tm * ROW_TILE, V7X_LANES), lambda i, *_: (i, 0))],
        out_specs=pl.BlockSpec(memory_space=pl.ANY),
        scratch_shapes=[pltpu.VMEM((MOE_BM * ROW_TILE, V7X_LANES), F32), pltpu.SemaphoreType.DMA((3,))],
    )
    return pl.pallas_call(
        _dispatch_kernel,
        grid_spec=grid_spec,
        out_shape=jax.ShapeDtypeStruct((P * ROW_TILE, V7X_LANES), F32),
        compiler_params=_cparams(("arbitrary",)),
        name="dispatch",
    )(fill_start, fill_n, tail, dest_flat, h2t)


FFN_PREP_COLS = 256
FFN_UNITS_PER_STEP = 1


def _ffn_kernel(cur_ref, src_ref, slot_ref, pos_ref, last_ref, used_ref,
                x_ref, wu_ref, wd_ref, bg_ref, bl_ref, bd_ref, perm_ref, y_ref,
                wg0, wl0, wd0, wg1, wl1, wd1):
    del cur_ref, src_ref
    s = pl.program_id(0)
    bm = x_ref.shape[0] // ROW_TILE
    D, De2 = wu_ref.shape
    De = De2 // 2
    half = FFN_PREP_COLS // 2
    n_units = De2 // FFN_PREP_COLS
    drows = De // n_units
    slot = slot_ref[s]
    used = used_ref[s] == 1
    stages = ((wg0, wl0, wd0), (wg1, wl1, wd1))
    perm = perm_ref[...]

    def stage_unit(u, dst):
        wg, wl, wdb = dst
        c0 = pl.multiple_of(u * FFN_PREP_COLS, FFN_PREP_COLS)
        p = _dot(wu_ref[:, pl.ds(c0, FFN_PREP_COLS)].astype(BF16), perm)
        r0 = pl.multiple_of(u * half, half)
        wg[:, pl.ds(r0, half)] = p[:, 0:half].astype(BF16)
        wl[:, pl.ds(r0, half)] = p[:, half:2 * half].astype(BF16)
        d0 = pl.multiple_of(u * drows, drows)
        wdb[pl.ds(d0, drows), :] = wd_ref[pl.ds(d0, drows), :].astype(BF16)

    @pl.when(s == 0)
    def _():
        for u in range(n_units):
            stage_unit(u, stages[0])

    done = (pos_ref[s] + 1) * FFN_UNITS_PER_STEP

    def block(cur, nxt):
        @pl.when(used)
        def _():
            u0 = jnp.minimum(pos_ref[s] * FFN_UNITS_PER_STEP, n_units - FFN_UNITS_PER_STEP)
            for j in range(FFN_UNITS_PER_STEP):
                stage_unit(u0 + j, nxt)
            wg, wl, wdb = cur
            x = _load_row_tiles(x_ref, bm).astype(BF16)
            glu = _dot(x, wg[...]) + bg_ref[...]
            lin = _dot(x, wl[...]) + bl_ref[...]
            glu = jnp.minimum(glu, SWIGLU_LIMIT)
            lin = jnp.clip(lin, -SWIGLU_LIMIT, SWIGLU_LIMIT)
            act = glu * _sigmoid(SWIGLU_ALPHA * glu) * (lin + 1.0)
            _store_row_tiles(y_ref, _dot(act.astype(BF16), wdb[...]) + bd_ref[...])

        @pl.when(used & (last_ref[s] == 1) & (done < n_units))
        def _():
            def body(u, c):
                stage_unit(u, nxt)
                return c
            lax.fori_loop(done, n_units, body, 0)

    @pl.when(slot == 0)
    def _():
        block(stages[0], stages[1])

    @pl.when(slot == 1)
    def _():
        block(stages[1], stages[0])

    @pl.when(jnp.logical_not(used) & (s > 0))
    def _():
        y_ref[...] = jnp.zeros_like(y_ref)


def _ffn_schedule(block_expert, n_used):
    n = block_expert.shape[0]
    idx = jnp.arange(n, dtype=I32)
    be = block_expert
    first = jnp.concatenate([jnp.ones((1,), bool), be[1:] != be[:-1]])
    run_start = lax.cummax(jnp.where(first, idx, 0))
    ordinal = jnp.cumsum(first.astype(I32)) - 1
    is_last = jnp.concatenate([first[1:], jnp.ones((1,), bool)])
    next_first = lax.cummin(jnp.where(first, idx, n), reverse=True)
    next_start = jnp.concatenate([next_first[1:], jnp.full((1,), n, I32)])
    next_e = be[jnp.minimum(next_start, n - 1)]
    blk = jnp.maximum(jnp.arange(n + 1, dtype=I32) - 1, 0)
    step = jnp.arange(n + 1, dtype=I32)
    cur = be[blk]
    src = jnp.where(step == 0, be[0], next_e[blk])
    slot = ordinal[blk] % 2
    pos = blk - run_start[blk]
    last = is_last[blk].astype(I32)
    used = ((step >= 1) & (blk < n_used[0])).astype(I32)
    return [a.astype(I32) for a in (cur, src, slot, pos, last, used)]


def _ffn(block_expert, n_used, xbuf, w_up, bg, bl, w_down, bd):
    P = xbuf.shape[0] // ROW_TILE
    bm = MOE_BM
    _, D, De2 = w_up.shape
    De = De2 // 2
    assert (De2 // FFN_PREP_COLS) % FFN_UNITS_PER_STEP == 0
    sched = _ffn_schedule(block_expert, n_used)
    cc = np.arange(FFN_PREP_COLS)
    perm = (cc[:, None] == np.where(cc < FFN_PREP_COLS // 2, 2 * cc, 2 * (cc - FFN_PREP_COLS // 2) + 1)[None, :])
    rows = lambda s, *_: (jnp.maximum(s - 1, 0), 0)
    wmap = lambda s, cur, src, *_: (src[s], 0, 0)
    bmap = lambda s, cur, *_: (cur[s], 0, 0)
    grid_spec = pltpu.PrefetchScalarGridSpec(
        num_scalar_prefetch=len(sched),
        grid=(P // bm + 1,),
        in_specs=[pl.BlockSpec((bm * ROW_TILE, V7X_LANES), rows),
                  pl.BlockSpec((None, D, De2), wmap),
                  pl.BlockSpec((None, De, D), wmap),
                  pl.BlockSpec((None, 1, De), bmap),
                  pl.BlockSpec((None, 1, De), bmap),
                  pl.BlockSpec((None, 1, D), bmap),
                  pl.BlockSpec(perm.shape, lambda s, *_: (0, 0))],
        out_specs=pl.BlockSpec((bm * ROW_TILE, V7X_LANES), rows),
        scratch_shapes=[pltpu.VMEM((D, De), BF16), pltpu.VMEM((D, De), BF16), pltpu.VMEM((De, D), BF16)] * 2,
    )
    return pl.pallas_call(
        _ffn_kernel,
        grid_spec=grid_spec,
        out_shape=jax.ShapeDtypeStruct((P * ROW_TILE, V7X_LANES), F32),
        compiler_params=_cparams(("arbitrary",)),
        name="expert_ffn",
    )(*sched, xbuf, w_up, w_down, bg, bl, bd, jnp.asarray(perm, dtype=BF16))


def _combine_kernel(dcur_ref, dnext_ref, gate_ref, x1_ref, mod_ref, y_hbm, o_ref, buf, sem):
    i = pl.program_id(0)
    n = pl.num_programs(0)
    tm = x1_ref.shape[0]
    slot = i % 2

    def issue(dref, s):
        def body(tb, c):
            for u in range(DISPATCH_UNROLL):
                t = tb * DISPATCH_UNROLL + u
                for kk in range(TOP_K):
                    d = dref[t * TOP_K + kk]
                    pltpu.make_async_copy(_row(y_hbm, d), _row(buf.at[s, kk], t), sem.at[s]).start(priority=kk % 2)
            return c
        lax.fori_loop(0, tm // DISPATCH_UNROLL, body, 0)

    @pl.when(i == 0)
    def _():
        issue(dcur_ref, 0)

    @pl.when(i + 1 < n)
    def _():
        issue(dnext_ref, 1 - slot)

    def drain(tb, c):
        for _ in range(DISPATCH_UNROLL * TOP_K):
            pltpu.make_async_copy(_row(y_hbm, 0), _row(buf.at[slot, 0], 0), sem.at[slot]).wait()
        return c

    lax.fori_loop(0, tm // DISPATCH_UNROLL, drain, 0)
    gate = gate_ref[...]
    acc = gate[:, 0:1] * _load_row_tiles(buf.at[slot, 0], tm)
    for kk in range(1, TOP_K):
        acc = acc + gate[:, kk:kk + 1] * _load_row_tiles(buf.at[slot, kk], tm)
    o_ref[...] = x1_ref[...] + mod_ref[5:6, :] * acc


def _combine(dest_flat, gate, x1, mod3, ybuf, S):
    T, D = x1.shape
    tm = TM_COMBINE
    per_b = S // tm
    nt = T // tm
    return pl.pallas_call(
        _combine_kernel,
        grid=(nt,),
        in_specs=[pl.BlockSpec((tm * TOP_K,), lambda i: (i,), memory_space=pltpu.SMEM),
                  pl.BlockSpec((tm * TOP_K,), lambda i: (jnp.minimum(i + 1, nt - 1),), memory_space=pltpu.SMEM),
                  pl.BlockSpec((tm, TOP_K), lambda i: (i, 0)),
                  pl.BlockSpec((tm, D), lambda i: (i, 0)),
                  pl.BlockSpec((None, 6, D), lambda i: (i // per_b, 0, 0)),
                  pl.BlockSpec(memory_space=pl.ANY)],
        out_specs=pl.BlockSpec((tm, D), lambda i: (i, 0)),
        out_shape=jax.ShapeDtypeStruct((T, D), F32),
        scratch_shapes=[pltpu.VMEM((2, TOP_K, tm * ROW_TILE, V7X_LANES), F32), pltpu.SemaphoreType.DMA((2,))],
        compiler_params=_cparams(("arbitrary",)),
        name="combine",
    )(dest_flat, dest_flat, gate, x1, mod3, ybuf)


def kernel(x, c, positions, w_ada, b_ada, norm1_gain, w_in, lower_bound_logits, hg_norm_gain, w_hg_branch,
           q_norm_gain, k_norm_gain, attn_sinks, w_attn_branch, w_out, norm2_gain, w_router, b_router,
           w_up, b_up, w_down, b_down):
    B, S, D = x.shape
    T = B * S
    assert w_ada.shape[0] == 1, "one layer"
    x2 = x.reshape(T, D)

    mod = _ada(c, w_ada[0], b_ada[0])
    mod3 = mod.reshape(B, 6, D)

    hg_in, at_in, gates, f_min = _inproj(x2, mod3, norm1_gain[0], w_in[0].astype(BF16), S)
    f_min = f_min[::V7X_SUBLANES, 0].reshape(B, S // TM_PROJ)
    hg_o = _hgrn(hg_in, f_min, lower_bound_logits, hg_norm_gain[0], B, S)
    at_o = _swa(at_in, positions, q_norm_gain[0], k_norm_gain[0], attn_sinks[0], B, S)

    x1, h2, idx, gate, rank, cnt = _merge_router(
        x2, hg_o, at_o, gates, mod3, w_hg_branch[0].astype(BF16), w_attn_branch[0].astype(BF16),
        w_out[0].astype(BF16), norm2_gain[0], w_router[0].astype(BF16), b_router[0], S)

    bm = MOE_BM
    counts = cnt[0].astype(I32)
    padded = (counts + bm - 1) // bm * bm
    pad_end = jnp.cumsum(padded)
    pad_start = pad_end - padded
    P = T * TOP_K + N_EXPERTS * bm
    n_blocks = P // bm
    block_start = jnp.arange(n_blocks, dtype=I32) * bm
    block_expert = jnp.minimum(
        jnp.sum((pad_end[None, :] <= block_start[:, None]).astype(I32), axis=1), N_EXPERTS - 1).astype(I32)
    n_used = (pad_end[-1:] // bm).astype(I32)
    tail = jnp.concatenate([n_used, n_blocks - n_used]).astype(I32)

    dest = _dest(idx, rank, pad_start)
    dest_flat = dest.reshape(T * TOP_K)
    xbuf = _dispatch((pad_start + counts).astype(I32), (padded - counts).astype(I32), tail, dest_flat, h2, P)

    ybuf = _ffn(block_expert, n_used, xbuf, w_up[0],
                b_up[0][:, None, 0::2], b_up[0][:, None, 1::2],
                w_down[0], b_down[0][:, None, :])

    out = _combine(dest_flat, gate, x1, mod3, ybuf, S)
    return out.reshape(B, S, D)
```

```python
import functools

import numpy as np
import jax
import jax.numpy as jnp
from jax import lax
from jax.experimental import pallas as pl
from jax.experimental.pallas import tpu as pltpu

F32 = jnp.float32
BF16 = jnp.bfloat16
I32 = jnp.int32

HG_HEADS = 4
HG_DIM = 128
HG_W = HG_HEADS * HG_DIM
ATT_Q_HEADS = 8
ATT_KV_HEADS = 2
ATT_GROUP = ATT_Q_HEADS // ATT_KV_HEADS
ATT_DH = 64
ATT_Q_W = ATT_Q_HEADS * ATT_DH
ATT_KV_W = ATT_KV_HEADS * ATT_DH
ATT_BLOCK = 128
ROPE_THETA = 500000.0
ROPE_DIM = ATT_DH // 4
N_EXPERTS = 32
TOP_K = 4
SWIGLU_ALPHA = 1.702
SWIGLU_LIMIT = 7.0
EPS = 1e-6

V7X_LANES = 128
V7X_SUBLANES = 8
V7X_VMEM_LIMIT_BYTES = 56 * 1024 * 1024

TM_PROJ = 256
TM_MERGE = 512
HG_TILE = 512
HG_CHUNK = 128
HG_SUB = 8
MOE_BM = 512
TM_DISPATCH = 512
TM_COMBINE = 256

NEG_INF = float("-inf")


def _cparams(sem, vmem=V7X_VMEM_LIMIT_BYTES):
    return pltpu.CompilerParams(dimension_semantics=sem, vmem_limit_bytes=vmem)


def _sigmoid(x):
    return 1.0 / (1.0 + jnp.exp(-x))


def _dot(a, b):
    return jnp.dot(a, b, preferred_element_type=F32)


def _dot_nt(a, b):
    return lax.dot_general(a, b, (((1,), (1,)), ((), ())), preferred_element_type=F32)


def _dot_tn(a, b):
    return lax.dot_general(a, b, (((0,), (0,)), ((), ())), preferred_element_type=F32)


ROW_TILE = V7X_SUBLANES


def _store_row_tiles(ref, val):
    rows = val.shape[0]
    for g in range(ROW_TILE):
        ref[pl.ds(g, rows, stride=ROW_TILE), :] = val[:, g * V7X_LANES:(g + 1) * V7X_LANES]


def _load_row_tiles(ref, rows):
    return jnp.concatenate([ref[pl.ds(g, rows, stride=ROW_TILE), :] for g in range(ROW_TILE)], axis=1)


def _dot_exact01(m01, x):
    hi = x.astype(BF16)
    r1 = x - hi.astype(F32)
    mid = r1.astype(BF16)
    lo = (r1 - mid.astype(F32)).astype(BF16)
    return _dot(m01, hi) + _dot(m01, mid) + _dot(m01, lo)


def _ada_kernel(c_ref, w_ref, b_ref, o_ref):
    c = c_ref[...]
    cond = c * _sigmoid(c)
    o_ref[...] = _dot(cond.astype(BF16), w_ref[...].astype(BF16)) + b_ref[...]


def _ada(c, w_ada, b_ada):
    B, D = c.shape
    N = w_ada.shape[1]
    tn = D
    return pl.pallas_call(
        _ada_kernel,
        grid=(N // tn,),
        in_specs=[pl.BlockSpec((B, D), lambda j: (0, 0)),
                  pl.BlockSpec((D, tn), lambda j: (0, j)),
                  pl.BlockSpec((1, tn), lambda j: (0, j))],
        out_specs=pl.BlockSpec((B, tn), lambda j: (0, j)),
        out_shape=jax.ShapeDtypeStruct((B, N), F32),
        compiler_params=_cparams(("parallel",)),
        name="ada",
    )(c, w_ada, b_ada.reshape(1, N))


def _inproj_kernel(x_ref, mod_ref, g_ref, w_ref, hg_ref, at_ref, gt_ref, fmin_ref):
    x = x_ref[...]
    ms = jnp.mean(x * x, axis=-1, keepdims=True)
    y = x * lax.rsqrt(ms + EPS) * g_ref[...]
    h = (y * (1.0 + mod_ref[1:2, :]) + mod_ref[0:1, :]).astype(BF16)
    n_hg = hg_ref.shape[1]
    n_at = at_ref.shape[1]
    n_gt = gt_ref.shape[1]
    hg = _dot(h, w_ref[:, 0:n_hg])
    hg_ref[...] = hg
    f_pre = hg[:, HG_W:2 * HG_W]
    f_min = jnp.min(jnp.min(f_pre, axis=-1, keepdims=True), axis=0, keepdims=True)
    fmin_ref[...] = jnp.broadcast_to(f_min, fmin_ref.shape)
    at_ref[...] = _dot(h, w_ref[:, n_hg:n_hg + n_at])
    gt_ref[...] = _dot(h, w_ref[:, n_hg + n_at:n_hg + n_at + n_gt])


def _inproj(x2, mod3, gain, w_in_bf, S):
    T, D = x2.shape
    tm = TM_PROJ
    n_hg = 4 * HG_W
    n_at = ATT_Q_W + 2 * ATT_KV_W
    n_gt = 2 * D
    assert w_in_bf.shape[1] == n_hg + n_at + n_gt
    per_b = S // tm
    return pl.pallas_call(
        _inproj_kernel,
        grid=(T // tm,),
        in_specs=[pl.BlockSpec((tm, D), lambda i: (i, 0)),
                  pl.BlockSpec((None, 6, D), lambda i: (i // per_b, 0, 0)),
                  pl.BlockSpec((1, D), lambda i: (0, 0)),
                  pl.BlockSpec(w_in_bf.shape, lambda i: (0, 0))],
        out_specs=[pl.BlockSpec((tm, n_hg), lambda i: (i, 0)),
                   pl.BlockSpec((tm, n_at), lambda i: (i, 0)),
                   pl.BlockSpec((tm, n_gt), lambda i: (i, 0)),
                   pl.BlockSpec((V7X_SUBLANES, V7X_LANES), lambda i: (i, 0))],
        out_shape=[jax.ShapeDtypeStruct((T, n_hg), F32),
                   jax.ShapeDtypeStruct((T, n_at), F32),
                   jax.ShapeDtypeStruct((T, n_gt), F32),
                   jax.ShapeDtypeStruct((T // tm * V7X_SUBLANES, V7X_LANES), F32)],
        compiler_params=_cparams(("parallel",)),
        name="inproj",
    )(x2, mod3, gain.reshape(1, D), w_in_bf)


def _hgrn_constants():
    C = HG_CHUNK
    tri = (np.arange(C)[None, :] <= np.arange(C)[:, None]).astype(np.float32)
    halves = []
    lh = C // 2
    while lh >= HG_SUB:
        halves.append(lh)
        lh //= 2
    code = np.zeros((C, C), np.int32)
    ii, jj = np.meshgrid(np.arange(C), np.arange(C), indexing="ij")
    for li, lh in enumerate(halves):
        sel = ((ii // (2 * lh)) == (jj // (2 * lh))) & ((ii % (2 * lh)) >= lh) & ((jj % (2 * lh)) < lh)
        code[sel] = li + 1
    return tri, code, halves


HG_MILD_FMIN = -7.0


def _hgrn_kernel(mild_ref, in_ref, lbl_ref, gain_ref, tri_ref, code_ref, wsum_ref, o_ref, st_ref, *, halves):
    C = HG_CHUNK
    W = HG_W
    n_chunks = in_ref.shape[1] // C
    step_is_mild = mild_ref[pl.program_id(0) * pl.num_programs(1) + pl.program_id(1)] == 1

    @pl.when(pl.program_id(1) == 0)
    def _():
        st_ref[...] = jnp.zeros_like(st_ref)

    l0 = lbl_ref[0:1, :]
    l1 = lbl_ref[1:2, :]
    mx = jnp.maximum(l0, l1)
    e0 = jnp.exp(l0 - mx)
    lb = e0 / (e0 + jnp.exp(l1 - mx))
    gain = gain_ref[...]
    tri = tri_ref[...].astype(BF16)
    code = code_ref[...]
    wsum = wsum_ref[...]
    row = lax.broadcasted_iota(I32, (C, C), 0)
    col = lax.broadcasted_iota(I32, (C, C), 1)
    row_in_sub = lax.broadcasted_iota(I32, (C, HG_DIM), 0) % HG_SUB
    same_sub = (row // HG_SUB) == (col // HG_SUB)
    sub_causal = same_sub & (col <= row)

    def group_row(a, group, r):
        a3 = a.reshape(C // group, group, a.shape[1])
        return jnp.broadcast_to(a3[:, r:r + 1, :], a3.shape).reshape(a.shape)

    def bcast_sub(a, j):
        return group_row(a, HG_SUB, j)

    def chunk(mild, c, carry):
        r0 = pl.multiple_of(c * C, C)
        rows = pl.ds(r0, C)
        for bb in range(in_ref.shape[0]):
            qp = in_ref[bb, rows, 0:W]
            fp = in_ref[bb, rows, W:2 * W]
            vv = in_ref[bb, rows, 2 * W:3 * W]
            gp = in_ref[bb, rows, 3 * W:4 * W]
            forget = lb + (1.0 - lb) * _sigmoid(fp)
            q = qp * _sigmoid(qp)
            k = 1.0 - forget
            lf = jnp.log(forget)
            b = _dot_exact01(tri, lf)
            e_b = jnp.exp(b)
            e_st = jnp.exp(group_row(b, C, C - 1) - b)
            e_lv = [jnp.exp(-jnp.abs(b - group_row(b, 2 * lh, lh - 1))) for lh in halves]
            if mild:
                x_sub = group_row(b, HG_SUB, HG_SUB - 1) - b
                e_subk = jnp.exp(x_sub)
                e_subq = jnp.exp(-x_sub)
            outs = []
            for h in range(HG_HEADS):
                ls = slice(h * HG_DIM, (h + 1) * HG_DIM)
                qh, kh, vh = q[:, ls], k[:, ls], vv[:, ls]
                bh = b[:, ls]
                st = st_ref[bb, h]
                o = _dot_nt((qh * e_b[:, ls]).astype(BF16), st.astype(BF16))
                if mild:
                    s = _dot_nt((qh * e_subq[:, ls]).astype(BF16), (kh * e_subk[:, ls]).astype(BF16))
                    scores = jnp.where(sub_causal, s, 0.0)
                else:
                    ps = []
                    for j in range(HG_SUB):
                        d = jnp.where(row_in_sub >= j, bh - bcast_sub(bh, j), NEG_INF)
                        ps.append((qh * bcast_sub(kh, j) * jnp.exp(d)).astype(BF16))
                    scores = jnp.where(same_sub, _dot(jnp.concatenate(ps, axis=1), wsum), 0.0)
                for li in range(len(halves)):
                    e = e_lv[li][:, ls]
                    s = _dot_nt((qh * e).astype(BF16), (kh * e).astype(BF16))
                    scores = jnp.where(code == li + 1, s, scores)
                o = o + _dot(scores.astype(BF16), vh.astype(BF16))
                kst = (kh * e_st[:, ls]).astype(BF16)
                st_ref[bb, h] = st * e_b[C - 1:C, ls] + _dot_tn(vh.astype(BF16), kst)
                ms = jnp.mean(o * o, axis=-1, keepdims=True)
                on = o * lax.rsqrt(ms + EPS) * gain
                gh = gp[:, ls]
                outs.append(on * (gh * _sigmoid(gh)))
            o_ref[bb, rows, :] = jnp.concatenate(outs, axis=-1).astype(o_ref.dtype)
        return carry

    @pl.when(step_is_mild)
    def _():
        lax.fori_loop(0, n_chunks, functools.partial(chunk, True), 0)

    @pl.when(jnp.logical_not(step_is_mild))
    def _():
        lax.fori_loop(0, n_chunks, functools.partial(chunk, False), 0)


HG_SEQS = 4


def _hgrn(hg_in, f_min, lb_logits, gain, B, S):
    T = hg_in.shape[0]
    lt = HG_TILE
    tri, code, halves = _hgrn_constants()
    nseq = HG_SEQS if B % HG_SEQS == 0 else 1
    step_min = jnp.min(f_min.reshape(B // nseq, nseq, S // lt, lt // TM_PROJ), axis=(1, 3))
    mild = (step_min >= HG_MILD_FMIN).astype(I32).reshape(-1)
    wsum = (np.arange(HG_SUB * HG_DIM)[:, None] // HG_DIM == np.arange(HG_CHUNK)[None, :] % HG_SUB).astype(np.float32)
    const = lambda b, s, m: (0, 0)
    grid_spec = pltpu.PrefetchScalarGridSpec(
        num_scalar_prefetch=1,
        grid=(B // nseq, S // lt),
        in_specs=[pl.BlockSpec((nseq, lt, 4 * HG_W), lambda b, s, m: (b, s, 0)),
                  pl.BlockSpec(lb_logits.shape, const),
                  pl.BlockSpec((1, HG_DIM), const),
                  pl.BlockSpec(tri.shape, const),
                  pl.BlockSpec(code.shape, const),
                  pl.BlockSpec(wsum.shape, const)],
        out_specs=pl.BlockSpec((nseq, lt, HG_W), lambda b, s, m: (b, s, 0)),
        scratch_shapes=[pltpu.VMEM((nseq, HG_HEADS, HG_DIM, HG_DIM), F32)],
    )
    out = pl.pallas_call(
        functools.partial(_hgrn_kernel, halves=tuple(halves)),
        grid_spec=grid_spec,
        out_shape=jax.ShapeDtypeStruct((B, S, HG_W), BF16),
        compiler_params=_cparams(("parallel", "arbitrary")),
        name="hgrn2",
    )(mild, hg_in.reshape(B, S, 4 * HG_W), lb_logits, gain.reshape(1, HG_DIM), jnp.asarray(tri),
      jnp.asarray(code), jnp.asarray(wsum, dtype=BF16))
    return out.reshape(T, HG_W)


SWA_QBLOCKS = 8
ROPE_ROWS = 16


def _swa_constants():
    lane = np.arange(V7X_LANES)
    c = lane % ATT_DH
    half = ROPE_DIM // 2
    inv_freq = ROPE_THETA ** (-jnp.arange(half, dtype=F32) / half)
    freq_rows = jnp.broadcast_to(
        jnp.concatenate([inv_freq, jnp.zeros((ROPE_ROWS - half,), F32)])[:, None], (ROPE_ROWS, V7X_LANES))
    sel = ((np.arange(ROPE_ROWS)[:, None] == (c % half)[None, :]) & (c < ROPE_DIM)[None, :]).astype(np.float32)
    sign = np.where(c < half, -1.0, np.where(c < ROPE_DIM, 1.0, 0.0)).astype(np.float32)
    first = (c < half).astype(np.float32)
    tab = np.stack([sign, first], axis=0)
    g = (lane[:, None] // ATT_DH == lane[None, :] // ATT_DH).astype(np.float32) / ATT_DH
    return freq_rows, sel, tab, g


def _swa_kernel(sink_ref, cur_ref, prev_ref, pcur_ref, pprev_ref, qg_ref, kg_ref, freq_ref, sel_ref, tab_ref, g_ref,
                o_ref):
    Bq = ATT_BLOCK
    n = pl.program_id(1)
    tab = tab_ref[...]
    sign, first = tab[0:1, :], tab[1:2, :]
    gmat = g_ref[...].astype(BF16)
    sel = sel_ref[...].astype(BF16)
    freq_rows = freq_ref[...]
    half = ROPE_DIM // 2
    scale = ATT_DH ** -0.5

    def group_ms(x):
        sq = x * x
        hi = sq.astype(BF16)
        lo = (sq - hi.astype(F32)).astype(BF16)
        return _dot(hi, gmat) + _dot(lo, gmat)

    def spread(a):
        hi = a.astype(BF16)
        r1 = a - hi.astype(F32)
        mid = r1.astype(BF16)
        lo = (r1 - mid.astype(F32)).astype(BF16)
        return _dot_tn(hi, sel) + _dot_tn(mid, sel) + _dot_tn(lo, sel)

    def rope_table(pos_row):
        ang = freq_rows * pos_row
        return 1.0 + spread(jnp.cos(ang) - 1.0), spread(jnp.sin(ang)) * sign

    def norm_rope(x, gain, cs_sn):
        y = x * lax.rsqrt(group_ms(x) + EPS) * gain
        partner = jnp.where(first > 0.5, pltpu.roll(y, V7X_LANES - half, 1), pltpu.roll(y, half, 1))
        return y * cs_sn[0] + partner * cs_sn[1]

    kq = ATT_Q_W
    lane = lax.broadcasted_iota(I32, (Bq, V7X_LANES), 1)
    lo_half = lane < ATT_DH

    def pad_variants(a):
        r = pltpu.roll(a, ATT_DH, 1)
        z = jnp.zeros_like(a)
        return [[jnp.where(lo_half, a, z).astype(BF16), jnp.where(lo_half, z, r).astype(BF16)],
                [jnp.where(lo_half, r, z).astype(BF16), jnp.where(lo_half, z, a).astype(BF16)]]

    tables = [rope_table(pprev_ref[0].astype(F32))]
    kblocks = [pad_variants(norm_rope(prev_ref[:, 0:ATT_KV_W], kg_ref[...], tables[0]))]
    vblocks = [pad_variants(prev_ref[:, ATT_KV_W:2 * ATT_KV_W])]
    for j in range(SWA_QBLOCKS):
        rows = slice(j * Bq, (j + 1) * Bq)
        tables.append(rope_table(pcur_ref[j].astype(F32)))
        kblocks.append(pad_variants(norm_rope(cur_ref[rows, kq:kq + ATT_KV_W], kg_ref[...], tables[j + 1])))
        vblocks.append(pad_variants(cur_ref[rows, kq + ATT_KV_W:kq + 2 * ATT_KV_W]))

    qi = lax.broadcasted_iota(I32, (Bq, 2 * Bq), 0)
    kj = lax.broadcasted_iota(I32, (Bq, 2 * Bq), 1)
    in_band = ((kj < Bq) & (kj > qi)) | ((kj >= Bq) & ((kj - Bq) <= qi))
    first_of_seq = (jnp.zeros((Bq, 2 * Bq), I32) + n) == 0
    for j in range(SWA_QBLOCKS):
        rows = slice(j * Bq, (j + 1) * Bq)
        mask = (in_band & jnp.logical_not(first_of_seq & (kj < Bq))) if j == 0 else in_band
        for t in range(ATT_Q_W // V7X_LANES):
            ls = slice(t * V7X_LANES, (t + 1) * V7X_LANES)
            qt = (norm_rope(cur_ref[rows, ls], qg_ref[...], tables[j + 1]) * scale).astype(BF16)
            acc = jnp.zeros((Bq, V7X_LANES), F32)
            for u in range(2):
                head = 2 * t + u
                kvh = head // ATT_GROUP
                kcat = jnp.concatenate([kblocks[j][kvh][u], kblocks[j + 1][kvh][u]], axis=0)
                vcat = jnp.concatenate([vblocks[j][kvh][u], vblocks[j + 1][kvh][u]], axis=0)
                s = jnp.where(mask, _dot_nt(qt, kcat), NEG_INF)
                sink = sink_ref[head]
                m = jnp.maximum(jnp.max(s, axis=-1, keepdims=True), sink)
                p = jnp.exp(s - m)
                denom = jnp.sum(p, axis=-1, keepdims=True) + jnp.exp(sink - m)
                acc = acc + _dot(p.astype(BF16), vcat) * (1.0 / denom)
            o_ref[rows, ls] = acc.astype(o_ref.dtype)


def _swa(at_in, positions, q_gain, k_gain, sinks, B, S):
    T = at_in.shape[0]
    nb = S // ATT_BLOCK
    qb = SWA_QBLOCKS
    assert nb % qb == 0
    steps = nb // qb
    freq_rows, sel, tab, g = _swa_constants()
    qg = jnp.tile(q_gain.reshape(1, ATT_DH), (1, V7X_LANES // ATT_DH))
    kg = jnp.tile(k_gain.reshape(1, ATT_DH), (1, V7X_LANES // ATT_DH))
    pos3 = positions.reshape(B * nb, 1, ATT_BLOCK)
    n_at = at_in.shape[1]
    kv_blk = 2 * ATT_KV_W
    assert ATT_Q_W % kv_blk == 0
    prev_blk = lambda b, n: b * nb + jnp.maximum(qb * n - 1, 0)
    const = lambda b, n: (0, 0)
    return pl.pallas_call(
        _swa_kernel,
        grid=(B, steps),
        in_specs=[pl.BlockSpec(memory_space=pltpu.SMEM),
                  pl.BlockSpec((qb * ATT_BLOCK, n_at), lambda b, n: (b * steps + n, 0)),
                  pl.BlockSpec((ATT_BLOCK, kv_blk), lambda b, n: (prev_blk(b, n), ATT_Q_W // kv_blk)),
                  pl.BlockSpec((qb, 1, ATT_BLOCK), lambda b, n: (b * steps + n, 0, 0)),
                  pl.BlockSpec((1, 1, ATT_BLOCK), lambda b, n: (prev_blk(b, n), 0, 0)),
                  pl.BlockSpec((1, V7X_LANES), const),
                  pl.BlockSpec((1, V7X_LANES), const),
                  pl.BlockSpec(freq_rows.shape, const),
                  pl.BlockSpec(sel.shape, const),
                  pl.BlockSpec(tab.shape, const),
                  pl.BlockSpec(g.shape, const)],
        out_specs=pl.BlockSpec((qb * ATT_BLOCK, ATT_Q_W), lambda b, n: (b * steps + n, 0)),
        out_shape=jax.ShapeDtypeStruct((T, ATT_Q_W), BF16),
        compiler_params=_cparams(("parallel", "parallel")),
        name="swa",
    )(sinks, at_in, at_in, pos3, pos3, qg, kg, freq_rows, jnp.asarray(sel), jnp.asarray(tab), jnp.asarray(g))


def _merge_router_kernel(x_ref, hg_ref, at_ref, gt_ref, mod_ref, whg_ref, wat_ref, wout_ref, g2_ref,
                         wr_ref, br_ref, tri_ref,
                         x1_ref, h2_ref, idx_ref, gate_ref, rank_ref, cnt_ref, run_ref):
    i = pl.program_id(0)
    D = x_ref.shape[1]
    tm = x_ref.shape[0]

    @pl.when(i == 0)
    def _():
        run_ref[...] = jnp.zeros_like(run_ref)

    y_h = _dot(hg_ref[...], whg_ref[...])
    y_a = _dot(at_ref[...], wat_ref[...])
    merged = _sigmoid(gt_ref[:, 0:D]) * y_h + _sigmoid(gt_ref[:, D:2 * D]) * y_a
    x1 = x_ref[...] + mod_ref[2:3, :] * _dot(merged.astype(BF16), wout_ref[...])
    x1_ref[...] = x1
    ms = jnp.mean(x1 * x1, axis=-1, keepdims=True)
    h2 = x1 * lax.rsqrt(ms + EPS) * g2_ref[...] * (1.0 + mod_ref[4:5, :]) + mod_ref[3:4, :]
    _store_row_tiles(h2_ref, h2)
    logits = _dot(h2.astype(BF16), wr_ref[...]) + br_ref[...]
    E = logits.shape[1]
    lane = lax.broadcasted_iota(I32, (tm, E), 1).astype(F32)
    vals, idxs = [], []
    l = logits
    for _ in range(TOP_K):
        m = jnp.max(l, axis=-1, keepdims=True)
        ik = jnp.min(jnp.where(l == m, lane, float(E)), axis=-1, keepdims=True)
        vals.append(m)
        idxs.append(ik)
        l = jnp.where(lane == ik, NEG_INF, l)
    ex = [jnp.exp(v - vals[0]) for v in vals]
    den = ex[0]
    for e in ex[1:]:
        den = den + e
    onehot = jnp.zeros((tm, E), F32)
    for ik in idxs:
        onehot = onehot + (lane == ik).astype(F32)
    cum = _dot(tri_ref[...], onehot.astype(BF16))
    run = run_ref[0:1, 0:E]
    excl = cum - onehot + run
    lane_k = lax.broadcasted_iota(I32, (tm, TOP_K), 1)
    idx_o = jnp.zeros((tm, TOP_K), I32)
    gate_o = jnp.zeros((tm, TOP_K), F32)
    rank_o = jnp.zeros((tm, TOP_K), I32)
    for kk in range(TOP_K):
        rk = jnp.sum(jnp.where(lane == idxs[kk], excl, 0.0), axis=-1, keepdims=True)
        idx_o = jnp.where(lane_k == kk, idxs[kk].astype(I32), idx_o)
        gate_o = jnp.where(lane_k == kk, ex[kk] / den, gate_o)
        rank_o = jnp.where(lane_k == kk, rk.astype(I32), rank_o)
    idx_ref[...] = idx_o
    gate_ref[...] = gate_o
    rank_ref[...] = rank_o
    new_run = run + cum[tm - 1:tm, :]
    run_ref[0:1, 0:E] = new_run
    cnt_ref[...] = jnp.broadcast_to(new_run, cnt_ref.shape)


def _merge_router(x2, hg_o, at_o, gates, mod3, whg, wat, wout, g2, wr, br, S):
    T, D = x2.shape
    tm = TM_MERGE
    per_b = S // tm
    E = wr.shape[1]
    tri = jnp.asarray(np.tril(np.ones((tm, tm), np.float32)), dtype=BF16)
    row = lambda i: (i, 0)
    const = lambda i: (0, 0)
    return pl.pallas_call(
        _merge_router_kernel,
        grid=(T // tm,),
        in_specs=[pl.BlockSpec((tm, D), row),
                  pl.BlockSpec((tm, HG_W), row),
                  pl.BlockSpec((tm, ATT_Q_W), row),
                  pl.BlockSpec((tm, 2 * D), row),
                  pl.BlockSpec((None, 6, D), lambda i: (i // per_b, 0, 0)),
                  pl.BlockSpec(whg.shape, const),
                  pl.BlockSpec(wat.shape, const),
                  pl.BlockSpec(wout.shape, const),
                  pl.BlockSpec((1, D), const),
                  pl.BlockSpec(wr.shape, const),
                  pl.BlockSpec((1, E), const),
                  pl.BlockSpec((tm, tm), const)],
        out_specs=[pl.BlockSpec((tm, D), row),
                   pl.BlockSpec((tm * ROW_TILE, V7X_LANES), row),
                   pl.BlockSpec((tm, TOP_K), row),
                   pl.BlockSpec((tm, TOP_K), row),
                   pl.BlockSpec((tm, TOP_K), row),
                   pl.BlockSpec((V7X_SUBLANES, E), const)],
        out_shape=[jax.ShapeDtypeStruct((T, D), F32),
                   jax.ShapeDtypeStruct((T * ROW_TILE, V7X_LANES), F32),
                   jax.ShapeDtypeStruct((T, TOP_K), I32),
                   jax.ShapeDtypeStruct((T, TOP_K), F32),
                   jax.ShapeDtypeStruct((T, TOP_K), I32),
                   jax.ShapeDtypeStruct((V7X_SUBLANES, E), F32)],
        scratch_shapes=[pltpu.VMEM((V7X_SUBLANES, V7X_LANES), F32)],
        compiler_params=_cparams(("arbitrary",)),
        name="merge_router",
    )(x2, hg_o, at_o, gates, mod3, whg, wat, wout, g2.reshape(1, D), wr, br.reshape(1, E), tri)


def _dest_kernel(idx_ref, rank_ref, tab_ref, o_ref):
    idx = idx_ref[...]
    tm = idx.shape[0]
    E = tab_ref.shape[0] // TOP_K
    lane = lax.broadcasted_iota(I32, (tm, tab_ref.shape[0]), 1)
    hot = lane == idx[:, 0:1]
    for kk in range(1, TOP_K):
        hot = hot | (lane == idx[:, kk:kk + 1] + kk * E)
    start = _dot_exact01(hot.astype(F32).astype(BF16), tab_ref[...])
    o_ref[...] = rank_ref[...] + start[:, 0:TOP_K].astype(I32)


def _dest(idx, rank, pad_start):
    T = idx.shape[0]
    tm = 1024
    E = pad_start.shape[0]
    assert TOP_K * E == V7X_LANES
    k_of = np.arange(TOP_K * E) // E
    tab = jnp.where(k_of[:, None] == np.arange(V7X_LANES)[None, :], jnp.tile(pad_start, TOP_K)[:, None], 0).astype(F32)
    row = lambda i: (i, 0)
    return pl.pallas_call(
        _dest_kernel,
        grid=(T // tm,),
        in_specs=[pl.BlockSpec((tm, TOP_K), row), pl.BlockSpec((tm, TOP_K), row),
                  pl.BlockSpec(tab.shape, lambda i: (0, 0))],
        out_specs=pl.BlockSpec((tm, TOP_K), row),
        out_shape=jax.ShapeDtypeStruct((T, TOP_K), I32),
        compiler_params=_cparams(("parallel",)),
        name="dest",
    )(idx, rank, tab)


DISPATCH_UNROLL = 4


def _row(ref, r):
    return ref.at[pl.ds(pl.multiple_of(r * ROW_TILE, ROW_TILE), ROW_TILE)]


def _dispatch_kernel(fill_start_ref, fill_n_ref, tail_ref, dest_ref, h_ref, xbuf_ref, zero_ref, sem):
    tm = h_ref.shape[0] // ROW_TILE
    zrows = zero_ref.shape[0]

    def zero_rows_copy(r, n):
        src = zero_ref.at[pl.ds(0, n * ROW_TILE)]
        dst = xbuf_ref.at[pl.ds(pl.multiple_of(r * ROW_TILE, ROW_TILE), n * ROW_TILE)]
        return pltpu.make_async_copy(src, dst, sem.at[1])

    def zero_block_copy(b):
        dst = xbuf_ref.at[pl.ds(pl.multiple_of(b * zrows, zrows), zrows)]
        return pltpu.make_async_copy(zero_ref, dst, sem.at[2])

    @pl.when(pl.program_id(0) == 0)
    def _():
        zero_ref[...] = jnp.zeros_like(zero_ref)

        def per_expert(start):
            def body(e, c):
                n = fill_n_ref[e]
                r = fill_start_ref[e]
                size = zrows // ROW_TILE // 2
                while size >= 1:
                    @pl.when((n & size) != 0)
                    def _(r=r, size=size):
                        cp = zero_rows_copy(r, size)
                        cp.start() if start else cp.wait()
                    r = r + (n & size)
                    size //= 2
                return c
            return body

        def tail(start):
            def body(b, c):
                cp = zero_block_copy(tail_ref[0] + b)
                cp.start() if start else cp.wait()
                return c
            return body

        lax.fori_loop(0, N_EXPERTS, per_expert(True), 0)
        lax.fori_loop(0, tail_ref[1], tail(True), 0)
        lax.fori_loop(0, N_EXPERTS, per_expert(False), 0)
        lax.fori_loop(0, tail_ref[1], tail(False), 0)

    def issue(tb, c):
        for u in range(DISPATCH_UNROLL):
            t = tb * DISPATCH_UNROLL + u
            src = _row(h_ref, t)
            for kk in range(TOP_K):
                d = dest_ref[t * TOP_K + kk]
                pltpu.make_async_copy(src, _row(xbuf_ref, d), sem.at[0]).start(priority=kk % 2)
        return c

    lax.fori_loop(0, tm // DISPATCH_UNROLL, issue, 0)

    def drain(tb, c):
        for _ in range(DISPATCH_UNROLL * TOP_K):
            pltpu.make_async_copy(_row(h_ref, 0), _row(xbuf_ref, 0), sem.at[0]).wait()
        return c

    lax.fori_loop(0, tm // DISPATCH_UNROLL, drain, 0)


def _dispatch(fill_start, fill_n, tail, dest_flat, h2t, P):
    T = h2t.shape[0] // ROW_TILE
    tm = TM_DISPATCH
    grid_spec = pltpu.PrefetchScalarGridSpec(
        num_scalar_prefetch=3,
        grid=(T // tm,),
        in_specs=[pl.BlockSpec((tm * TOP_K,), lambda i, *_: (i,), memory_space=pltpu.SMEM),
                  pl.BlockSpec((tm * ROW_TILE, V7X_LANES), lambda i, *_: (i, 0))],
        out_specs=pl.BlockSpec(memory_space=pl.ANY),
        scratch_shapes=[pltpu.VMEM((MOE_BM * ROW_TILE, V7X_LANES), F32), pltpu.SemaphoreType.DMA((3,))],
    )
    return pl.pallas_call(
        _dispatch_kernel,
        grid_spec=grid_spec,
        out_shape=jax.ShapeDtypeStruct((P * ROW_TILE, V7X_LANES), F32),
        compiler_params=_cparams(("arbitrary",)),
        name="dispatch",
    )(fill_start, fill_n, tail, dest_flat, h2t)


FFN_PREP_COLS = 256
FFN_UNITS_PER_STEP = 1


def _ffn_kernel(cur_ref, src_ref, slot_ref, pos_ref, last_ref, used_ref,
                x_ref, wu_ref, wd_ref, bg_ref, bl_ref, bd_ref, perm_ref, y_ref,
                wg0, wl0, wd0, wg1, wl1, wd1):
    del cur_ref, src_ref
    s = pl.program_id(0)
    bm = x_ref.shape[0] // ROW_TILE
    D, De2 = wu_ref.shape
    De = De2 // 2
    half = FFN_PREP_COLS // 2
    n_units = De2 // FFN_PREP_COLS
    drows = De // n_units
    slot = slot_ref[s]
    used = used_ref[s] == 1
    stages = ((wg0, wl0, wd0), (wg1, wl1, wd1))
    perm = perm_ref[...]

    def stage_unit(u, dst):
        wg, wl, wdb = dst
        c0 = pl.multiple_of(u * FFN_PREP_COLS, FFN_PREP_COLS)
        p = _dot(wu_ref[:, pl.ds(c0, FFN_PREP_COLS)].astype(BF16), perm)
        r0 = pl.multiple_of(u * half, half)
        wg[:, pl.ds(r0, half)] = p[:, 0:half].astype(BF16)
        wl[:, pl.ds(r0, half)] = p[:, half:2 * half].astype(BF16)
        d0 = pl.multiple_of(u * drows, drows)
        wdb[pl.ds(d0, drows), :] = wd_ref[pl.ds(d0, drows), :].astype(BF16)

    @pl.when(s == 0)
    def _():
        for u in range(n_units):
            stage_unit(u, stages[0])

    done = (pos_ref[s] + 1) * FFN_UNITS_PER_STEP

    def block(cur, nxt):
        @pl.when(used)
        def _():
            u0 = jnp.minimum(pos_ref[s] * FFN_UNITS_PER_STEP, n_units - FFN_UNITS_PER_STEP)
            for j in range(FFN_UNITS_PER_STEP):
                stage_unit(u0 + j, nxt)
            wg, wl, wdb = cur
            x = _load_row_tiles(x_ref, bm).astype(BF16)
            glu = _dot(x, wg[...]) + bg_ref[...]
            lin = _dot(x, wl[...]) + bl_ref[...]
            glu = jnp.minimum(glu, SWIGLU_LIMIT)
            lin = jnp.clip(lin, -SWIGLU_LIMIT, SWIGLU_LIMIT)
            act = glu * _sigmoid(SWIGLU_ALPHA * glu) * (lin + 1.0)
            _store_row_tiles(y_ref, _dot(act.astype(BF16), wdb[...]) + bd_ref[...])

        @pl.when(used & (last_ref[s] == 1) & (done < n_units))
        def _():
            def body(u, c):
                stage_unit(u, nxt)
                return c
            lax.fori_loop(done, n_units, body, 0)

    @pl.when(slot == 0)
    def _():
        block(stages[0], stages[1])

    @pl.when(slot == 1)
    def _():
        block(stages[1], stages[0])

    @pl.when(jnp.logical_not(used) & (s > 0))
    def _():
        y_ref[...] = jnp.zeros_like(y_ref)


def _ffn_schedule(block_expert, n_used):
    n = block_expert.shape[0]
    idx = jnp.arange(n, dtype=I32)
    be = block_expert
    first = jnp.concatenate([jnp.ones((1,), bool), be[1:] != be[:-1]])
    run_start = lax.cummax(jnp.where(first, idx, 0))
    ordinal = jnp.cumsum(first.astype(I32)) - 1
    is_last = jnp.concatenate([first[1:], jnp.ones((1,), bool)])
    next_first = lax.cummin(jnp.where(first, idx, n), reverse=True)
    next_start = jnp.concatenate([next_first[1:], jnp.full((1,), n, I32)])
    next_e = be[jnp.minimum(next_start, n - 1)]
    blk = jnp.maximum(jnp.arange(n + 1, dtype=I32) - 1, 0)
    step = jnp.arange(n + 1, dtype=I32)
    cur = be[blk]
    src = jnp.where(step == 0, be[0], next_e[blk])
    slot = ordinal[blk] % 2
    pos = blk - run_start[blk]
    last = is_last[blk].astype(I32)
    used = ((step >= 1) & (blk < n_used[0])).astype(I32)
    return [a.astype(I32) for a in (cur, src, slot, pos, last, used)]


def _ffn(block_expert, n_used, xbuf, w_up, bg, bl, w_down, bd):
    P = xbuf.shape[0] // ROW_TILE
    bm = MOE_BM
    _, D, De2 = w_up.shape
    De = De2 // 2
    assert (De2 // FFN_PREP_COLS) % FFN_UNITS_PER_STEP == 0
    sched = _ffn_schedule(block_expert, n_used)
    cc = np.arange(FFN_PREP_COLS)
    perm = (cc[:, None] == np.where(cc < FFN_PREP_COLS // 2, 2 * cc, 2 * (cc - FFN_PREP_COLS // 2) + 1)[None, :])
    rows = lambda s, *_: (jnp.maximum(s - 1, 0), 0)
    wmap = lambda s, cur, src, *_: (src[s], 0, 0)
    bmap = lambda s, cur, *_: (cur[s], 0, 0)
    grid_spec = pltpu.PrefetchScalarGridSpec(
        num_scalar_prefetch=len(sched),
        grid=(P // bm + 1,),
        in_specs=[pl.BlockSpec((bm * ROW_TILE, V7X_LANES), rows),
                  pl.BlockSpec((None, D, De2), wmap),
                  pl.BlockSpec((None, De, D), wmap),
                  pl.BlockSpec((None, 1, De), bmap),
                  pl.BlockSpec((None, 1, De), bmap),
                  pl.BlockSpec((None, 1, D), bmap),
                  pl.BlockSpec(perm.shape, lambda s, *_: (0, 0))],
        out_specs=pl.BlockSpec((bm * ROW_TILE, V7X_LANES), rows),
        scratch_shapes=[pltpu.VMEM((D, De), BF16), pltpu.VMEM((D, De), BF16), pltpu.VMEM((De, D), BF16)] * 2,
    )
    return pl.pallas_call(
        _ffn_kernel,
        grid_spec=grid_spec,
        out_shape=jax.ShapeDtypeStruct((P * ROW_TILE, V7X_LANES), F32),
        compiler_params=_cparams(("arbitrary",)),
        name="expert_ffn",
    )(*sched, xbuf, w_up, w_down, bg, bl, bd, jnp.asarray(perm, dtype=BF16))


def _combine_kernel(dcur_ref, dnext_ref, gate_ref, x1_ref, mod_ref, y_hbm, o_ref, buf, sem):
    i = pl.program_id(0)
    n = pl.num_programs(0)
    tm = x1_ref.shape[0]
    slot = i % 2

    def issue(dref, s):
        def body(tb, c):
            for u in range(DISPATCH_UNROLL):
                t = tb * DISPATCH_UNROLL + u
                for kk in range(TOP_K):
                    d = dref[t * TOP_K + kk]
                    pltpu.make_async_copy(_row(y_hbm, d), _row(buf.at[s, kk], t), sem.at[s]).start(priority=kk % 2)
            return c
        lax.fori_loop(0, tm // DISPATCH_UNROLL, body, 0)

    @pl.when(i == 0)
    def _():
        issue(dcur_ref, 0)

    @pl.when(i + 1 < n)
    def _():
        issue(dnext_ref, 1 - slot)

    def drain(tb, c):
        for _ in range(DISPATCH_UNROLL * TOP_K):
            pltpu.make_async_copy(_row(y_hbm, 0), _row(buf.at[slot, 0], 0), sem.at[slot]).wait()
        return c

    lax.fori_loop(0, tm // DISPATCH_UNROLL, drain, 0)
    gate = gate_ref[...]
    acc = gate[:, 0:1] * _load_row_tiles(buf.at[slot, 0], tm)
    for kk in range(1, TOP_K):
        acc = acc + gate[:, kk:kk + 1] * _load_row_tiles(buf.at[slot, kk], tm)
    o_ref[...] = x1_ref[...] + mod_ref[5:6, :] * acc


def _combine(dest_flat, gate, x1, mod3, ybuf, S):
    T, D = x1.shape
    tm = TM_COMBINE
    per_b = S // tm
    nt = T // tm
    return pl.pallas_call(
        _combine_kernel,
        grid=(nt,),
        in_specs=[pl.BlockSpec((tm * TOP_K,), lambda i: (i,), memory_space=pltpu.SMEM),
                  pl.BlockSpec((tm * TOP_K,), lambda i: (jnp.minimum(i + 1, nt - 1),), memory_space=pltpu.SMEM),
                  pl.BlockSpec((tm, TOP_K), lambda i: (i, 0)),
                  pl.BlockSpec((tm, D), lambda i: (i, 0)),
                  pl.BlockSpec((None, 6, D), lambda i: (i // per_b, 0, 0)),
                  pl.BlockSpec(memory_space=pl.ANY)],
        out_specs=pl.BlockSpec((tm, D), lambda i: (i, 0)),
        out_shape=jax.ShapeDtypeStruct((T, D), F32),
        scratch_shapes=[pltpu.VMEM((2, TOP_K, tm * ROW_TILE, V7X_LANES), F32), pltpu.SemaphoreType.DMA((2,))],
        compiler_params=_cparams(("arbitrary",)),
        name="combine",
    )(dest_flat, dest_flat, gate, x1, mod3, ybuf)


def kernel(x, c, positions, w_ada, b_ada, norm1_gain, w_in, lower_bound_logits, hg_norm_gain, w_hg_branch,
           q_norm_gain, k_norm_gain, attn_sinks, w_attn_branch, w_out, norm2_gain, w_router, b_router,
           w_up, b_up, w_down, b_down):
    B, S, D = x.shape
    T = B * S
    assert w_ada.shape[0] == 1, "one layer"
    x2 = x.reshape(T, D)

    mod = _ada(c, w_ada[0], b_ada[0])
    mod3 = mod.reshape(B, 6, D)

    hg_in, at_in, gates, f_min = _inproj(x2, mod3, norm1_gain[0], w_in[0].astype(BF16), S)
    f_min = f_min[::V7X_SUBLANES, 0].reshape(B, S // TM_PROJ)
    hg_o = _hgrn(hg_in, f_min, lower_bound_logits, hg_norm_gain[0], B, S)
    at_o = _swa(at_in, positions, q_norm_gain[0], k_norm_gain[0], attn_sinks[0], B, S)

    x1, h2, idx, gate, rank, cnt = _merge_router(
        x2, hg_o, at_o, gates, mod3, w_hg_branch[0].astype(BF16), w_attn_branch[0].astype(BF16),
        w_out[0].astype(BF16), norm2_gain[0], w_router[0].astype(BF16), b_router[0], S)

    bm = MOE_BM
    counts = cnt[0].astype(I32)
    padded = (counts + bm - 1) // bm * bm
    pad_end = jnp.cumsum(padded)
    pad_start = pad_end - padded
    P = T * TOP_K + N_EXPERTS * bm
    n_blocks = P // bm
    block_start = jnp.arange(n_blocks, dtype=I32) * bm
    block_expert = jnp.minimum(
        jnp.sum((pad_end[None, :] <= block_start[:, None]).astype(I32), axis=1), N_EXPERTS - 1).astype(I32)
    n_used = (pad_end[-1:] // bm).astype(I32)
    tail = jnp.concatenate([n_used, n_blocks - n_used]).astype(I32)

    dest = _dest(idx, rank, pad_start)
    dest_flat = dest.reshape(T * TOP_K)
    xbuf = _dispatch((pad_start + counts).astype(I32), (padded - counts).astype(I32), tail, dest_flat, h2, P)

    ybuf = _ffn(block_expert, n_used, xbuf, w_up[0],
                b_up[0][:, None, 0::2], b_up[0][:, None, 1::2],
                w_down[0], b_down[0][:, None, :])

    out = _combine(dest_flat, gate, x1, mod3, ybuf, S)
    return out.reshape(B, S, D)
```

```python
import functools

import numpy as np
import jax
import jax.numpy as jnp
from jax import lax
from jax.experimental import pallas as pl
from jax.experimental.pallas import tpu as pltpu

F32 = jnp.float32
BF16 = jnp.bfloat16
I32 = jnp.int32

HG_HEADS = 4
HG_DIM = 128
HG_W = HG_HEADS * HG_DIM
ATT_Q_HEADS = 8
ATT_KV_HEADS = 2
ATT_GROUP = ATT_Q_HEADS // ATT_KV_HEADS
ATT_DH = 64
ATT_Q_W = ATT_Q_HEADS * ATT_DH
ATT_KV_W = ATT_KV_HEADS * ATT_DH
ATT_BLOCK = 128
ROPE_THETA = 500000.0
ROPE_DIM = ATT_DH // 4
N_EXPERTS = 32
TOP_K = 4
SWIGLU_ALPHA = 1.702
SWIGLU_LIMIT = 7.0
EPS = 1e-6

V7X_LANES = 128
V7X_SUBLANES = 8
V7X_VMEM_LIMIT_BYTES = 56 * 1024 * 1024

TM_PROJ = 256
TM_MERGE = 512
HG_TILE = 512
HG_CHUNK = 128
HG_SUB = 8
MOE_BM = 512
TM_DISPATCH = 256
TM_COMBINE = 256

NEG_INF = float("-inf")


def _cparams(sem, vmem=V7X_VMEM_LIMIT_BYTES):
    return pltpu.CompilerParams(dimension_semantics=sem, vmem_limit_bytes=vmem)


def _sigmoid(x):
    return 1.0 / (1.0 + jnp.exp(-x))


def _dot(a, b):
    return jnp.dot(a, b, preferred_element_type=F32)


def _dot_nt(a, b):
    return lax.dot_general(a, b, (((1,), (1,)), ((), ())), preferred_element_type=F32)


def _dot_tn(a, b):
    return lax.dot_general(a, b, (((0,), (0,)), ((), ())), preferred_element_type=F32)


ROW_TILE = V7X_SUBLANES


def _store_row_tiles(ref, val):
    rows = val.shape[0]
    for g in range(ROW_TILE):
        ref[pl.ds(g, rows, stride=ROW_TILE), :] = val[:, g * V7X_LANES:(g + 1) * V7X_LANES]


def _load_row_tiles(ref, rows):
    return jnp.concatenate([ref[pl.ds(g, rows, stride=ROW_TILE), :] for g in range(ROW_TILE)], axis=1)


def _dot_exact01(m01, x):
    hi = x.astype(BF16)
    r1 = x - hi.astype(F32)
    mid = r1.astype(BF16)
    lo = (r1 - mid.astype(F32)).astype(BF16)
    return _dot(m01, hi) + _dot(m01, mid) + _dot(m01, lo)


def _ada_kernel(c_ref, w_ref, b_ref, o_ref):
    c = c_ref[...]
    cond = c * _sigmoid(c)
    o_ref[...] = _dot(cond.astype(BF16), w_ref[...].astype(BF16)) + b_ref[...]


def _ada(c, w_ada, b_ada):
    B, D = c.shape
    N = w_ada.shape[1]
    tn = D
    return pl.pallas_call(
        _ada_kernel,
        grid=(N // tn,),
        in_specs=[pl.BlockSpec((B, D), lambda j: (0, 0)),
                  pl.BlockSpec((D, tn), lambda j: (0, j)),
                  pl.BlockSpec((1, tn), lambda j: (0, j))],
        out_specs=pl.BlockSpec((B, tn), lambda j: (0, j)),
        out_shape=jax.ShapeDtypeStruct((B, N), F32),
        compiler_params=_cparams(("parallel",)),
        name="ada",
    )(c, w_ada, b_ada.reshape(1, N))


def _inproj_kernel(x_ref, mod_ref, g_ref, w_ref, hg_ref, at_ref, gt_ref, fmin_ref):
    x = x_ref[...]
    ms = jnp.mean(x * x, axis=-1, keepdims=True)
    y = x * lax.rsqrt(ms + EPS) * g_ref[...]
    h = (y * (1.0 + mod_ref[1:2, :]) + mod_ref[0:1, :]).astype(BF16)
    n_hg = hg_ref.shape[1]
    n_at = at_ref.shape[1]
    n_gt = gt_ref.shape[1]
    hg = _dot(h, w_ref[:, 0:n_hg])
    hg_ref[...] = hg
    f_pre = hg[:, HG_W:2 * HG_W]
    f_min = jnp.min(jnp.min(f_pre, axis=-1, keepdims=True), axis=0, keepdims=True)
    fmin_ref[...] = jnp.broadcast_to(f_min, fmin_ref.shape)
    at_ref[...] = _dot(h, w_ref[:, n_hg:n_hg + n_at])
    gt_ref[...] = _dot(h, w_ref[:, n_hg + n_at:n_hg + n_at + n_gt])


def _inproj(x2, mod3, gain, w_in_bf, S):
    T, D = x2.shape
    tm = TM_PROJ
    n_hg = 4 * HG_W
    n_at = ATT_Q_W + 2 * ATT_KV_W
    n_gt = 2 * D
    assert w_in_bf.shape[1] == n_hg + n_at + n_gt
    per_b = S // tm
    return pl.pallas_call(
        _inproj_kernel,
        grid=(T // tm,),
        in_specs=[pl.BlockSpec((tm, D), lambda i: (i, 0)),
                  pl.BlockSpec((None, 6, D), lambda i: (i // per_b, 0, 0)),
                  pl.BlockSpec((1, D), lambda i: (0, 0)),
                  pl.BlockSpec(w_in_bf.shape, lambda i: (0, 0))],
        out_specs=[pl.BlockSpec((tm, n_hg), lambda i: (i, 0)),
                   pl.BlockSpec((tm, n_at), lambda i: (i, 0)),
                   pl.BlockSpec((tm, n_gt), lambda i: (i, 0)),
                   pl.BlockSpec((V7X_SUBLANES, V7X_LANES), lambda i: (i, 0))],
        out_shape=[jax.ShapeDtypeStruct((T, n_hg), F32),
                   jax.ShapeDtypeStruct((T, n_at), F32),
                   jax.ShapeDtypeStruct((T, n_gt), F32),
                   jax.ShapeDtypeStruct((T // tm * V7X_SUBLANES, V7X_LANES), F32)],
        compiler_params=_cparams(("parallel",)),
        name="inproj",
    )(x2, mod3, gain.reshape(1, D), w_in_bf)


def _hgrn_constants():
    C = HG_CHUNK
    tri = (np.arange(C)[None, :] <= np.arange(C)[:, None]).astype(np.float32)
    halves = []
    lh = C // 2
    while lh >= HG_SUB:
        halves.append(lh)
        lh //= 2
    code = np.zeros((C, C), np.int32)
    ii, jj = np.meshgrid(np.arange(C), np.arange(C), indexing="ij")
    for li, lh in enumerate(halves):
        sel = ((ii // (2 * lh)) == (jj // (2 * lh))) & ((ii % (2 * lh)) >= lh) & ((jj % (2 * lh)) < lh)
        code[sel] = li + 1
    return tri, code, halves


HG_MILD_FMIN = -7.0


def _hgrn_kernel(mild_ref, in_ref, lbl_ref, gain_ref, tri_ref, code_ref, wsum_ref, o_ref, st_ref, *, halves):
    C = HG_CHUNK
    W = HG_W
    n_chunks = in_ref.shape[1] // C
    step_is_mild = mild_ref[pl.program_id(0) * pl.num_programs(1) + pl.program_id(1)] == 1

    @pl.when(pl.program_id(1) == 0)
    def _():
        st_ref[...] = jnp.zeros_like(st_ref)

    l0 = lbl_ref[0:1, :]
    l1 = lbl_ref[1:2, :]
    mx = jnp.maximum(l0, l1)
    e0 = jnp.exp(l0 - mx)
    lb = e0 / (e0 + jnp.exp(l1 - mx))
    gain = gain_ref[...]
    tri = tri_ref[...].astype(BF16)
    code = code_ref[...]
    wsum = wsum_ref[...]
    row = lax.broadcasted_iota(I32, (C, C), 0)
    col = lax.broadcasted_iota(I32, (C, C), 1)
    row_in_sub = lax.broadcasted_iota(I32, (C, HG_DIM), 0) % HG_SUB
    same_sub = (row // HG_SUB) == (col // HG_SUB)
    sub_causal = same_sub & (col <= row)

    def group_row(a, group, r):
        a3 = a.reshape(C // group, group, a.shape[1])
        return jnp.broadcast_to(a3[:, r:r + 1, :], a3.shape).reshape(a.shape)

    def bcast_sub(a, j):
        return group_row(a, HG_SUB, j)

    def chunk(mild, c, carry):
        r0 = pl.multiple_of(c * C, C)
        rows = pl.ds(r0, C)
        for bb in range(in_ref.shape[0]):
            qp = in_ref[bb, rows, 0:W]
            fp = in_ref[bb, rows, W:2 * W]
            vv = in_ref[bb, rows, 2 * W:3 * W]
            gp = in_ref[bb, rows, 3 * W:4 * W]
            forget = lb + (1.0 - lb) * _sigmoid(fp)
            q = qp * _sigmoid(qp)
            k = 1.0 - forget
            lf = jnp.log(forget)
            b = _dot_exact01(tri, lf)
            e_b = jnp.exp(b)
            e_st = jnp.exp(group_row(b, C, C - 1) - b)
            e_lv = [jnp.exp(-jnp.abs(b - group_row(b, 2 * lh, lh - 1))) for lh in halves]
            if mild:
                x_sub = group_row(b, HG_SUB, HG_SUB - 1) - b
                e_subk = jnp.exp(x_sub)
                e_subq = jnp.exp(-x_sub)
            outs = []
            for h in range(HG_HEADS):
                ls = slice(h * HG_DIM, (h + 1) * HG_DIM)
                qh, kh, vh = q[:, ls], k[:, ls], vv[:, ls]
                bh = b[:, ls]
                st = st_ref[bb, h]
                o = _dot_nt((qh * e_b[:, ls]).astype(BF16), st.astype(BF16))
                if mild:
                    s = _dot_nt((qh * e_subq[:, ls]).astype(BF16), (kh * e_subk[:, ls]).astype(BF16))
                    scores = jnp.where(sub_causal, s, 0.0)
                else:
                    ps = []
                    for j in range(HG_SUB):
                        d = jnp.where(row_in_sub >= j, bh - bcast_sub(bh, j), NEG_INF)
                        ps.append((qh * bcast_sub(kh, j) * jnp.exp(d)).astype(BF16))
                    scores = jnp.where(same_sub, _dot(jnp.concatenate(ps, axis=1), wsum), 0.0)
                for li in range(len(halves)):
                    e = e_lv[li][:, ls]
                    s = _dot_nt((qh * e).astype(BF16), (kh * e).astype(BF16))
                    scores = jnp.where(code == li + 1, s, scores)
                o = o + _dot(scores.astype(BF16), vh.astype(BF16))
                kst = (kh * e_st[:, ls]).astype(BF16)
                st_ref[bb, h] = st * e_b[C - 1:C, ls] + _dot_tn(vh.astype(BF16), kst)
                ms = jnp.mean(o * o, axis=-1, keepdims=True)
                on = o * lax.rsqrt(ms + EPS) * gain
                gh = gp[:, ls]
                outs.append(on * (gh * _sigmoid(gh)))
            o_ref[bb, rows, :] = jnp.concatenate(outs, axis=-1).astype(o_ref.dtype)
        return carry

    @pl.when(step_is_mild)
    def _():
        lax.fori_loop(0, n_chunks, functools.partial(chunk, True), 0)

    @pl.when(jnp.logical_not(step_is_mild))
    def _():
        lax.fori_loop(0, n_chunks, functools.partial(chunk, False), 0)


HG_SEQS = 4


def _hgrn(hg_in, f_min, lb_logits, gain, B, S):
    T = hg_in.shape[0]
    lt = HG_TILE
    tri, code, halves = _hgrn_constants()
    nseq = HG_SEQS if B % HG_SEQS == 0 else 1
    step_min = jnp.min(f_min.reshape(B // nseq, nseq, S // lt, lt // TM_PROJ), axis=(1, 3))
    mild = (step_min >= HG_MILD_FMIN).astype(I32).reshape(-1)
    wsum = (np.arange(HG_SUB * HG_DIM)[:, None] // HG_DIM == np.arange(HG_CHUNK)[None, :] % HG_SUB).astype(np.float32)
    const = lambda b, s, m: (0, 0)
    grid_spec = pltpu.PrefetchScalarGridSpec(
        num_scalar_prefetch=1,
        grid=(B // nseq, S // lt),
        in_specs=[pl.BlockSpec((nseq, lt, 4 * HG_W), lambda b, s, m: (b, s, 0)),
                  pl.BlockSpec(lb_logits.shape, const),
                  pl.BlockSpec((1, HG_DIM), const),
                  pl.BlockSpec(tri.shape, const),
                  pl.BlockSpec(code.shape, const),
                  pl.BlockSpec(wsum.shape, const)],
        out_specs=pl.BlockSpec((nseq, lt, HG_W), lambda b, s, m: (b, s, 0)),
        scratch_shapes=[pltpu.VMEM((nseq, HG_HEADS, HG_DIM, HG_DIM), F32)],
    )
    out = pl.pallas_call(
        functools.partial(_hgrn_kernel, halves=tuple(halves)),
        grid_spec=grid_spec,
        out_shape=jax.ShapeDtypeStruct((B, S, HG_W), BF16),
        compiler_params=_cparams(("parallel", "arbitrary")),
        name="hgrn2",
    )(mild, hg_in.reshape(B, S, 4 * HG_W), lb_logits, gain.reshape(1, HG_DIM), jnp.asarray(tri),
      jnp.asarray(code), jnp.asarray(wsum, dtype=BF16))
    return out.reshape(T, HG_W)


SWA_QBLOCKS = 8
ROPE_ROWS = 16


def _swa_constants():
    lane = np.arange(V7X_LANES)
    c = lane % ATT_DH
    half = ROPE_DIM // 2
    inv_freq = ROPE_THETA ** (-jnp.arange(half, dtype=F32) / half)
    freq_rows = jnp.broadcast_to(
        jnp.concatenate([inv_freq, jnp.zeros((ROPE_ROWS - half,), F32)])[:, None], (ROPE_ROWS, V7X_LANES))
    sel = ((np.arange(ROPE_ROWS)[:, None] == (c % half)[None, :]) & (c < ROPE_DIM)[None, :]).astype(np.float32)
    sign = np.where(c < half, -1.0, np.where(c < ROPE_DIM, 1.0, 0.0)).astype(np.float32)
    first = (c < half).astype(np.float32)
    tab = np.stack([sign, first], axis=0)
    g = (lane[:, None] // ATT_DH == lane[None, :] // ATT_DH).astype(np.float32) / ATT_DH
    return freq_rows, sel, tab, g


def _swa_kernel(sink_ref, cur_ref, prev_ref, pcur_ref, pprev_ref, qg_ref, kg_ref, freq_ref, sel_ref, tab_ref, g_ref,
                o_ref):
    Bq = ATT_BLOCK
    n = pl.program_id(1)
    tab = tab_ref[...]
    sign, first = tab[0:1, :], tab[1:2, :]
    gmat = g_ref[...].astype(BF16)
    sel = sel_ref[...].astype(BF16)
    freq_rows = freq_ref[...]
    half = ROPE_DIM // 2
    scale = ATT_DH ** -0.5

    def group_ms(x):
        sq = x * x
        hi = sq.astype(BF16)
        lo = (sq - hi.astype(F32)).astype(BF16)
        return _dot(hi, gmat) + _dot(lo, gmat)

    def spread(a):
        hi = a.astype(BF16)
        r1 = a - hi.astype(F32)
        mid = r1.astype(BF16)
        lo = (r1 - mid.astype(F32)).astype(BF16)
        return _dot_tn(hi, sel) + _dot_tn(mid, sel) + _dot_tn(lo, sel)

    def rope_table(pos_row):
        ang = freq_rows * pos_row
        return 1.0 + spread(jnp.cos(ang) - 1.0), spread(jnp.sin(ang)) * sign

    def norm_rope(x, gain, cs_sn):
        y = x * lax.rsqrt(group_ms(x) + EPS) * gain
        partner = jnp.where(first > 0.5, pltpu.roll(y, V7X_LANES - half, 1), pltpu.roll(y, half, 1))
        return y * cs_sn[0] + partner * cs_sn[1]

    kq = ATT_Q_W
    lane = lax.broadcasted_iota(I32, (Bq, V7X_LANES), 1)
    lo_half = lane < ATT_DH

    def pad_variants(a):
        r = pltpu.roll(a, ATT_DH, 1)
        z = jnp.zeros_like(a)
        return [[jnp.where(lo_half, a, z).astype(BF16), jnp.where(lo_half, z, r).astype(BF16)],
                [jnp.where(lo_half, r, z).astype(BF16), jnp.where(lo_half, z, a).astype(BF16)]]

    tables = [rope_table(pprev_ref[0].astype(F32))]
    kblocks = [pad_variants(norm_rope(prev_ref[:, 0:ATT_KV_W], kg_ref[...], tables[0]))]
    vblocks = [pad_variants(prev_ref[:, ATT_KV_W:2 * ATT_KV_W])]
    for j in range(SWA_QBLOCKS):
        rows = slice(j * Bq, (j + 1) * Bq)
        tables.append(rope_table(pcur_ref[j].astype(F32)))
        kblocks.append(pad_variants(norm_rope(cur_ref[rows, kq:kq + ATT_KV_W], kg_ref[...], tables[j + 1])))
        vblocks.append(pad_variants(cur_ref[rows, kq + ATT_KV_W:kq + 2 * ATT_KV_W]))

    qi = lax.broadcasted_iota(I32, (Bq, 2 * Bq), 0)
    kj = lax.broadcasted_iota(I32, (Bq, 2 * Bq), 1)
    in_band = ((kj < Bq) & (kj > qi)) | ((kj >= Bq) & ((kj - Bq) <= qi))
    first_of_seq = (jnp.zeros((Bq, 2 * Bq), I32) + n) == 0
    for j in range(SWA_QBLOCKS):
        rows = slice(j * Bq, (j + 1) * Bq)
        mask = (in_band & jnp.logical_not(first_of_seq & (kj < Bq))) if j == 0 else in_band
        for t in range(ATT_Q_W // V7X_LANES):
            ls = slice(t * V7X_LANES, (t + 1) * V7X_LANES)
            qt = (norm_rope(cur_ref[rows, ls], qg_ref[...], tables[j + 1]) * scale).astype(BF16)
            acc = jnp.zeros((Bq, V7X_LANES), F32)
            for u in range(2):
                head = 2 * t + u
                kvh = head // ATT_GROUP
                kcat = jnp.concatenate([kblocks[j][kvh][u], kblocks[j + 1][kvh][u]], axis=0)
                vcat = jnp.concatenate([vblocks[j][kvh][u], vblocks[j + 1][kvh][u]], axis=0)
                s = jnp.where(mask, _dot_nt(qt, kcat), NEG_INF)
                sink = sink_ref[head]
                m = jnp.maximum(jnp.max(s, axis=-1, keepdims=True), sink)
                p = jnp.exp(s - m)
                denom = jnp.sum(p, axis=-1, keepdims=True) + jnp.exp(sink - m)
                acc = acc + _dot(p.astype(BF16), vcat) * (1.0 / denom)
            o_ref[rows, ls] = acc.astype(o_ref.dtype)


def _swa(at_in, positions, q_gain, k_gain, sinks, B, S):
    T = at_in.shape[0]
    nb = S // ATT_BLOCK
    qb = SWA_QBLOCKS
    assert nb % qb == 0
    steps = nb // qb
    freq_rows, sel, tab, g = _swa_constants()
    qg = jnp.tile(q_gain.reshape(1, ATT_DH), (1, V7X_LANES // ATT_DH))
    kg = jnp.tile(k_gain.reshape(1, ATT_DH), (1, V7X_LANES // ATT_DH))
    pos3 = positions.reshape(B * nb, 1, ATT_BLOCK)
    n_at = at_in.shape[1]
    kv_blk = 2 * ATT_KV_W
    assert ATT_Q_W % kv_blk == 0
    prev_blk = lambda b, n: b * nb + jnp.maximum(qb * n - 1, 0)
    const = lambda b, n: (0, 0)
    return pl.pallas_call(
        _swa_kernel,
        grid=(B, steps),
        in_specs=[pl.BlockSpec(memory_space=pltpu.SMEM),
                  pl.BlockSpec((qb * ATT_BLOCK, n_at), lambda b, n: (b * steps + n, 0)),
                  pl.BlockSpec((ATT_BLOCK, kv_blk), lambda b, n: (prev_blk(b, n), ATT_Q_W // kv_blk)),
                  pl.BlockSpec((qb, 1, ATT_BLOCK), lambda b, n: (b * steps + n, 0, 0)),
                  pl.BlockSpec((1, 1, ATT_BLOCK), lambda b, n: (prev_blk(b, n), 0, 0)),
                  pl.BlockSpec((1, V7X_LANES), const),
                  pl.BlockSpec((1, V7X_LANES), const),
                  pl.BlockSpec(freq_rows.shape, const),
                  pl.BlockSpec(sel.shape, const),
                  pl.BlockSpec(tab.shape, const),
                  pl.BlockSpec(g.shape, const)],
        out_specs=pl.BlockSpec((qb * ATT_BLOCK, ATT_Q_W), lambda b, n: (b * steps + n, 0)),
        out_shape=jax.ShapeDtypeStruct((T, ATT_Q_W), BF16),
        compiler_params=_cparams(("parallel", "parallel")),
        name="swa",
    )(sinks, at_in, at_in, pos3, pos3, qg, kg, freq_rows, jnp.asarray(sel), jnp.asarray(tab), jnp.asarray(g))


def _merge_router_kernel(x_ref, hg_ref, at_ref, gt_ref, mod_ref, whg_ref, wat_ref, wout_ref, g2_ref,
                         wr_ref, br_ref, tri_ref,
                         x1_ref, h2_ref, idx_ref, gate_ref, rank_ref, cnt_ref, run_ref):
    i = pl.program_id(0)
    D = x_ref.shape[1]
    tm = x_ref.shape[0]

    @pl.when(i == 0)
    def _():
        run_ref[...] = jnp.zeros_like(run_ref)

    y_h = _dot(hg_ref[...], whg_ref[...])
    y_a = _dot(at_ref[...], wat_ref[...])
    merged = _sigmoid(gt_ref[:, 0:D]) * y_h + _sigmoid(gt_ref[:, D:2 * D]) * y_a
    x1 = x_ref[...] + mod_ref[2:3, :] * _dot(merged.astype(BF16), wout_ref[...])
    x1_ref[...] = x1
    ms = jnp.mean(x1 * x1, axis=-1, keepdims=True)
    h2 = x1 * lax.rsqrt(ms + EPS) * g2_ref[...] * (1.0 + mod_ref[4:5, :]) + mod_ref[3:4, :]
    _store_row_tiles(h2_ref, h2)
    logits = _dot(h2.astype(BF16), wr_ref[...]) + br_ref[...]
    E = logits.shape[1]
    lane = lax.broadcasted_iota(I32, (tm, E), 1).astype(F32)
    vals, idxs = [], []
    l = logits
    for _ in range(TOP_K):
        m = jnp.max(l, axis=-1, keepdims=True)
        ik = jnp.min(jnp.where(l == m, lane, float(E)), axis=-1, keepdims=True)
        vals.append(m)
        idxs.append(ik)
        l = jnp.where(lane == ik, NEG_INF, l)
    ex = [jnp.exp(v - vals[0]) for v in vals]
    den = ex[0]
    for e in ex[1:]:
        den = den + e
    onehot = jnp.zeros((tm, E), F32)
    for ik in idxs:
        onehot = onehot + (lane == ik).astype(F32)
    cum = _dot(tri_ref[...], onehot.astype(BF16))
    run = run_ref[0:1, 0:E]
    excl = cum - onehot + run
    lane_k = lax.broadcasted_iota(I32, (tm, TOP_K), 1)
    idx_o = jnp.zeros((tm, TOP_K), I32)
    gate_o = jnp.zeros((tm, TOP_K), F32)
    rank_o = jnp.zeros((tm, TOP_K), I32)
    for kk in range(TOP_K):
        rk = jnp.sum(jnp.where(lane == idxs[kk], excl, 0.0), axis=-1, keepdims=True)
        idx_o = jnp.where(lane_k == kk, idxs[kk].astype(I32), idx_o)
        gate_o = jnp.where(lane_k == kk, ex[kk] / den, gate_o)
        rank_o = jnp.where(lane_k == kk, rk.astype(I32), rank_o)
    idx_ref[...] = idx_o
    gate_ref[...] = gate_o
    rank_ref[...] = rank_o
    new_run = run + cum[tm - 1:tm, :]
    run_ref[0:1, 0:E] = new_run
    cnt_ref[...] = jnp.broadcast_to(new_run, cnt_ref.shape)


def _merge_router(x2, hg_o, at_o, gates, mod3, whg, wat, wout, g2, wr, br, S):
    T, D = x2.shape
    tm = TM_MERGE
    per_b = S // tm
    E = wr.shape[1]
    tri = jnp.asarray(np.tril(np.ones((tm, tm), np.float32)), dtype=BF16)
    row = lambda i: (i, 0)
    const = lambda i: (0, 0)
    return pl.pallas_call(
        _merge_router_kernel,
        grid=(T // tm,),
        in_specs=[pl.BlockSpec((tm, D), row),
                  pl.BlockSpec((tm, HG_W), row),
                  pl.BlockSpec((tm, ATT_Q_W), row),
                  pl.BlockSpec((tm, 2 * D), row),
                  pl.BlockSpec((None, 6, D), lambda i: (i // per_b, 0, 0)),
                  pl.BlockSpec(whg.shape, const),
                  pl.BlockSpec(wat.shape, const),
                  pl.BlockSpec(wout.shape, const),
                  pl.BlockSpec((1, D), const),
                  pl.BlockSpec(wr.shape, const),
                  pl.BlockSpec((1, E), const),
                  pl.BlockSpec((tm, tm), const)],
        out_specs=[pl.BlockSpec((tm, D), row),
                   pl.BlockSpec((tm * ROW_TILE, V7X_LANES), row),
                   pl.BlockSpec((tm, TOP_K), row),
                   pl.BlockSpec((tm, TOP_K), row),
                   pl.BlockSpec((tm, TOP_K), row),
                   pl.BlockSpec((V7X_SUBLANES, E), const)],
        out_shape=[jax.ShapeDtypeStruct((T, D), F32),
                   jax.ShapeDtypeStruct((T * ROW_TILE, V7X_LANES), F32),
                   jax.ShapeDtypeStruct((T, TOP_K), I32),
                   jax.ShapeDtypeStruct((T, TOP_K), F32),
                   jax.ShapeDtypeStruct((T, TOP_K), I32),
                   jax.ShapeDtypeStruct((V7X_SUBLANES, E), F32)],
        scratch_shapes=[pltpu.VMEM((V7X_SUBLANES, V7X_LANES), F32)],
        compiler_params=_cparams(("arbitrary",)),
        name="merge_router",
    )(x2, hg_o, at_o, gates, mod3, whg, wat, wout, g2.reshape(1, D), wr, br.reshape(1, E), tri)


def _dest_kernel(idx_ref, rank_ref, tab_ref, o_ref):
    idx = idx_ref[...]
    tm = idx.shape[0]
    E = tab_ref.shape[0] // TOP_K
    lane = lax.broadcasted_iota(I32, (tm, tab_ref.shape[0]), 1)
    hot = lane == idx[:, 0:1]
    for kk in range(1, TOP_K):
        hot = hot | (lane == idx[:, kk:kk + 1] + kk * E)
    start = _dot_exact01(hot.astype(F32).astype(BF16), tab_ref[...])
    o_ref[...] = rank_ref[...] + start[:, 0:TOP_K].astype(I32)


def _dest(idx, rank, pad_start):
    T = idx.shape[0]
    tm = 1024
    E = pad_start.shape[0]
    assert TOP_K * E == V7X_LANES
    k_of = np.arange(TOP_K * E) // E
    tab = jnp.where(k_of[:, None] == np.arange(V7X_LANES)[None, :], jnp.tile(pad_start, TOP_K)[:, None], 0).astype(F32)
    row = lambda i: (i, 0)
    return pl.pallas_call(
        _dest_kernel,
        grid=(T // tm,),
        in_specs=[pl.BlockSpec((tm, TOP_K), row), pl.BlockSpec((tm, TOP_K), row),
                  pl.BlockSpec(tab.shape, lambda i: (0, 0))],
        out_specs=pl.BlockSpec((tm, TOP_K), row),
        out_shape=jax.ShapeDtypeStruct((T, TOP_K), I32),
        compiler_params=_cparams(("parallel",)),
        name="dest",
    )(idx, rank, tab)


DISPATCH_UNROLL = 4


def _row(ref, r):
    return ref.at[pl.ds(pl.multiple_of(r * ROW_TILE, ROW_TILE), ROW_TILE)]


def _dispatch_kernel(fill_start_ref, fill_n_ref, tail_ref, dest_ref, h_ref, xbuf_ref, zero_ref, sem):
    tm = h_ref.shape[0] // ROW_TILE
    zrows = zero_ref.shape[0]

    def zero_rows_copy(r, n):
        src = zero_ref.at[pl.ds(0, n * ROW_TILE)]
        dst = xbuf_ref.at[pl.ds(pl.multiple_of(r * ROW_TILE, ROW_TILE), n * ROW_TILE)]
        return pltpu.make_async_copy(src, dst, sem.at[1])

    def zero_block_copy(b):
        dst = xbuf_ref.at[pl.ds(pl.multiple_of(b * zrows, zrows), zrows)]
        return pltpu.make_async_copy(zero_ref, dst, sem.at[2])

    @pl.when(pl.program_id(0) == 0)
    def _():
        zero_ref[...] = jnp.zeros_like(zero_ref)

        def per_expert(start):
            def body(e, c):
                n = fill_n_ref[e]
                r = fill_start_ref[e]
                size = zrows // ROW_TILE // 2
                while size >= 1:
                    @pl.when((n & size) != 0)
                    def _(r=r, size=size):
                        cp = zero_rows_copy(r, size)
                        cp.start() if start else cp.wait()
                    r = r + (n & size)
                    size //= 2
                return c
            return body

        def tail(start):
            def body(b, c):
                cp = zero_block_copy(tail_ref[0] + b)
                cp.start() if start else cp.wait()
                return c
            return body

        lax.fori_loop(0, N_EXPERTS, per_expert(True), 0)
        lax.fori_loop(0, tail_ref[1], tail(True), 0)
        lax.fori_loop(0, N_EXPERTS, per_expert(False), 0)
        lax.fori_loop(0, tail_ref[1], tail(False), 0)

    def issue(tb, c):
        for u in range(DISPATCH_UNROLL):
            t = tb * DISPATCH_UNROLL + u
            src = _row(h_ref, t)
            for kk in range(TOP_K):
                d = dest_ref[t * TOP_K + kk]
                pltpu.make_async_copy(src, _row(xbuf_ref, d), sem.at[0]).start(priority=kk % 2)
        return c

    lax.fori_loop(0, tm // DISPATCH_UNROLL, issue, 0)

    def drain(tb, c):
        for _ in range(DISPATCH_UNROLL * TOP_K):
            pltpu.make_async_copy(_row(h_ref, 0), _row(xbuf_ref, 0), sem.at[0]).wait()
        return c

    lax.fori_loop(0, tm // DISPATCH_UNROLL, drain, 0)


def _dispatch(fill_start, fill_n, tail, dest_flat, h2t, P):
    T = h2t.shape[0] // ROW_TILE
    tm = TM_DISPATCH
    grid_spec = pltpu.PrefetchScalarGridSpec(
        num_scalar_prefetch=3,
        grid=(T // tm,),
        in_specs=[pl.BlockSpec((tm * TOP_K,), lambda i, *_: (i,), memory_space=pltpu.SMEM),
                  pl.BlockSpec((tm * ROW_TILE, V7X_LANES), lambda i, *_: (i, 0))],
        out_specs=pl.BlockSpec(memory_space=pl.ANY),
        scratch_shapes=[pltpu.VMEM((MOE_BM * ROW_TILE, V7X_LANES), F32), pltpu.SemaphoreType.DMA((3,))],
    )
    return pl.pallas_call(
        _dispatch_kernel,
        grid_spec=grid_spec,
        out_shape=jax.ShapeDtypeStruct((P * ROW_TILE, V7X_LANES), F32),
        compiler_params=_cparams(("arbitrary",)),
        name="dispatch",
    )(fill_start, fill_n, tail, dest_flat, h2t)


FFN_PREP_COLS = 256
FFN_UNITS_PER_STEP = 1


def _ffn_kernel(cur_ref, src_ref, slot_ref, pos_ref, last_ref, used_ref,
                x_ref, wu_ref, wd_ref, bg_ref, bl_ref, bd_ref, perm_ref, y_ref,
                wg0, wl0, wd0, wg1, wl1, wd1):
    del cur_ref, src_ref
    s = pl.program_id(0)
    bm = x_ref.shape[0] // ROW_TILE
    D, De2 = wu_ref.shape
    De = De2 // 2
    half = FFN_PREP_COLS // 2
    n_units = De2 // FFN_PREP_COLS
    drows = De // n_units
    slot = slot_ref[s]
    used = used_ref[s] == 1
    stages = ((wg0, wl0, wd0), (wg1, wl1, wd1))
    perm = perm_ref[...]

    def stage_unit(u, dst):
        wg, wl, wdb = dst
        c0 = pl.multiple_of(u * FFN_PREP_COLS, FFN_PREP_COLS)
        p = _dot(wu_ref[:, pl.ds(c0, FFN_PREP_COLS)].astype(BF16), perm)
        r0 = pl.multiple_of(u * half, half)
        wg[:, pl.ds(r0, half)] = p[:, 0:half].astype(BF16)
        wl[:, pl.ds(r0, half)] = p[:, half:2 * half].astype(BF16)
        d0 = pl.multiple_of(u * drows, drows)
        wdb[pl.ds(d0, drows), :] = wd_ref[pl.ds(d0, drows), :].astype(BF16)

    @pl.when(s == 0)
    def _():
        for u in range(n_units):
            stage_unit(u, stages[0])

    done = (pos_ref[s] + 1) * FFN_UNITS_PER_STEP

    def block(cur, nxt):
        @pl.when(used)
        def _():
            u0 = jnp.minimum(pos_ref[s] * FFN_UNITS_PER_STEP, n_units - FFN_UNITS_PER_STEP)
            for j in range(FFN_UNITS_PER_STEP):
                stage_unit(u0 + j, nxt)
            wg, wl, wdb = cur
            x = _load_row_tiles(x_ref, bm).astype(BF16)
            glu = _dot(x, wg[...]) + bg_ref[...]
            lin = _dot(x, wl[...]) + bl_ref[...]
            glu = jnp.minimum(glu, SWIGLU_LIMIT)
            lin = jnp.clip(lin, -SWIGLU_LIMIT, SWIGLU_LIMIT)
            act = glu * _sigmoid(SWIGLU_ALPHA * glu) * (lin + 1.0)
            _store_row_tiles(y_ref, _dot(act.astype(BF16), wdb[...]) + bd_ref[...])

        @pl.when(used & (last_ref[s] == 1) & (done < n_units))
        def _():
            def body(u, c):
                stage_unit(u, nxt)
                return c
            lax.fori_loop(done, n_units, body, 0)

    @pl.when(slot == 0)
    def _():
        block(stages[0], stages[1])

    @pl.when(slot == 1)
    def _():
        block(stages[1], stages[0])

    @pl.when(jnp.logical_not(used) & (s > 0))
    def _():
        y_ref[...] = jnp.zeros_like(y_ref)


def _ffn_schedule(block_expert, n_used):
    n = block_expert.shape[0]
    idx = jnp.arange(n, dtype=I32)
    be = block_expert
    first = jnp.concatenate([jnp.ones((1,), bool), be[1:] != be[:-1]])
    run_start = lax.cummax(jnp.where(first, idx, 0))
    ordinal = jnp.cumsum(first.astype(I32)) - 1
    is_last = jnp.concatenate([first[1:], jnp.ones((1,), bool)])
    next_first = lax.cummin(jnp.where(first, idx, n), reverse=True)
    next_start = jnp.concatenate([next_first[1:], jnp.full((1,), n, I32)])
    next_e = be[jnp.minimum(next_start, n - 1)]
    blk = jnp.maximum(jnp.arange(n + 1, dtype=I32) - 1, 0)
    step = jnp.arange(n + 1, dtype=I32)
    cur = be[blk]
    src = jnp.where(step == 0, be[0], next_e[blk])
    slot = ordinal[blk] % 2
    pos = blk - run_start[blk]
    last = is_last[blk].astype(I32)
    used = ((step >= 1) & (blk < n_used[0])).astype(I32)
    return [a.astype(I32) for a in (cur, src, slot, pos, last, used)]


def _ffn(block_expert, n_used, xbuf, w_up, bg, bl, w_down, bd):
    P = xbuf.shape[0] // ROW_TILE
    bm = MOE_BM
    _, D, De2 = w_up.shape
    De = De2 // 2
    assert (De2 // FFN_PREP_COLS) % FFN_UNITS_PER_STEP == 0
    sched = _ffn_schedule(block_expert, n_used)
    cc = np.arange(FFN_PREP_COLS)
    perm = (cc[:, None] == np.where(cc < FFN_PREP_COLS // 2, 2 * cc, 2 * (cc - FFN_PREP_COLS // 2) + 1)[None, :])
    rows = lambda s, *_: (jnp.maximum(s - 1, 0), 0)
    wmap = lambda s, cur, src, *_: (src[s], 0, 0)
    bmap = lambda s, cur, *_: (cur[s], 0, 0)
    grid_spec = pltpu.PrefetchScalarGridSpec(
        num_scalar_prefetch=len(sched),
        grid=(P // bm + 1,),
        in_specs=[pl.BlockSpec((bm * ROW_TILE, V7X_LANES), rows),
                  pl.BlockSpec((None, D, De2), wmap),
                  pl.BlockSpec((None, De, D), wmap),
                  pl.BlockSpec((None, 1, De), bmap),
                  pl.BlockSpec((None, 1, De), bmap),
                  pl.BlockSpec((None, 1, D), bmap),
                  pl.BlockSpec(perm.shape, lambda s, *_: (0, 0))],
        out_specs=pl.BlockSpec((bm * ROW_TILE, V7X_LANES), rows),
        scratch_shapes=[pltpu.VMEM((D, De), BF16), pltpu.VMEM((D, De), BF16), pltpu.VMEM((De, D), BF16)] * 2,
    )
    return pl.pallas_call(
        _ffn_kernel,
        grid_spec=grid_spec,
        out_shape=jax.ShapeDtypeStruct((P * ROW_TILE, V7X_LANES), F32),
        compiler_params=_cparams(("arbitrary",)),
        name="expert_ffn",
    )(*sched, xbuf, w_up, w_down, bg, bl, bd, jnp.asarray(perm, dtype=BF16))


def _combine_kernel(dcur_ref, dnext_ref, gate_ref, x1_ref, mod_ref, y_hbm, o_ref, buf, sem):
    i = pl.program_id(0)
    n = pl.num_programs(0)
    tm = x1_ref.shape[0]
    slot = i % 2

    def issue(dref, s):
        def body(tb, c):
            for u in range(DISPATCH_UNROLL):
                t = tb * DISPATCH_UNROLL + u
                for kk in range(TOP_K):
                    d = dref[t * TOP_K + kk]
                    pltpu.make_async_copy(_row(y_hbm, d), _row(buf.at[s, kk], t), sem.at[s]).start(priority=kk % 2)
            return c
        lax.fori_loop(0, tm // DISPATCH_UNROLL, body, 0)

    @pl.when(i == 0)
    def _():
        issue(dcur_ref, 0)

    @pl.when(i + 1 < n)
    def _():
        issue(dnext_ref, 1 - slot)

    def drain(tb, c):
        for _ in range(DISPATCH_UNROLL * TOP_K):
            pltpu.make_async_copy(_row(y_hbm, 0), _row(buf.at[slot, 0], 0), sem.at[slot]).wait()
        return c

    lax.fori_loop(0, tm // DISPATCH_UNROLL, drain, 0)
    gate = gate_ref[...]
    acc = gate[:, 0:1] * _load_row_tiles(buf.at[slot, 0], tm)
    for kk in range(1, TOP_K):
        acc = acc + gate[:, kk:kk + 1] * _load_row_tiles(buf.at[slot, kk], tm)
    o_ref[...] = x1_ref[...] + mod_ref[5:6, :] * acc


def _combine(dest_flat, gate, x1, mod3, ybuf, S):
    T, D = x1.shape
    tm = TM_COMBINE
    per_b = S // tm
    nt = T // tm
    return pl.pallas_call(
        _combine_kernel,
        grid=(nt,),
        in_specs=[pl.BlockSpec((tm * TOP_K,), lambda i: (i,), memory_space=pltpu.SMEM),
                  pl.BlockSpec((tm * TOP_K,), lambda i: (jnp.minimum(i + 1, nt - 1),), memory_space=pltpu.SMEM),
                  pl.BlockSpec((tm, TOP_K), lambda i: (i, 0)),
                  pl.BlockSpec((tm, D), lambda i: (i, 0)),
                  pl.BlockSpec((None, 6, D), lambda i: (i // per_b, 0, 0)),
                  pl.BlockSpec(memory_space=pl.ANY)],
        out_specs=pl.BlockSpec((tm, D), lambda i: (i, 0)),
        out_shape=jax.ShapeDtypeStruct((T, D), F32),
        scratch_shapes=[pltpu.VMEM((2, TOP_K, tm * ROW_TILE, V7X_LANES), F32), pltpu.SemaphoreType.DMA((2,))],
        compiler_params=_cparams(("arbitrary",)),
        name="combine",
    )(dest_flat, dest_flat, gate, x1, mod3, ybuf)


def kernel(x, c, positions, w_ada, b_ada, norm1_gain, w_in, lower_bound_logits, hg_norm_gain, w_hg_branch,
           q_norm_gain, k_norm_gain, attn_sinks, w_attn_branch, w_out, norm2_gain, w_router, b_router,
           w_up, b_up, w_down, b_down):
    B, S, D = x.shape
    T = B * S
    assert w_ada.shape[0] == 1, "one layer"
    x2 = x.reshape(T, D)

    mod = _ada(c, w_ada[0], b_ada[0])
    mod3 = mod.reshape(B, 6, D)

    hg_in, at_in, gates, f_min = _inproj(x2, mod3, norm1_gain[0], w_in[0].astype(BF16), S)
    f_min = f_min[::V7X_SUBLANES, 0].reshape(B, S // TM_PROJ)
    hg_o = _hgrn(hg_in, f_min, lower_bound_logits, hg_norm_gain[0], B, S)
    at_o = _swa(at_in, positions, q_norm_gain[0], k_norm_gain[0], attn_sinks[0], B, S)

    x1, h2, idx, gate, rank, cnt = _merge_router(
        x2, hg_o, at_o, gates, mod3, w_hg_branch[0].astype(BF16), w_attn_branch[0].astype(BF16),
        w_out[0].astype(BF16), norm2_gain[0], w_router[0].astype(BF16), b_router[0], S)

    bm = MOE_BM
    counts = cnt[0].astype(I32)
    padded = (counts + bm - 1) // bm * bm
    pad_end = jnp.cumsum(padded)
    pad_start = pad_end - padded
    P = T * TOP_K + N_EXPERTS * bm
    n_blocks = P // bm
    block_start = jnp.arange(n_blocks, dtype=I32) * bm
    block_expert = jnp.minimum(
        jnp.sum((pad_end[None, :] <= block_start[:, None]).astype(I32), axis=1), N_EXPERTS - 1).astype(I32)
    n_used = (pad_end[-1:] // bm).astype(I32)
    tail = jnp.concatenate([n_used, n_blocks - n_used]).astype(I32)

    dest = _dest(idx, rank, pad_start)
    dest_flat = dest.reshape(T * TOP_K)
    xbuf = _dispatch((pad_start + counts).astype(I32), (padded - counts).astype(I32), tail, dest_flat, h2, P)

    ybuf = _ffn(block_expert, n_used, xbuf, w_up[0],
                b_up[0][:, None, 0::2], b_up[0][:, None, 1::2],
                w_down[0], b_down[0][:, None, :])

    out = _combine(dest_flat, gate, x1, mod3, ybuf, S)
    return out.reshape(B, S, D)
```

```python
import functools

import numpy as np
import jax
import jax.numpy as jnp
from jax import lax
from jax.experimental import pallas as pl
from jax.experimental.pallas import tpu as pltpu

F32 = jnp.float32
BF16 = jnp.bfloat16
I32 = jnp.int32

HG_HEADS = 4
HG_DIM = 128
HG_W = HG_HEADS * HG_DIM
ATT_Q_HEADS = 8
ATT_KV_HEADS = 2
ATT_GROUP = ATT_Q_HEADS // ATT_KV_HEADS
ATT_DH = 64
ATT_Q_W = ATT_Q_HEADS * ATT_DH
ATT_KV_W = ATT_KV_HEADS * ATT_DH
ATT_BLOCK = 128
ROPE_THETA = 500000.0
ROPE_DIM = ATT_DH // 4
N_EXPERTS = 32
TOP_K = 4
SWIGLU_ALPHA = 1.702
SWIGLU_LIMIT = 7.0
EPS = 1e-6

V7X_LANES = 128
V7X_SUBLANES = 8
V7X_VMEM_LIMIT_BYTES = 56 * 1024 * 1024

TM_PROJ = 256
TM_MERGE = 512
HG_TILE = 512
HG_CHUNK = 128
HG_SUB = 8
MOE_BM = 512
TM_DISPATCH = 512
TM_COMBINE = 256

NEG_INF = float("-inf")


def _cparams(sem, vmem=V7X_VMEM_LIMIT_BYTES):
    return pltpu.CompilerParams(dimension_semantics=sem, vmem_limit_bytes=vmem)


def _sigmoid(x):
    return 1.0 / (1.0 + jnp.exp(-x))


def _dot(a, b):
    return jnp.dot(a, b, preferred_element_type=F32)


def _dot_nt(a, b):
    return lax.dot_general(a, b, (((1,), (1,)), ((), ())), preferred_element_type=F32)


def _dot_tn(a, b):
    return lax.dot_general(a, b, (((0,), (0,)), ((), ())), preferred_element_type=F32)


ROW_TILE = V7X_SUBLANES


def _store_row_tiles(ref, val):
    rows = val.shape[0]
    for g in range(ROW_TILE):
        ref[pl.ds(g, rows, stride=ROW_TILE), :] = val[:, g * V7X_LANES:(g + 1) * V7X_LANES]


def _load_row_tiles(ref, rows):
    return jnp.concatenate([ref[pl.ds(g, rows, stride=ROW_TILE), :] for g in range(ROW_TILE)], axis=1)


def _dot_exact01(m01, x):
    hi = x.astype(BF16)
    r1 = x - hi.astype(F32)
    mid = r1.astype(BF16)
    lo = (r1 - mid.astype(F32)).astype(BF16)
    return _dot(m01, hi) + _dot(m01, mid) + _dot(m01, lo)


def _ada_kernel(c_ref, w_ref, b_ref, o_ref):
    c = c_ref[...]
    cond = c * _sigmoid(c)
    o_ref[...] = _dot(cond.astype(BF16), w_ref[...].astype(BF16)) + b_ref[...]


def _ada(c, w_ada, b_ada):
    B, D = c.shape
    N = w_ada.shape[1]
    tn = D
    return pl.pallas_call(
        _ada_kernel,
        grid=(N // tn,),
        in_specs=[pl.BlockSpec((B, D), lambda j: (0, 0)),
                  pl.BlockSpec((D, tn), lambda j: (0, j)),
                  pl.BlockSpec((1, tn), lambda j: (0, j))],
        out_specs=pl.BlockSpec((B, tn), lambda j: (0, j)),
        out_shape=jax.ShapeDtypeStruct((B, N), F32),
        compiler_params=_cparams(("parallel",)),
        name="ada",
    )(c, w_ada, b_ada.reshape(1, N))


def _inproj_kernel(x_ref, mod_ref, g_ref, w_ref, hg_ref, at_ref, gt_ref, fmin_ref):
    x = x_ref[...]
    ms = jnp.mean(x * x, axis=-1, keepdims=True)
    y = x * lax.rsqrt(ms + EPS) * g_ref[...]
    h = (y * (1.0 + mod_ref[1:2, :]) + mod_ref[0:1, :]).astype(BF16)
    n_hg = hg_ref.shape[1]
    n_at = at_ref.shape[1]
    n_gt = gt_ref.shape[1]
    hg = _dot(h, w_ref[:, 0:n_hg])
    hg_ref[...] = hg
    f_pre = hg[:, HG_W:2 * HG_W]
    f_min = jnp.min(jnp.min(f_pre, axis=-1, keepdims=True), axis=0, keepdims=True)
    fmin_ref[...] = jnp.broadcast_to(f_min, fmin_ref.shape)
    at_ref[...] = _dot(h, w_ref[:, n_hg:n_hg + n_at])
    gt_ref[...] = _dot(h, w_ref[:, n_hg + n_at:n_hg + n_at + n_gt])


def _inproj(x2, mod3, gain, w_in_bf, S):
    T, D = x2.shape
    tm = TM_PROJ
    n_hg = 4 * HG_W
    n_at = ATT_Q_W + 2 * ATT_KV_W
    n_gt = 2 * D
    assert w_in_bf.shape[1] == n_hg + n_at + n_gt
    per_b = S // tm
    return pl.pallas_call(
        _inproj_kernel,
        grid=(T // tm,),
        in_specs=[pl.BlockSpec((tm, D), lambda i: (i, 0)),
                  pl.BlockSpec((None, 6, D), lambda i: (i // per_b, 0, 0)),
                  pl.BlockSpec((1, D), lambda i: (0, 0)),
                  pl.BlockSpec(w_in_bf.shape, lambda i: (0, 0))],
        out_specs=[pl.BlockSpec((tm, n_hg), lambda i: (i, 0)),
                   pl.BlockSpec((tm, n_at), lambda i: (i, 0)),
                   pl.BlockSpec((tm, n_gt), lambda i: (i, 0)),
                   pl.BlockSpec((V7X_SUBLANES, V7X_LANES), lambda i: (i, 0))],
        out_shape=[jax.ShapeDtypeStruct((T, n_hg), F32),
                   jax.ShapeDtypeStruct((T, n_at), F32),
                   jax.ShapeDtypeStruct((T, n_gt), F32),
                   jax.ShapeDtypeStruct((T // tm * V7X_SUBLANES, V7X_LANES), F32)],
        compiler_params=_cparams(("parallel",)),
        name="inproj",
    )(x2, mod3, gain.reshape(1, D), w_in_bf)


def _hgrn_constants():
    C = HG_CHUNK
    tri = (np.arange(C)[None, :] <= np.arange(C)[:, None]).astype(np.float32)
    halves = []
    lh = C // 2
    while lh >= HG_SUB:
        halves.append(lh)
        lh //= 2
    code = np.zeros((C, C), np.int32)
    ii, jj = np.meshgrid(np.arange(C), np.arange(C), indexing="ij")
    for li, lh in enumerate(halves):
        sel = ((ii // (2 * lh)) == (jj // (2 * lh))) & ((ii % (2 * lh)) >= lh) & ((jj % (2 * lh)) < lh)
        code[sel] = li + 1
    return tri, code, halves


HG_MILD_FMIN = 1.0e9


def _hgrn_kernel(mild_ref, in_ref, lbl_ref, gain_ref, tri_ref, code_ref, wsum_ref, o_ref, st_ref, *, halves):
    C = HG_CHUNK
    W = HG_W
    n_chunks = in_ref.shape[1] // C
    step_is_mild = mild_ref[pl.program_id(0) * pl.num_programs(1) + pl.program_id(1)] == 1

    @pl.when(pl.program_id(1) == 0)
    def _():
        st_ref[...] = jnp.zeros_like(st_ref)

    l0 = lbl_ref[0:1, :]
    l1 = lbl_ref[1:2, :]
    mx = jnp.maximum(l0, l1)
    e0 = jnp.exp(l0 - mx)
    lb = e0 / (e0 + jnp.exp(l1 - mx))
    gain = gain_ref[...]
    tri = tri_ref[...].astype(BF16)
    code = code_ref[...]
    wsum = wsum_ref[...]
    row = lax.broadcasted_iota(I32, (C, C), 0)
    col = lax.broadcasted_iota(I32, (C, C), 1)
    row_in_sub = lax.broadcasted_iota(I32, (C, HG_DIM), 0) % HG_SUB
    same_sub = (row // HG_SUB) == (col // HG_SUB)
    sub_causal = same_sub & (col <= row)

    def group_row(a, group, r):
        a3 = a.reshape(C // group, group, a.shape[1])
        return jnp.broadcast_to(a3[:, r:r + 1, :], a3.shape).reshape(a.shape)

    def bcast_sub(a, j):
        return group_row(a, HG_SUB, j)

    def chunk(mild, c, carry):
        r0 = pl.multiple_of(c * C, C)
        rows = pl.ds(r0, C)
        for bb in range(in_ref.shape[0]):
            qp = in_ref[bb, rows, 0:W]
            fp = in_ref[bb, rows, W:2 * W]
            vv = in_ref[bb, rows, 2 * W:3 * W]
            gp = in_ref[bb, rows, 3 * W:4 * W]
            forget = lb + (1.0 - lb) * _sigmoid(fp)
            q = qp * _sigmoid(qp)
            k = 1.0 - forget
            lf = jnp.log(forget)
            b = _dot_exact01(tri, lf)
            e_b = jnp.exp(b)
            e_st = jnp.exp(group_row(b, C, C - 1) - b)
            e_lv = [jnp.exp(-jnp.abs(b - group_row(b, 2 * lh, lh - 1))) for lh in halves]
            if mild:
                x_sub = group_row(b, HG_SUB, HG_SUB - 1) - b
                e_subk = jnp.exp(x_sub)
                e_subq = jnp.exp(-x_sub)
            outs = []
            for h in range(HG_HEADS):
                ls = slice(h * HG_DIM, (h + 1) * HG_DIM)
                qh, kh, vh = q[:, ls], k[:, ls], vv[:, ls]
                bh = b[:, ls]
                st = st_ref[bb, h]
                o = _dot_nt((qh * e_b[:, ls]).astype(BF16), st.astype(BF16))
                if mild:
                    s = _dot_nt((qh * e_subq[:, ls]).astype(BF16), (kh * e_subk[:, ls]).astype(BF16))
                    scores = jnp.where(sub_causal, s, 0.0)
                else:
                    ps = []
                    for j in range(HG_SUB):
                        d = jnp.where(row_in_sub >= j, bh - bcast_sub(bh, j), NEG_INF)
                        ps.append((qh * bcast_sub(kh, j) * jnp.exp(d)).astype(BF16))
                    scores = jnp.where(same_sub, _dot(jnp.concatenate(ps, axis=1), wsum), 0.0)
                for li in range(len(halves)):
                    e = e_lv[li][:, ls]
                    s = _dot_nt((qh * e).astype(BF16), (kh * e).astype(BF16))
                    scores = jnp.where(code == li + 1, s, scores)
                o = o + _dot(scores.astype(BF16), vh.astype(BF16))
                kst = (kh * e_st[:, ls]).astype(BF16)
                st_ref[bb, h] = st * e_b[C - 1:C, ls] + _dot_tn(vh.astype(BF16), kst)
                ms = jnp.mean(o * o, axis=-1, keepdims=True)
                on = o * lax.rsqrt(ms + EPS) * gain
                gh = gp[:, ls]
                outs.append(on * (gh * _sigmoid(gh)))
            o_ref[bb, rows, :] = jnp.concatenate(outs, axis=-1).astype(o_ref.dtype)
        return carry

    @pl.when(step_is_mild)
    def _():
        lax.fori_loop(0, n_chunks, functools.partial(chunk, True), 0)

    @pl.when(jnp.logical_not(step_is_mild))
    def _():
        lax.fori_loop(0, n_chunks, functools.partial(chunk, False), 0)


HG_SEQS = 4


def _hgrn(hg_in, f_min, lb_logits, gain, B, S):
    T = hg_in.shape[0]
    lt = HG_TILE
    tri, code, halves = _hgrn_constants()
    nseq = HG_SEQS if B % HG_SEQS == 0 else 1
    step_min = jnp.min(f_min.reshape(B // nseq, nseq, S // lt, lt // TM_PROJ), axis=(1, 3))
    mild = (step_min >= HG_MILD_FMIN).astype(I32).reshape(-1)
    wsum = (np.arange(HG_SUB * HG_DIM)[:, None] // HG_DIM == np.arange(HG_CHUNK)[None, :] % HG_SUB).astype(np.float32)
    const = lambda b, s, m: (0, 0)
    grid_spec = pltpu.PrefetchScalarGridSpec(
        num_scalar_prefetch=1,
        grid=(B // nseq, S // lt),
        in_specs=[pl.BlockSpec((nseq, lt, 4 * HG_W), lambda b, s, m: (b, s, 0)),
                  pl.BlockSpec(lb_logits.shape, const),
                  pl.BlockSpec((1, HG_DIM), const),
                  pl.BlockSpec(tri.shape, const),
                  pl.BlockSpec(code.shape, const),
                  pl.BlockSpec(wsum.shape, const)],
        out_specs=pl.BlockSpec((nseq, lt, HG_W), lambda b, s, m: (b, s, 0)),
        scratch_shapes=[pltpu.VMEM((nseq, HG_HEADS, HG_DIM, HG_DIM), F32)],
    )
    out = pl.pallas_call(
        functools.partial(_hgrn_kernel, halves=tuple(halves)),
        grid_spec=grid_spec,
        out_shape=jax.ShapeDtypeStruct((B, S, HG_W), BF16),
        compiler_params=_cparams(("parallel", "arbitrary")),
        name="hgrn2",
    )(mild, hg_in.reshape(B, S, 4 * HG_W), lb_logits, gain.reshape(1, HG_DIM), jnp.asarray(tri),
      jnp.asarray(code), jnp.asarray(wsum, dtype=BF16))
    return out.reshape(T, HG_W)


SWA_QBLOCKS = 8
ROPE_ROWS = 16


def _swa_constants():
    lane = np.arange(V7X_LANES)
    c = lane % ATT_DH
    half = ROPE_DIM // 2
    inv_freq = ROPE_THETA ** (-jnp.arange(half, dtype=F32) / half)
    freq_rows = jnp.broadcast_to(
        jnp.concatenate([inv_freq, jnp.zeros((ROPE_ROWS - half,), F32)])[:, None], (ROPE_ROWS, V7X_LANES))
    sel = ((np.arange(ROPE_ROWS)[:, None] == (c % half)[None, :]) & (c < ROPE_DIM)[None, :]).astype(np.float32)
    sign = np.where(c < half, -1.0, np.where(c < ROPE_DIM, 1.0, 0.0)).astype(np.float32)
    first = (c < half).astype(np.float32)
    tab = np.stack([sign, first], axis=0)
    g = (lane[:, None] // ATT_DH == lane[None, :] // ATT_DH).astype(np.float32) / ATT_DH
    return freq_rows, sel, tab, g


def _swa_kernel(sink_ref, cur_ref, prev_ref, pcur_ref, pprev_ref, qg_ref, kg_ref, freq_ref, sel_ref, tab_ref, g_ref,
                o_ref):
    Bq = ATT_BLOCK
    n = pl.program_id(1)
    tab = tab_ref[...]
    sign, first = tab[0:1, :], tab[1:2, :]
    gmat = g_ref[...].astype(BF16)
    sel = sel_ref[...].astype(BF16)
    freq_rows = freq_ref[...]
    half = ROPE_DIM // 2
    scale = ATT_DH ** -0.5

    def group_ms(x):
        sq = x * x
        hi = sq.astype(BF16)
        lo = (sq - hi.astype(F32)).astype(BF16)
        return _dot(hi, gmat) + _dot(lo, gmat)

    def spread(a):
        hi = a.astype(BF16)
        r1 = a - hi.astype(F32)
        mid = r1.astype(BF16)
        lo = (r1 - mid.astype(F32)).astype(BF16)
        return _dot_tn(hi, sel) + _dot_tn(mid, sel) + _dot_tn(lo, sel)

    def rope_table(pos_row):
        ang = freq_rows * pos_row
        return 1.0 + spread(jnp.cos(ang) - 1.0), spread(jnp.sin(ang)) * sign

    def norm_rope(x, gain, cs_sn):
        y = x * lax.rsqrt(group_ms(x) + EPS) * gain
        partner = jnp.where(first > 0.5, pltpu.roll(y, V7X_LANES - half, 1), pltpu.roll(y, half, 1))
        return y * cs_sn[0] + partner * cs_sn[1]

    kq = ATT_Q_W
    lane = lax.broadcasted_iota(I32, (Bq, V7X_LANES), 1)
    lo_half = lane < ATT_DH

    def pad_variants(a):
        r = pltpu.roll(a, ATT_DH, 1)
        z = jnp.zeros_like(a)
        return [[jnp.where(lo_half, a, z).astype(BF16), jnp.where(lo_half, z, r).astype(BF16)],
                [jnp.where(lo_half, r, z).astype(BF16), jnp.where(lo_half, z, a).astype(BF16)]]

    tables = [rope_table(pprev_ref[0].astype(F32))]
    kblocks = [pad_variants(norm_rope(prev_ref[:, 0:ATT_KV_W], kg_ref[...], tables[0]))]
    vblocks = [pad_variants(prev_ref[:, ATT_KV_W:2 * ATT_KV_W])]
    for j in range(SWA_QBLOCKS):
        rows = slice(j * Bq, (j + 1) * Bq)
        tables.append(rope_table(pcur_ref[j].astype(F32)))
        kblocks.append(pad_variants(norm_rope(cur_ref[rows, kq:kq + ATT_KV_W], kg_ref[...], tables[j + 1])))
        vblocks.append(pad_variants(cur_ref[rows, kq + ATT_KV_W:kq + 2 * ATT_KV_W]))

    qi = lax.broadcasted_iota(I32, (Bq, 2 * Bq), 0)
    kj = lax.broadcasted_iota(I32, (Bq, 2 * Bq), 1)
    in_band = ((kj < Bq) & (kj > qi)) | ((kj >= Bq) & ((kj - Bq) <= qi))
    first_of_seq = (jnp.zeros((Bq, 2 * Bq), I32) + n) == 0
    for j in range(SWA_QBLOCKS):
        rows = slice(j * Bq, (j + 1) * Bq)
        mask = (in_band & jnp.logical_not(first_of_seq & (kj < Bq))) if j == 0 else in_band
        for t in range(ATT_Q_W // V7X_LANES):
            ls = slice(t * V7X_LANES, (t + 1) * V7X_LANES)
            qt = (norm_rope(cur_ref[rows, ls], qg_ref[...], tables[j + 1]) * scale).astype(BF16)
            acc = jnp.zeros((Bq, V7X_LANES), F32)
            for u in range(2):
                head = 2 * t + u
                kvh = head // ATT_GROUP
                kcat = jnp.concatenate([kblocks[j][kvh][u], kblocks[j + 1][kvh][u]], axis=0)
                vcat = jnp.concatenate([vblocks[j][kvh][u], vblocks[j + 1][kvh][u]], axis=0)
                s = jnp.where(mask, _dot_nt(qt, kcat), NEG_INF)
                sink = sink_ref[head]
                m = jnp.maximum(jnp.max(s, axis=-1, keepdims=True), sink)
                p = jnp.exp(s - m)
                denom = jnp.sum(p, axis=-1, keepdims=True) + jnp.exp(sink - m)
                acc = acc + _dot(p.astype(BF16), vcat) * (1.0 / denom)
            o_ref[rows, ls] = acc.astype(o_ref.dtype)


def _swa(at_in, positions, q_gain, k_gain, sinks, B, S):
    T = at_in.shape[0]
    nb = S // ATT_BLOCK
    qb = SWA_QBLOCKS
    assert nb % qb == 0
    steps = nb // qb
    freq_rows, sel, tab, g = _swa_constants()
    qg = jnp.tile(q_gain.reshape(1, ATT_DH), (1, V7X_LANES // ATT_DH))
    kg = jnp.tile(k_gain.reshape(1, ATT_DH), (1, V7X_LANES // ATT_DH))
    pos3 = positions.reshape(B * nb, 1, ATT_BLOCK)
    n_at = at_in.shape[1]
    kv_blk = 2 * ATT_KV_W
    assert ATT_Q_W % kv_blk == 0
    prev_blk = lambda b, n: b * nb + jnp.maximum(qb * n - 1, 0)
    const = lambda b, n: (0, 0)
    return pl.pallas_call(
        _swa_kernel,
        grid=(B, steps),
        in_specs=[pl.BlockSpec(memory_space=pltpu.SMEM),
                  pl.BlockSpec((qb * ATT_BLOCK, n_at), lambda b, n: (b * steps + n, 0)),
                  pl.BlockSpec((ATT_BLOCK, kv_blk), lambda b, n: (prev_blk(b, n), ATT_Q_W // kv_blk)),
                  pl.BlockSpec((qb, 1, ATT_BLOCK), lambda b, n: (b * steps + n, 0, 0)),
                  pl.BlockSpec((1, 1, ATT_BLOCK), lambda b, n: (prev_blk(b, n), 0, 0)),
                  pl.BlockSpec((1, V7X_LANES), const),
                  pl.BlockSpec((1, V7X_LANES), const),
                  pl.BlockSpec(freq_rows.shape, const),
                  pl.BlockSpec(sel.shape, const),
                  pl.BlockSpec(tab.shape, const),
                  pl.BlockSpec(g.shape, const)],
        out_specs=pl.BlockSpec((qb * ATT_BLOCK, ATT_Q_W), lambda b, n: (b * steps + n, 0)),
        out_shape=jax.ShapeDtypeStruct((T, ATT_Q_W), BF16),
        compiler_params=_cparams(("parallel", "parallel")),
        name="swa",
    )(sinks, at_in, at_in, pos3, pos3, qg, kg, freq_rows, jnp.asarray(sel), jnp.asarray(tab), jnp.asarray(g))


def _merge_router_kernel(x_ref, hg_ref, at_ref, gt_ref, mod_ref, whg_ref, wat_ref, wout_ref, g2_ref,
                         wr_ref, br_ref, tri_ref,
                         x1_ref, h2_ref, idx_ref, gate_ref, rank_ref, cnt_ref, run_ref):
    i = pl.program_id(0)
    D = x_ref.shape[1]
    tm = x_ref.shape[0]

    @pl.when(i == 0)
    def _():
        run_ref[...] = jnp.zeros_like(run_ref)

    y_h = _dot(hg_ref[...], whg_ref[...])
    y_a = _dot(at_ref[...], wat_ref[...])
    merged = _sigmoid(gt_ref[:, 0:D]) * y_h + _sigmoid(gt_ref[:, D:2 * D]) * y_a
    x1 = x_ref[...] + mod_ref[2:3, :] * _dot(merged.astype(BF16), wout_ref[...])
    x1_ref[...] = x1
    ms = jnp.mean(x1 * x1, axis=-1, keepdims=True)
    h2 = x1 * lax.rsqrt(ms + EPS) * g2_ref[...] * (1.0 + mod_ref[4:5, :]) + mod_ref[3:4, :]
    _store_row_tiles(h2_ref, h2)
    logits = _dot(h2.astype(BF16), wr_ref[...]) + br_ref[...]
    E = logits.shape[1]
    lane = lax.broadcasted_iota(I32, (tm, E), 1).astype(F32)
    vals, idxs = [], []
    l = logits
    for _ in range(TOP_K):
        m = jnp.max(l, axis=-1, keepdims=True)
        ik = jnp.min(jnp.where(l == m, lane, float(E)), axis=-1, keepdims=True)
        vals.append(m)
        idxs.append(ik)
        l = jnp.where(lane == ik, NEG_INF, l)
    ex = [jnp.exp(v - vals[0]) for v in vals]
    den = ex[0]
    for e in ex[1:]:
        den = den + e
    onehot = jnp.zeros((tm, E), F32)
    for ik in idxs:
        onehot = onehot + (lane == ik).astype(F32)
    cum = _dot(tri_ref[...], onehot.astype(BF16))
    run = run_ref[0:1, 0:E]
    excl = cum - onehot + run
    lane_k = lax.broadcasted_iota(I32, (tm, TOP_K), 1)
    idx_o = jnp.zeros((tm, TOP_K), I32)
    gate_o = jnp.zeros((tm, TOP_K), F32)
    rank_o = jnp.zeros((tm, TOP_K), I32)
    for kk in range(TOP_K):
        rk = jnp.sum(jnp.where(lane == idxs[kk], excl, 0.0), axis=-1, keepdims=True)
        idx_o = jnp.where(lane_k == kk, idxs[kk].astype(I32), idx_o)
        gate_o = jnp.where(lane_k == kk, ex[kk] / den, gate_o)
        rank_o = jnp.where(lane_k == kk, rk.astype(I32), rank_o)
    idx_ref[...] = idx_o
    gate_ref[...] = gate_o
    rank_ref[...] = rank_o
    new_run = run + cum[tm - 1:tm, :]
    run_ref[0:1, 0:E] = new_run
    cnt_ref[...] = jnp.broadcast_to(new_run, cnt_ref.shape)


def _merge_router(x2, hg_o, at_o, gates, mod3, whg, wat, wout, g2, wr, br, S):
    T, D = x2.shape
    tm = TM_MERGE
    per_b = S // tm
    E = wr.shape[1]
    tri = jnp.asarray(np.tril(np.ones((tm, tm), np.float32)), dtype=BF16)
    row = lambda i: (i, 0)
    const = lambda i: (0, 0)
    return pl.pallas_call(
        _merge_router_kernel,
        grid=(T // tm,),
        in_specs=[pl.BlockSpec((tm, D), row),
                  pl.BlockSpec((tm, HG_W), row),
                  pl.BlockSpec((tm, ATT_Q_W), row),
                  pl.BlockSpec((tm, 2 * D), row),
                  pl.BlockSpec((None, 6, D), lambda i: (i // per_b, 0, 0)),
                  pl.BlockSpec(whg.shape, const),
                  pl.BlockSpec(wat.shape, const),
                  pl.BlockSpec(wout.shape, const),
                  pl.BlockSpec((1, D), const),
                  pl.BlockSpec(wr.shape, const),
                  pl.BlockSpec((1, E), const),
                  pl.BlockSpec((tm, tm), const)],
        out_specs=[pl.BlockSpec((tm, D), row),
                   pl.BlockSpec((tm * ROW_TILE, V7X_LANES), row),
                   pl.BlockSpec((tm, TOP_K), row),
                   pl.BlockSpec((tm, TOP_K), row),
                   pl.BlockSpec((tm, TOP_K), row),
                   pl.BlockSpec((V7X_SUBLANES, E), const)],
        out_shape=[jax.ShapeDtypeStruct((T, D), F32),
                   jax.ShapeDtypeStruct((T * ROW_TILE, V7X_LANES), F32),
                   jax.ShapeDtypeStruct((T, TOP_K), I32),
                   jax.ShapeDtypeStruct((T, TOP_K), F32),
                   jax.ShapeDtypeStruct((T, TOP_K), I32),
                   jax.ShapeDtypeStruct((V7X_SUBLANES, E), F32)],
        scratch_shapes=[pltpu.VMEM((V7X_SUBLANES, V7X_LANES), F32)],
        compiler_params=_cparams(("arbitrary",)),
        name="merge_router",
    )(x2, hg_o, at_o, gates, mod3, whg, wat, wout, g2.reshape(1, D), wr, br.reshape(1, E), tri)


def _dest_kernel(idx_ref, rank_ref, tab_ref, o_ref):
    idx = idx_ref[...]
    tm = idx.shape[0]
    E = tab_ref.shape[0] // TOP_K
    lane = lax.broadcasted_iota(I32, (tm, tab_ref.shape[0]), 1)
    hot = lane == idx[:, 0:1]
    for kk in range(1, TOP_K):
        hot = hot | (lane == idx[:, kk:kk + 1] + kk * E)
    start = _dot_exact01(hot.astype(F32).astype(BF16), tab_ref[...])
    o_ref[...] = rank_ref[...] + start[:, 0:TOP_K].astype(I32)


def _dest(idx, rank, pad_start):
    T = idx.shape[0]
    tm = 1024
    E = pad_start.shape[0]
    assert TOP_K * E == V7X_LANES
    k_of = np.arange(TOP_K * E) // E
    tab = jnp.where(k_of[:, None] == np.arange(V7X_LANES)[None, :], jnp.tile(pad_start, TOP_K)[:, None], 0).astype(F32)
    row = lambda i: (i, 0)
    return pl.pallas_call(
        _dest_kernel,
        grid=(T // tm,),
        in_specs=[pl.BlockSpec((tm, TOP_K), row), pl.BlockSpec((tm, TOP_K), row),
                  pl.BlockSpec(tab.shape, lambda i: (0, 0))],
        out_specs=pl.BlockSpec((tm, TOP_K), row),
        out_shape=jax.ShapeDtypeStruct((T, TOP_K), I32),
        compiler_params=_cparams(("parallel",)),
        name="dest",
    )(idx, rank, tab)


DISPATCH_UNROLL = 4


def _row(ref, r):
    return ref.at[pl.ds(pl.multiple_of(r * ROW_TILE, ROW_TILE), ROW_TILE)]


def _dispatch_kernel(fill_start_ref, fill_n_ref, tail_ref, dest_ref, h_ref, xbuf_ref, zero_ref, sem):
    tm = h_ref.shape[0] // ROW_TILE
    zrows = zero_ref.shape[0]

    def zero_rows_copy(r, n):
        src = zero_ref.at[pl.ds(0, n * ROW_TILE)]
        dst = xbuf_ref.at[pl.ds(pl.multiple_of(r * ROW_TILE, ROW_TILE), n * ROW_TILE)]
        return pltpu.make_async_copy(src, dst, sem.at[1])

    def zero_block_copy(b):
        dst = xbuf_ref.at[pl.ds(pl.multiple_of(b * zrows, zrows), zrows)]
        return pltpu.make_async_copy(zero_ref, dst, sem.at[2])

    @pl.when(pl.program_id(0) == 0)
    def _():
        zero_ref[...] = jnp.zeros_like(zero_ref)

        def per_expert(start):
            def body(e, c):
                n = fill_n_ref[e]
                r = fill_start_ref[e]
                size = zrows // ROW_TILE // 2
                while size >= 1:
                    @pl.when((n & size) != 0)
                    def _(r=r, size=size):
                        cp = zero_rows_copy(r, size)
                        cp.start() if start else cp.wait()
                    r = r + (n & size)
                    size //= 2
                return c
            return body

        def tail(start):
            def body(b, c):
                cp = zero_block_copy(tail_ref[0] + b)
                cp.start() if start else cp.wait()
                return c
            return body

        lax.fori_loop(0, N_EXPERTS, per_expert(True), 0)
        lax.fori_loop(0, tail_ref[1], tail(True), 0)
        lax.fori_loop(0, N_EXPERTS, per_expert(False), 0)
        lax.fori_loop(0, tail_ref[1], tail(False), 0)

    def issue(tb, c):
        for u in range(DISPATCH_UNROLL):
            t = tb * DISPATCH_UNROLL + u
            src = _row(h_ref, t)
            for kk in range(TOP_K):
                d = dest_ref[t * TOP_K + kk]
                pltpu.make_async_copy(src, _row(xbuf_ref, d), sem.at[0]).start(priority=kk % 2)
        return c

    lax.fori_loop(0, tm // DISPATCH_UNROLL, issue, 0)

    def drain(tb, c):
        for _ in range(DISPATCH_UNROLL * TOP_K):
            pltpu.make_async_copy(_row(h_ref, 0), _row(xbuf_ref, 0), sem.at[0]).wait()
        return c

    lax.fori_loop(0, tm // DISPATCH_UNROLL, drain, 0)


def _dispatch(fill_start, fill_n, tail, dest_flat, h2t, P):
    T = h2t.shape[0] // ROW_TILE
    tm = TM_DISPATCH
    grid_spec = pltpu.PrefetchScalarGridSpec(
        num_scalar_prefetch=3,
        grid=(T // tm,),
        in_specs=[pl.BlockSpec((tm * TOP_K,), lambda i, *_: (i,), memory_space=pltpu.SMEM),
                  pl.BlockSpec((tm * ROW_TILE, V7X_LANES), lambda i, *_: (i, 0))],
        out_specs=pl.BlockSpec(memory_space=pl.ANY),
        scratch_shapes=[pltpu.VMEM((MOE_BM * ROW_TILE, V7X_LANES), F32), pltpu.SemaphoreType.DMA((3,))],
    )
    return pl.pallas_call(
        _dispatch_kernel,
        grid_spec=grid_spec,
        out_shape=jax.ShapeDtypeStruct((P * ROW_TILE, V7X_LANES), F32),
        compiler_params=_cparams(("arbitrary",)),
        name="dispatch",
    )(fill_start, fill_n, tail, dest_flat, h2t)


FFN_PREP_COLS = 256
FFN_UNITS_PER_STEP = 1


def _ffn_kernel(cur_ref, src_ref, slot_ref, pos_ref, last_ref, used_ref,
                x_ref, wu_ref, wd_ref, bg_ref, bl_ref, bd_ref, perm_ref, y_ref,
                wg0, wl0, wd0, wg1, wl1, wd1):
    del cur_ref, src_ref
    s = pl.program_id(0)
    bm = x_ref.shape[0] // ROW_TILE
    D, De2 = wu_ref.shape
    De = De2 // 2
    half = FFN_PREP_COLS // 2
    n_units = De2 // FFN_PREP_COLS
    drows = De // n_units
    slot = slot_ref[s]
    used = used_ref[s] == 1
    stages = ((wg0, wl0, wd0), (wg1, wl1, wd1))
    perm = perm_ref[...]

    def stage_unit(u, dst):
        wg, wl, wdb = dst
        c0 = pl.multiple_of(u * FFN_PREP_COLS, FFN_PREP_COLS)
        p = _dot(wu_ref[:, pl.ds(c0, FFN_PREP_COLS)].astype(BF16), perm)
        r0 = pl.multiple_of(u * half, half)
        wg[:, pl.ds(r0, half)] = p[:, 0:half].astype(BF16)
        wl[:, pl.ds(r0, half)] = p[:, half:2 * half].astype(BF16)
        d0 = pl.multiple_of(u * drows, drows)
        wdb[pl.ds(d0, drows), :] = wd_ref[pl.ds(d0, drows), :].astype(BF16)

    @pl.when(s == 0)
    def _():
        for u in range(n_units):
            stage_unit(u, stages[0])

    done = (pos_ref[s] + 1) * FFN_UNITS_PER_STEP

    def block(cur, nxt):
        @pl.when(used)
        def _():
            u0 = jnp.minimum(pos_ref[s] * FFN_UNITS_PER_STEP, n_units - FFN_UNITS_PER_STEP)
            for j in range(FFN_UNITS_PER_STEP):
                stage_unit(u0 + j, nxt)
            wg, wl, wdb = cur
            x = _load_row_tiles(x_ref, bm).astype(BF16)
            glu = _dot(x, wg[...]) + bg_ref[...]
            lin = _dot(x, wl[...]) + bl_ref[...]
            glu = jnp.minimum(glu, SWIGLU_LIMIT)
            lin = jnp.clip(lin, -SWIGLU_LIMIT, SWIGLU_LIMIT)
            act = glu * _sigmoid(SWIGLU_ALPHA * glu) * (lin + 1.0)
            _store_row_tiles(y_ref, _dot(act.astype(BF16), wdb[...]) + bd_ref[...])

        @pl.when(used & (last_ref[s] == 1) & (done < n_units))
        def _():
            def body(u, c):
                stage_unit(u, nxt)
                return c
            lax.fori_loop(done, n_units, body, 0)

    @pl.when(slot == 0)
    def _():
        block(stages[0], stages[1])

    @pl.when(slot == 1)
    def _():
        block(stages[1], stages[0])

    @pl.when(jnp.logical_not(used) & (s > 0))
    def _():
        y_ref[...] = jnp.zeros_like(y_ref)


def _ffn_schedule(block_expert, n_used):
    n = block_expert.shape[0]
    idx = jnp.arange(n, dtype=I32)
    be = block_expert
    first = jnp.concatenate([jnp.ones((1,), bool), be[1:] != be[:-1]])
    run_start = lax.cummax(jnp.where(first, idx, 0))
    ordinal = jnp.cumsum(first.astype(I32)) - 1
    is_last = jnp.concatenate([first[1:], jnp.ones((1,), bool)])
    next_first = lax.cummin(jnp.where(first, idx, n), reverse=True)
    next_start = jnp.concatenate([next_first[1:], jnp.full((1,), n, I32)])
    next_e = be[jnp.minimum(next_start, n - 1)]
    blk = jnp.maximum(jnp.arange(n + 1, dtype=I32) - 1, 0)
    step = jnp.arange(n + 1, dtype=I32)
    cur = be[blk]
    src = jnp.where(step == 0, be[0], next_e[blk])
    slot = ordinal[blk] % 2
    pos = blk - run_start[blk]
    last = is_last[blk].astype(I32)
    used = ((step >= 1) & (blk < n_used[0])).astype(I32)
    return [a.astype(I32) for a in (cur, src, slot, pos, last, used)]


def _ffn(block_expert, n_used, xbuf, w_up, bg, bl, w_down, bd):
    P = xbuf.shape[0] // ROW_TILE
    bm = MOE_BM
    _, D, De2 = w_up.shape
    De = De2 // 2
    assert (De2 // FFN_PREP_COLS) % FFN_UNITS_PER_STEP == 0
    sched = _ffn_schedule(block_expert, n_used)
    cc = np.arange(FFN_PREP_COLS)
    perm = (cc[:, None] == np.where(cc < FFN_PREP_COLS // 2, 2 * cc, 2 * (cc - FFN_PREP_COLS // 2) + 1)[None, :])
    rows = lambda s, *_: (jnp.maximum(s - 1, 0), 0)
    wmap = lambda s, cur, src, *_: (src[s], 0, 0)
    bmap = lambda s, cur, *_: (cur[s], 0, 0)
    grid_spec = pltpu.PrefetchScalarGridSpec(
        num_scalar_prefetch=len(sched),
        grid=(P // bm + 1,),
        in_specs=[pl.BlockSpec((bm * ROW_TILE, V7X_LANES), rows),
                  pl.BlockSpec((None, D, De2), wmap),
                  pl.BlockSpec((None, De, D), wmap),
                  pl.BlockSpec((None, 1, De), bmap),
                  pl.BlockSpec((None, 1, De), bmap),
                  pl.BlockSpec((None, 1, D), bmap),
                  pl.BlockSpec(perm.shape, lambda s, *_: (0, 0))],
        out_specs=pl.BlockSpec((bm * ROW_TILE, V7X_LANES), rows),
        scratch_shapes=[pltpu.VMEM((D, De), BF16), pltpu.VMEM((D, De), BF16), pltpu.VMEM((De, D), BF16)] * 2,
    )
    return pl.pallas_call(
        _ffn_kernel,
        grid_spec=grid_spec,
        out_shape=jax.ShapeDtypeStruct((P * ROW_TILE, V7X_LANES), F32),
        compiler_params=_cparams(("arbitrary",)),
        name="expert_ffn",
    )(*sched, xbuf, w_up, w_down, bg, bl, bd, jnp.asarray(perm, dtype=BF16))


def _combine_kernel(dcur_ref, dnext_ref, gate_ref, x1_ref, mod_ref, y_hbm, o_ref, buf, sem):
    i = pl.program_id(0)
    n = pl.num_programs(0)
    tm = x1_ref.shape[0]
    slot = i % 2

    def issue(dref, s):
        def body(tb, c):
            for u in range(DISPATCH_UNROLL):
                t = tb * DISPATCH_UNROLL + u
                for kk in range(TOP_K):
                    d = dref[t * TOP_K + kk]
                    pltpu.make_async_copy(_row(y_hbm, d), _row(buf.at[s, kk], t), sem.at[s]).start(priority=kk % 2)
            return c
        lax.fori_loop(0, tm // DISPATCH_UNROLL, body, 0)

    @pl.when(i == 0)
    def _():
        issue(dcur_ref, 0)

    @pl.when(i + 1 < n)
    def _():
        issue(dnext_ref, 1 - slot)

    def drain(tb, c):
        for _ in range(DISPATCH_UNROLL * TOP_K):
            pltpu.make_async_copy(_row(y_hbm, 0), _row(buf.at[slot, 0], 0), sem.at[slot]).wait()
        return c

    lax.fori_loop(0, tm // DISPATCH_UNROLL, drain, 0)
    gate = gate_ref[...]
    acc = gate[:, 0:1] * _load_row_tiles(buf.at[slot, 0], tm)
    for kk in range(1, TOP_K):
        acc = acc + gate[:, kk:kk + 1] * _load_row_tiles(buf.at[slot, kk], tm)
    o_ref[...] = x1_ref[...] + mod_ref[5:6, :] * acc


def _combine(dest_flat, gate, x1, mod3, ybuf, S):
    T, D = x1.shape
    tm = TM_COMBINE
    per_b = S // tm
    nt = T // tm
    return pl.pallas_call(
        _combine_kernel,
        grid=(nt,),
        in_specs=[pl.BlockSpec((tm * TOP_K,), lambda i: (i,), memory_space=pltpu.SMEM),
                  pl.BlockSpec((tm * TOP_K,), lambda i: (jnp.minimum(i + 1, nt - 1),), memory_space=pltpu.SMEM),
                  pl.BlockSpec((tm, TOP_K), lambda i: (i, 0)),
                  pl.BlockSpec((tm, D), lambda i: (i, 0)),
                  pl.BlockSpec((None, 6, D), lambda i: (i // per_b, 0, 0)),
                  pl.BlockSpec(memory_space=pl.ANY)],
        out_specs=pl.BlockSpec((tm, D), lambda i: (i, 0)),
        out_shape=jax.ShapeDtypeStruct((T, D), F32),
        scratch_shapes=[pltpu.VMEM((2, TOP_K, tm * ROW_TILE, V7X_LANES), F32), pltpu.SemaphoreType.DMA((2,))],
        compiler_params=_cparams(("arbitrary",)),
        name="combine",
    )(dest_flat, dest_flat, gate, x1, mod3, ybuf)


def kernel(x, c, positions, w_ada, b_ada, norm1_gain, w_in, lower_bound_logits, hg_norm_gain, w_hg_branch,
           q_norm_gain, k_norm_gain, attn_sinks, w_attn_branch, w_out, norm2_gain, w_router, b_router,
           w_up, b_up, w_down, b_down):
    B, S, D = x.shape
    T = B * S
    assert w_ada.shape[0] == 1, "one layer"
    x2 = x.reshape(T, D)

    mod = _ada(c, w_ada[0], b_ada[0])
    mod3 = mod.reshape(B, 6, D)

    hg_in, at_in, gates, f_min = _inproj(x2, mod3, norm1_gain[0], w_in[0].astype(BF16), S)
    f_min = f_min[::V7X_SUBLANES, 0].reshape(B, S // TM_PROJ)
    hg_o = _hgrn(hg_in, f_min, lower_bound_logits, hg_norm_gain[0], B, S)
    at_o = _swa(at_in, positions, q_norm_gain[0], k_norm_gain[0], attn_sinks[0], B, S)

    x1, h2, idx, gate, rank, cnt = _merge_router(
        x2, hg_o, at_o, gates, mod3, w_hg_branch[0].astype(BF16), w_attn_branch[0].astype(BF16),
        w_out[0].astype(BF16), norm2_gain[0], w_router[0].astype(BF16), b_router[0], S)

    bm = MOE_BM
    counts = cnt[0].astype(I32)
    padded = (counts + bm - 1) // bm * bm
    pad_end = jnp.cumsum(padded)
    pad_start = pad_end - padded
    P = T * TOP_K + N_EXPERTS * bm
    n_blocks = P // bm
    block_start = jnp.arange(n_blocks, dtype=I32) * bm
    block_expert = jnp.minimum(
        jnp.sum((pad_end[None, :] <= block_start[:, None]).astype(I32), axis=1), N_EXPERTS - 1).astype(I32)
    n_used = (pad_end[-1:] // bm).astype(I32)
    tail = jnp.concatenate([n_used, n_blocks - n_used]).astype(I32)

    dest = _dest(idx, rank, pad_start)
    dest_flat = dest.reshape(T * TOP_K)
    xbuf = _dispatch((pad_start + counts).astype(I32), (padded - counts).astype(I32), tail, dest_flat, h2, P)

    ybuf = _ffn(block_expert, n_used, xbuf, w_up[0],
                b_up[0][:, None, 0::2], b_up[0][:, None, 1::2],
                w_down[0], b_down[0][:, None, :])

    out = _combine(dest_flat, gate, x1, mod3, ybuf, S)
    return out.reshape(B, S, D)
```

```python
import functools

import numpy as np
import jax
import jax.numpy as jnp
from jax import lax
from jax.experimental import pallas as pl
from jax.experimental.pallas import tpu as pltpu

F32 = jnp.float32
BF16 = jnp.bfloat16
I32 = jnp.int32

HG_HEADS = 4
HG_DIM = 128
HG_W = HG_HEADS * HG_DIM
ATT_Q_HEADS = 8
ATT_KV_HEADS = 2
ATT_GROUP = ATT_Q_HEADS // ATT_KV_HEADS
ATT_DH = 64
ATT_Q_W = ATT_Q_HEADS * ATT_DH
ATT_KV_W = ATT_KV_HEADS * ATT_DH
ATT_BLOCK = 128
ROPE_THETA = 500000.0
ROPE_DIM = ATT_DH // 4
N_EXPERTS = 32
TOP_K = 4
SWIGLU_ALPHA = 1.702
SWIGLU_LIMIT = 7.0
EPS = 1e-6

V7X_LANES = 128
V7X_SUBLANES = 8
V7X_VMEM_LIMIT_BYTES = 56 * 1024 * 1024

TM_PROJ = 256
TM_MERGE = 512
HG_TILE = 512
HG_CHUNK = 128
HG_SUB = 8
MOE_BM = 512
TM_DISPATCH = 512
TM_COMBINE = 512

NEG_INF = float("-inf")


def _cparams(sem, vmem=V7X_VMEM_LIMIT_BYTES):
    return pltpu.CompilerParams(dimension_semantics=sem, vmem_limit_bytes=vmem)


def _sigmoid(x):
    return 1.0 / (1.0 + jnp.exp(-x))


def _dot(a, b):
    return jnp.dot(a, b, preferred_element_type=F32)


def _dot_nt(a, b):
    return lax.dot_general(a, b, (((1,), (1,)), ((), ())), preferred_element_type=F32)


def _dot_tn(a, b):
    return lax.dot_general(a, b, (((0,), (0,)), ((), ())), preferred_element_type=F32)


ROW_TILE = V7X_SUBLANES


def _store_row_tiles(ref, val):
    rows = val.shape[0]
    for g in range(ROW_TILE):
        ref[pl.ds(g, rows, stride=ROW_TILE), :] = val[:, g * V7X_LANES:(g + 1) * V7X_LANES]


def _load_row_tiles(ref, rows):
    return jnp.concatenate([ref[pl.ds(g, rows, stride=ROW_TILE), :] for g in range(ROW_TILE)], axis=1)


def _dot_exact01(m01, x):
    hi = x.astype(BF16)
    r1 = x - hi.astype(F32)
    mid = r1.astype(BF16)
    lo = (r1 - mid.astype(F32)).astype(BF16)
    return _dot(m01, hi) + _dot(m01, mid) + _dot(m01, lo)


def _ada_kernel(c_ref, w_ref, b_ref, o_ref):
    c = c_ref[...]
    cond = c * _sigmoid(c)
    o_ref[...] = _dot(cond.astype(BF16), w_ref[...].astype(BF16)) + b_ref[...]


def _ada(c, w_ada, b_ada):
    B, D = c.shape
    N = w_ada.shape[1]
    tn = D
    return pl.pallas_call(
        _ada_kernel,
        grid=(N // tn,),
        in_specs=[pl.BlockSpec((B, D), lambda j: (0, 0)),
                  pl.BlockSpec((D, tn), lambda j: (0, j)),
                  pl.BlockSpec((1, tn), lambda j: (0, j))],
        out_specs=pl.BlockSpec((B, tn), lambda j: (0, j)),
        out_shape=jax.ShapeDtypeStruct((B, N), F32),
        compiler_params=_cparams(("parallel",)),
        name="ada",
    )(c, w_ada, b_ada.reshape(1, N))


def _inproj_kernel(x_ref, mod_ref, g_ref, w_ref, hg_ref, at_ref, gt_ref, fmin_ref):
    x = x_ref[...]
    ms = jnp.mean(x * x, axis=-1, keepdims=True)
    y = x * lax.rsqrt(ms + EPS) * g_ref[...]
    h = (y * (1.0 + mod_ref[1:2, :]) + mod_ref[0:1, :]).astype(BF16)
    n_hg = hg_ref.shape[1]
    n_at = at_ref.shape[1]
    n_gt = gt_ref.shape[1]
    hg = _dot(h, w_ref[:, 0:n_hg])
    hg_ref[...] = hg
    f_pre = hg[:, HG_W:2 * HG_W]
    f_min = jnp.min(jnp.min(f_pre, axis=-1, keepdims=True), axis=0, keepdims=True)
    fmin_ref[...] = jnp.broadcast_to(f_min, fmin_ref.shape)
    at_ref[...] = _dot(h, w_ref[:, n_hg:n_hg + n_at])
    gt_ref[...] = _dot(h, w_ref[:, n_hg + n_at:n_hg + n_at + n_gt])


def _inproj(x2, mod3, gain, w_in_bf, S):
    T, D = x2.shape
    tm = TM_PROJ
    n_hg = 4 * HG_W
    n_at = ATT_Q_W + 2 * ATT_KV_W
    n_gt = 2 * D
    assert w_in_bf.shape[1] == n_hg + n_at + n_gt
    per_b = S // tm
    return pl.pallas_call(
        _inproj_kernel,
        grid=(T // tm,),
        in_specs=[pl.BlockSpec((tm, D), lambda i: (i, 0)),
                  pl.BlockSpec((None, 6, D), lambda i: (i // per_b, 0, 0)),
                  pl.BlockSpec((1, D), lambda i: (0, 0)),
                  pl.BlockSpec(w_in_bf.shape, lambda i: (0, 0))],
        out_specs=[pl.BlockSpec((tm, n_hg), lambda i: (i, 0)),
                   pl.BlockSpec((tm, n_at), lambda i: (i, 0)),
                   pl.BlockSpec((tm, n_gt), lambda i: (i, 0)),
                   pl.BlockSpec((V7X_SUBLANES, V7X_LANES), lambda i: (i, 0))],
        out_shape=[jax.ShapeDtypeStruct((T, n_hg), F32),
                   jax.ShapeDtypeStruct((T, n_at), F32),
                   jax.ShapeDtypeStruct((T, n_gt), F32),
                   jax.ShapeDtypeStruct((T // tm * V7X_SUBLANES, V7X_LANES), F32)],
        compiler_params=_cparams(("parallel",)),
        name="inproj",
    )(x2, mod3, gain.reshape(1, D), w_in_bf)


def _hgrn_constants():
    C = HG_CHUNK
    tri = (np.arange(C)[None, :] <= np.arange(C)[:, None]).astype(np.float32)
    halves = []
    lh = C // 2
    while lh >= HG_SUB:
        halves.append(lh)
        lh //= 2
    code = np.zeros((C, C), np.int32)
    ii, jj = np.meshgrid(np.arange(C), np.arange(C), indexing="ij")
    for li, lh in enumerate(halves):
        sel = ((ii // (2 * lh)) == (jj // (2 * lh))) & ((ii % (2 * lh)) >= lh) & ((jj % (2 * lh)) < lh)
        code[sel] = li + 1
    return tri, code, halves


HG_MILD_FMIN = -7.0


def _hgrn_kernel(mild_ref, in_ref, lbl_ref, gain_ref, tri_ref, code_ref, wsum_ref, o_ref, st_ref, *, halves):
    C = HG_CHUNK
    W = HG_W
    n_chunks = in_ref.shape[1] // C
    step_is_mild = mild_ref[pl.program_id(0) * pl.num_programs(1) + pl.program_id(1)] == 1

    @pl.when(pl.program_id(1) == 0)
    def _():
        st_ref[...] = jnp.zeros_like(st_ref)

    l0 = lbl_ref[0:1, :]
    l1 = lbl_ref[1:2, :]
    mx = jnp.maximum(l0, l1)
    e0 = jnp.exp(l0 - mx)
    lb = e0 / (e0 + jnp.exp(l1 - mx))
    gain = gain_ref[...]
    tri = tri_ref[...].astype(BF16)
    code = code_ref[...]
    wsum = wsum_ref[...]
    row = lax.broadcasted_iota(I32, (C, C), 0)
    col = lax.broadcasted_iota(I32, (C, C), 1)
    row_in_sub = lax.broadcasted_iota(I32, (C, HG_DIM), 0) % HG_SUB
    same_sub = (row // HG_SUB) == (col // HG_SUB)
    sub_causal = same_sub & (col <= row)

    def group_row(a, group, r):
        a3 = a.reshape(C // group, group, a.shape[1])
        return jnp.broadcast_to(a3[:, r:r + 1, :], a3.shape).reshape(a.shape)

    def bcast_sub(a, j):
        return group_row(a, HG_SUB, j)

    def chunk(mild, c, carry):
        r0 = pl.multiple_of(c * C, C)
        rows = pl.ds(r0, C)
        for bb in range(in_ref.shape[0]):
            qp = in_ref[bb, rows, 0:W]
            fp = in_ref[bb, rows, W:2 * W]
            vv = in_ref[bb, rows, 2 * W:3 * W]
            gp = in_ref[bb, rows, 3 * W:4 * W]
            forget = lb + (1.0 - lb) * _sigmoid(fp)
            q = qp * _sigmoid(qp)
            k = 1.0 - forget
            lf = jnp.log(forget)
            b = _dot_exact01(tri, lf)
            e_b = jnp.exp(b)
            e_st = jnp.exp(group_row(b, C, C - 1) - b)
            e_lv = [jnp.exp(-jnp.abs(b - group_row(b, 2 * lh, lh - 1))) for lh in halves]
            if mild:
                x_sub = group_row(b, HG_SUB, HG_SUB - 1) - b
                e_subk = jnp.exp(x_sub)
                e_subq = jnp.exp(-x_sub)
            outs = []
            for h in range(HG_HEADS):
                ls = slice(h * HG_DIM, (h + 1) * HG_DIM)
                qh, kh, vh = q[:, ls], k[:, ls], vv[:, ls]
                bh = b[:, ls]
                st = st_ref[bb, h]
                o = _dot_nt((qh * e_b[:, ls]).astype(BF16), st.astype(BF16))
                if mild:
                    s = _dot_nt((qh * e_subq[:, ls]).astype(BF16), (kh * e_subk[:, ls]).astype(BF16))
                    scores = jnp.where(sub_causal, s, 0.0)
                else:
                    ps = []
                    for j in range(HG_SUB):
                        d = jnp.where(row_in_sub >= j, bh - bcast_sub(bh, j), NEG_INF)
                        ps.append((qh * bcast_sub(kh, j) * jnp.exp(d)).astype(BF16))
                    scores = jnp.where(same_sub, _dot(jnp.concatenate(ps, axis=1), wsum), 0.0)
                for li in range(len(halves)):
                    e = e_lv[li][:, ls]
                    s = _dot_nt((qh * e).astype(BF16), (kh * e).astype(BF16))
                    scores = jnp.where(code == li + 1, s, scores)
                o = o + _dot(scores.astype(BF16), vh.astype(BF16))
                kst = (kh * e_st[:, ls]).astype(BF16)
                st_ref[bb, h] = st * e_b[C - 1:C, ls] + _dot_tn(vh.astype(BF16), kst)
                ms = jnp.mean(o * o, axis=-1, keepdims=True)
                on = o * lax.rsqrt(ms + EPS) * gain
                gh = gp[:, ls]
                outs.append(on * (gh * _sigmoid(gh)))
            o_ref[bb, rows, :] = jnp.concatenate(outs, axis=-1).astype(o_ref.dtype)
        return carry

    @pl.when(step_is_mild)
    def _():
        lax.fori_loop(0, n_chunks, functools.partial(chunk, True), 0)

    @pl.when(jnp.logical_not(step_is_mild))
    def _():
        lax.fori_loop(0, n_chunks, functools.partial(chunk, False), 0)


HG_SEQS = 4


def _hgrn(hg_in, f_min, lb_logits, gain, B, S):
    T = hg_in.shape[0]
    lt = HG_TILE
    tri, code, halves = _hgrn_constants()
    nseq = HG_SEQS if B % HG_SEQS == 0 else 1
    step_min = jnp.min(f_min.reshape(B // nseq, nseq, S // lt, lt // TM_PROJ), axis=(1, 3))
    mild = (step_min >= HG_MILD_FMIN).astype(I32).reshape(-1)
    wsum = (np.arange(HG_SUB * HG_DIM)[:, None] // HG_DIM == np.arange(HG_CHUNK)[None, :] % HG_SUB).astype(np.float32)
    const = lambda b, s, m: (0, 0)
    grid_spec = pltpu.PrefetchScalarGridSpec(
        num_scalar_prefetch=1,
        grid=(B // nseq, S // lt),
        in_specs=[pl.BlockSpec((nseq, lt, 4 * HG_W), lambda b, s, m: (b, s, 0)),
                  pl.BlockSpec(lb_logits.shape, const),
                  pl.BlockSpec((1, HG_DIM), const),
                  pl.BlockSpec(tri.shape, const),
                  pl.BlockSpec(code.shape, const),
                  pl.BlockSpec(wsum.shape, const)],
        out_specs=pl.BlockSpec((nseq, lt, HG_W), lambda b, s, m: (b, s, 0)),
        scratch_shapes=[pltpu.VMEM((nseq, HG_HEADS, HG_DIM, HG_DIM), F32)],
    )
    out = pl.pallas_call(
        functools.partial(_hgrn_kernel, halves=tuple(halves)),
        grid_spec=grid_spec,
        out_shape=jax.ShapeDtypeStruct((B, S, HG_W), BF16),
        compiler_params=_cparams(("parallel", "arbitrary")),
        name="hgrn2",
    )(mild, hg_in.reshape(B, S, 4 * HG_W), lb_logits, gain.reshape(1, HG_DIM), jnp.asarray(tri),
      jnp.asarray(code), jnp.asarray(wsum, dtype=BF16))
    return out.reshape(T, HG_W)


SWA_QBLOCKS = 8
ROPE_ROWS = 16


def _swa_constants():
    lane = np.arange(V7X_LANES)
    c = lane % ATT_DH
    half = ROPE_DIM // 2
    inv_freq = ROPE_THETA ** (-jnp.arange(half, dtype=F32) / half)
    freq_rows = jnp.broadcast_to(
        jnp.concatenate([inv_freq, jnp.zeros((ROPE_ROWS - half,), F32)])[:, None], (ROPE_ROWS, V7X_LANES))
    sel = ((np.arange(ROPE_ROWS)[:, None] == (c % half)[None, :]) & (c < ROPE_DIM)[None, :]).astype(np.float32)
    sign = np.where(c < half, -1.0, np.where(c < ROPE_DIM, 1.0, 0.0)).astype(np.float32)
    first = (c < half).astype(np.float32)
    tab = np.stack([sign, first], axis=0)
    g = (lane[:, None] // ATT_DH == lane[None, :] // ATT_DH).astype(np.float32) / ATT_DH
    return freq_rows, sel, tab, g


def _swa_kernel(sink_ref, cur_ref, prev_ref, pcur_ref, pprev_ref, qg_ref, kg_ref, freq_ref, sel_ref, tab_ref, g_ref,
                o_ref):
    Bq = ATT_BLOCK
    n = pl.program_id(1)
    tab = tab_ref[...]
    sign, first = tab[0:1, :], tab[1:2, :]
    gmat = g_ref[...].astype(BF16)
    sel = sel_ref[...].astype(BF16)
    freq_rows = freq_ref[...]
    half = ROPE_DIM // 2
    scale = ATT_DH ** -0.5

    def group_ms(x):
        sq = x * x
        hi = sq.astype(BF16)
        lo = (sq - hi.astype(F32)).astype(BF16)
        return _dot(hi, gmat) + _dot(lo, gmat)

    def spread(a):
        hi = a.astype(BF16)
        r1 = a - hi.astype(F32)
        mid = r1.astype(BF16)
        lo = (r1 - mid.astype(F32)).astype(BF16)
        return _dot_tn(hi, sel) + _dot_tn(mid, sel) + _dot_tn(lo, sel)

    def rope_table(pos_row):
        ang = freq_rows * pos_row
        return 1.0 + spread(jnp.cos(ang) - 1.0), spread(jnp.sin(ang)) * sign

    def norm_rope(x, gain, cs_sn):
        y = x * lax.rsqrt(group_ms(x) + EPS) * gain
        partner = jnp.where(first > 0.5, pltpu.roll(y, V7X_LANES - half, 1), pltpu.roll(y, half, 1))
        return y * cs_sn[0] + partner * cs_sn[1]

    kq = ATT_Q_W
    lane = lax.broadcasted_iota(I32, (Bq, V7X_LANES), 1)
    lo_half = lane < ATT_DH

    def pad_variants(a):
        r = pltpu.roll(a, ATT_DH, 1)
        z = jnp.zeros_like(a)
        return [[jnp.where(lo_half, a, z).astype(BF16), jnp.where(lo_half, z, r).astype(BF16)],
                [jnp.where(lo_half, r, z).astype(BF16), jnp.where(lo_half, z, a).astype(BF16)]]

    tables = [rope_table(pprev_ref[0].astype(F32))]
    kblocks = [pad_variants(norm_rope(prev_ref[:, 0:ATT_KV_W], kg_ref[...], tables[0]))]
    vblocks = [pad_variants(prev_ref[:, ATT_KV_W:2 * ATT_KV_W])]
    for j in range(SWA_QBLOCKS):
        rows = slice(j * Bq, (j + 1) * Bq)
        tables.append(rope_table(pcur_ref[j].astype(F32)))
        kblocks.append(pad_variants(norm_rope(cur_ref[rows, kq:kq + ATT_KV_W], kg_ref[...], tables[j + 1])))
        vblocks.append(pad_variants(cur_ref[rows, kq + ATT_KV_W:kq + 2 * ATT_KV_W]))

    qi = lax.broadcasted_iota(I32, (Bq, 2 * Bq), 0)
    kj = lax.broadcasted_iota(I32, (Bq, 2 * Bq), 1)
    in_band = ((kj < Bq) & (kj > qi)) | ((kj >= Bq) & ((kj - Bq) <= qi))
    first_of_seq = (jnp.zeros((Bq, 2 * Bq), I32) + n) == 0
    for j in range(SWA_QBLOCKS):
        rows = slice(j * Bq, (j + 1) * Bq)
        mask = (in_band & jnp.logical_not(first_of_seq & (kj < Bq))) if j == 0 else in_band
        for t in range(ATT_Q_W // V7X_LANES):
            ls = slice(t * V7X_LANES, (t + 1) * V7X_LANES)
            qt = (norm_rope(cur_ref[rows, ls], qg_ref[...], tables[j + 1]) * scale).astype(BF16)
            acc = jnp.zeros((Bq, V7X_LANES), F32)
            for u in range(2):
                head = 2 * t + u
                kvh = head // ATT_GROUP
                kcat = jnp.concatenate([kblocks[j][kvh][u], kblocks[j + 1][kvh][u]], axis=0)
                vcat = jnp.concatenate([vblocks[j][kvh][u], vblocks[j + 1][kvh][u]], axis=0)
                s = jnp.where(mask, _dot_nt(qt, kcat), NEG_INF)
                sink = sink_ref[head]
                m = jnp.maximum(jnp.max(s, axis=-1, keepdims=True), sink)
                p = jnp.exp(s - m)
                denom = jnp.sum(p, axis=-1, keepdims=True) + jnp.exp(sink - m)
                acc = acc + _dot(p.astype(BF16), vcat) * (1.0 / denom)
            o_ref[rows, ls] = acc.astype(o_ref.dtype)


def _swa(at_in, positions, q_gain, k_gain, sinks, B, S):
    T = at_in.shape[0]
    nb = S // ATT_BLOCK
    qb = SWA_QBLOCKS
    assert nb % qb == 0
    steps = nb // qb
    freq_rows, sel, tab, g = _swa_constants()
    qg = jnp.tile(q_gain.reshape(1, ATT_DH), (1, V7X_LANES // ATT_DH))
    kg = jnp.tile(k_gain.reshape(1, ATT_DH), (1, V7X_LANES // ATT_DH))
    pos3 = positions.reshape(B * nb, 1, ATT_BLOCK)
    n_at = at_in.shape[1]
    kv_blk = 2 * ATT_KV_W
    assert ATT_Q_W % kv_blk == 0
    prev_blk = lambda b, n: b * nb + jnp.maximum(qb * n - 1, 0)
    const = lambda b, n: (0, 0)
    return pl.pallas_call(
        _swa_kernel,
        grid=(B, steps),
        in_specs=[pl.BlockSpec(memory_space=pltpu.SMEM),
                  pl.BlockSpec((qb * ATT_BLOCK, n_at), lambda b, n: (b * steps + n, 0)),
                  pl.BlockSpec((ATT_BLOCK, kv_blk), lambda b, n: (prev_blk(b, n), ATT_Q_W // kv_blk)),
                  pl.BlockSpec((qb, 1, ATT_BLOCK), lambda b, n: (b * steps + n, 0, 0)),
                  pl.BlockSpec((1, 1, ATT_BLOCK), lambda b, n: (prev_blk(b, n), 0, 0)),
                  pl.BlockSpec((1, V7X_LANES), const),
                  pl.BlockSpec((1, V7X_LANES), const),
                  pl.BlockSpec(freq_rows.shape, const),
                  pl.BlockSpec(sel.shape, const),
                  pl.BlockSpec(tab.shape, const),
                  pl.BlockSpec(g.shape, const)],
        out_specs=pl.BlockSpec((qb * ATT_BLOCK, ATT_Q_W), lambda b, n: (b * steps + n, 0)),
        out_shape=jax.ShapeDtypeStruct((T, ATT_Q_W), BF16),
        compiler_params=_cparams(("parallel", "parallel")),
        name="swa",
    )(sinks, at_in, at_in, pos3, pos3, qg, kg, freq_rows, jnp.asarray(sel), jnp.asarray(tab), jnp.asarray(g))


def _merge_router_kernel(x_ref, hg_ref, at_ref, gt_ref, mod_ref, whg_ref, wat_ref, wout_ref, g2_ref,
                         wr_ref, br_ref, tri_ref,
                         x1_ref, h2_ref, idx_ref, gate_ref, rank_ref, cnt_ref, run_ref):
    i = pl.program_id(0)
    D = x_ref.shape[1]
    tm = x_ref.shape[0]

    @pl.when(i == 0)
    def _():
        run_ref[...] = jnp.zeros_like(run_ref)

    y_h = _dot(hg_ref[...], whg_ref[...])
    y_a = _dot(at_ref[...], wat_ref[...])
    merged = _sigmoid(gt_ref[:, 0:D]) * y_h + _sigmoid(gt_ref[:, D:2 * D]) * y_a
    x1 = x_ref[...] + mod_ref[2:3, :] * _dot(merged.astype(BF16), wout_ref[...])
    x1_ref[...] = x1
    ms = jnp.mean(x1 * x1, axis=-1, keepdims=True)
    h2 = x1 * lax.rsqrt(ms + EPS) * g2_ref[...] * (1.0 + mod_ref[4:5, :]) + mod_ref[3:4, :]
    _store_row_tiles(h2_ref, h2)
    logits = _dot(h2.astype(BF16), wr_ref[...]) + br_ref[...]
    E = logits.shape[1]
    lane = lax.broadcasted_iota(I32, (tm, E), 1).astype(F32)
    vals, idxs = [], []
    l = logits
    for _ in range(TOP_K):
        m = jnp.max(l, axis=-1, keepdims=True)
        ik = jnp.min(jnp.where(l == m, lane, float(E)), axis=-1, keepdims=True)
        vals.append(m)
        idxs.append(ik)
        l = jnp.where(lane == ik, NEG_INF, l)
    ex = [jnp.exp(v - vals[0]) for v in vals]
    den = ex[0]
    for e in ex[1:]:
        den = den + e
    onehot = jnp.zeros((tm, E), F32)
    for ik in idxs:
        onehot = onehot + (lane == ik).astype(F32)
    cum = _dot(tri_ref[...], onehot.astype(BF16))
    run = run_ref[0:1, 0:E]
    excl = cum - onehot + run
    lane_k = lax.broadcasted_iota(I32, (tm, TOP_K), 1)
    idx_o = jnp.zeros((tm, TOP_K), I32)
    gate_o = jnp.zeros((tm, TOP_K), F32)
    rank_o = jnp.zeros((tm, TOP_K), I32)
    for kk in range(TOP_K):
        rk = jnp.sum(jnp.where(lane == idxs[kk], excl, 0.0), axis=-1, keepdims=True)
        idx_o = jnp.where(lane_k == kk, idxs[kk].astype(I32), idx_o)
        gate_o = jnp.where(lane_k == kk, ex[kk] / den, gate_o)
        rank_o = jnp.where(lane_k == kk, rk.astype(I32), rank_o)
    idx_ref[...] = idx_o
    gate_ref[...] = gate_o
    rank_ref[...] = rank_o
    new_run = run + cum[tm - 1:tm, :]
    run_ref[0:1, 0:E] = new_run
    cnt_ref[...] = jnp.broadcast_to(new_run, cnt_ref.shape)


def _merge_router(x2, hg_o, at_o, gates, mod3, whg, wat, wout, g2, wr, br, S):
    T, D = x2.shape
    tm = TM_MERGE
    per_b = S // tm
    E = wr.shape[1]
    tri = jnp.asarray(np.tril(np.ones((tm, tm), np.float32)), dtype=BF16)
    row = lambda i: (i, 0)
    const = lambda i: (0, 0)
    return pl.pallas_call(
        _merge_router_kernel,
        grid=(T // tm,),
        in_specs=[pl.BlockSpec((tm, D), row),
                  pl.BlockSpec((tm, HG_W), row),
                  pl.BlockSpec((tm, ATT_Q_W), row),
                  pl.BlockSpec((tm, 2 * D), row),
                  pl.BlockSpec((None, 6, D), lambda i: (i // per_b, 0, 0)),
                  pl.BlockSpec(whg.shape, const),
                  pl.BlockSpec(wat.shape, const),
                  pl.BlockSpec(wout.shape, const),
                  pl.BlockSpec((1, D), const),
                  pl.BlockSpec(wr.shape, const),
                  pl.BlockSpec((1, E), const),
                  pl.BlockSpec((tm, tm), const)],
        out_specs=[pl.BlockSpec((tm, D), row),
                   pl.BlockSpec((tm * ROW_TILE, V7X_LANES), row),
                   pl.BlockSpec((tm, TOP_K), row),
                   pl.BlockSpec((tm, TOP_K), row),
                   pl.BlockSpec((tm, TOP_K), row),
                   pl.BlockSpec((V7X_SUBLANES, E), const)],
        out_shape=[jax.ShapeDtypeStruct((T, D), F32),
                   jax.ShapeDtypeStruct((T * ROW_TILE, V7X_LANES), F32),
                   jax.ShapeDtypeStruct((T, TOP_K), I32),
                   jax.ShapeDtypeStruct((T, TOP_K), F32),
                   jax.ShapeDtypeStruct((T, TOP_K), I32),
                   jax.ShapeDtypeStruct((V7X_SUBLANES, E), F32)],
        scratch_shapes=[pltpu.VMEM((V7X_SUBLANES, V7X_LANES), F32)],
        compiler_params=_cparams(("arbitrary",)),
        name="merge_router",
    )(x2, hg_o, at_o, gates, mod3, whg, wat, wout, g2.reshape(1, D), wr, br.reshape(1, E), tri)


def _dest_kernel(idx_ref, rank_ref, tab_ref, o_ref):
    idx = idx_ref[...]
    tm = idx.shape[0]
    E = tab_ref.shape[0] // TOP_K
    lane = lax.broadcasted_iota(I32, (tm, tab_ref.shape[0]), 1)
    hot = lane == idx[:, 0:1]
    for kk in range(1, TOP_K):
        hot = hot | (lane == idx[:, kk:kk + 1] + kk * E)
    start = _dot_exact01(hot.astype(F32).astype(BF16), tab_ref[...])
    o_ref[...] = rank_ref[...] + start[:, 0:TOP_K].astype(I32)


def _dest(idx, rank, pad_start):
    T = idx.shape[0]
    tm = 1024
    E = pad_start.shape[0]
    assert TOP_K * E == V7X_LANES
    k_of = np.arange(TOP_K * E) // E
    tab = jnp.where(k_of[:, None] == np.arange(V7X_LANES)[None, :], jnp.tile(pad_start, TOP_K)[:, None], 0).astype(F32)
    row = lambda i: (i, 0)
    return pl.pallas_call(
        _dest_kernel,
        grid=(T // tm,),
        in_specs=[pl.BlockSpec((tm, TOP_K), row), pl.BlockSpec((tm, TOP_K), row),
                  pl.BlockSpec(tab.shape, lambda i: (0, 0))],
        out_specs=pl.BlockSpec((tm, TOP_K), row),
        out_shape=jax.ShapeDtypeStruct((T, TOP_K), I32),
        compiler_params=_cparams(("parallel",)),
        name="dest",
    )(idx, rank, tab)


DISPATCH_UNROLL = 4


def _row(ref, r):
    return ref.at[pl.ds(pl.multiple_of(r * ROW_TILE, ROW_TILE), ROW_TILE)]


def _dispatch_kernel(fill_start_ref, fill_n_ref, tail_ref, dest_ref, h_ref, xbuf_ref, zero_ref, sem):
    tm = h_ref.shape[0] // ROW_TILE
    zrows = zero_ref.shape[0]

    def zero_rows_copy(r, n):
        src = zero_ref.at[pl.ds(0, n * ROW_TILE)]
        dst = xbuf_ref.at[pl.ds(pl.multiple_of(r * ROW_TILE, ROW_TILE), n * ROW_TILE)]
        return pltpu.make_async_copy(src, dst, sem.at[1])

    def zero_block_copy(b):
        dst = xbuf_ref.at[pl.ds(pl.multiple_of(b * zrows, zrows), zrows)]
        return pltpu.make_async_copy(zero_ref, dst, sem.at[2])

    @pl.when(pl.program_id(0) == 0)
    def _():
        zero_ref[...] = jnp.zeros_like(zero_ref)

        def per_expert(start):
            def body(e, c):
                n = fill_n_ref[e]
                r = fill_start_ref[e]
                size = zrows // ROW_TILE // 2
                while size >= 1:
                    @pl.when((n & size) != 0)
                    def _(r=r, size=size):
                        cp = zero_rows_copy(r, size)
                        cp.start() if start else cp.wait()
                    r = r + (n & size)
                    size //= 2
                return c
            return body

        def tail(start):
            def body(b, c):
                cp = zero_block_copy(tail_ref[0] + b)
                cp.start() if start else cp.wait()
                return c
            return body

        lax.fori_loop(0, N_EXPERTS, per_expert(True), 0)
        lax.fori_loop(0, tail_ref[1], tail(True), 0)
        lax.fori_loop(0, N_EXPERTS, per_expert(False), 0)
        lax.fori_loop(0, tail_ref[1], tail(False), 0)

    def issue(tb, c):
        for u in range(DISPATCH_UNROLL):
            t = tb * DISPATCH_UNROLL + u
            src = _row(h_ref, t)
            for kk in range(TOP_K):
                d = dest_ref[t * TOP_K + kk]
                pltpu.make_async_copy(src, _row(xbuf_ref, d), sem.at[0]).start(priority=kk % 2)
        return c

    lax.fori_loop(0, tm // DISPATCH_UNROLL, issue, 0)

    def drain(tb, c):
        for _ in range(DISPATCH_UNROLL * TOP_K):
            pltpu.make_async_copy(_row(h_ref, 0), _row(xbuf_ref, 0), sem.at[0]).wait()
        return c

    lax.fori_loop(0, tm // DISPATCH_UNROLL, drain, 0)


def _dispatch(fill_start, fill_n, tail, dest_flat, h2t, P):
    T = h2t.shape[0] // ROW_TILE
    tm = TM_DISPATCH
    grid_spec = pltpu.PrefetchScalarGridSpec(
        num_scalar_prefetch=3,
        grid=(T // tm,),
        in_specs=[pl.BlockSpec((tm * TOP_K,), lambda i, *_: (i,), memory_space=pltpu.SMEM),
                  pl.BlockSpec((tm * ROW_TILE, V7X_LANES), lambda i, *_: (i, 0))],
        out_specs=pl.BlockSpec(memory_space=pl.ANY),
        scratch_shapes=[pltpu.VMEM((MOE_BM * ROW_TILE, V7X_LANES), F32), pltpu.SemaphoreType.DMA((3,))],
    )
    return pl.pallas_call(
        _dispatch_kernel,
        grid_spec=grid_spec,
        out_shape=jax.ShapeDtypeStruct((P * ROW_TILE, V7X_LANES), F32),
        compiler_params=_cparams(("arbitrary",)),
        name="dispatch",
    )(fill_start, fill_n, tail, dest_flat, h2t)


FFN_PREP_COLS = 256
FFN_UNITS_PER_STEP = 1


def _ffn_kernel(cur_ref, src_ref, slot_ref, pos_ref, last_ref, used_ref,
                x_ref, wu_ref, wd_ref, bg_ref, bl_ref, bd_ref, perm_ref, y_ref,
                wg0, wl0, wd0, wg1, wl1, wd1):
    del cur_ref, src_ref
    s = pl.program_id(0)
    bm = x_ref.shape[0] // ROW_TILE
    D, De2 = wu_ref.shape
    De = De2 // 2
    half = FFN_PREP_COLS // 2
    n_units = De2 // FFN_PREP_COLS
    drows = De // n_units
    slot = slot_ref[s]
    used = used_ref[s] == 1
    stages = ((wg0, wl0, wd0), (wg1, wl1, wd1))
    perm = perm_ref[...]

    def stage_unit(u, dst):
        wg, wl, wdb = dst
        c0 = pl.multiple_of(u * FFN_PREP_COLS, FFN_PREP_COLS)
        p = _dot(wu_ref[:, pl.ds(c0, FFN_PREP_COLS)].astype(BF16), perm)
        r0 = pl.multiple_of(u * half, half)
        wg[:, pl.ds(r0, half)] = p[:, 0:half].astype(BF16)
        wl[:, pl.ds(r0, half)] = p[:, half:2 * half].astype(BF16)
        d0 = pl.multiple_of(u * drows, drows)
        wdb[pl.ds(d0, drows), :] = wd_ref[pl.ds(d0, drows), :].astype(BF16)

    @pl.when(s == 0)
    def _():
        for u in range(n_units):
            stage_unit(u, stages[0])

    done = (pos_ref[s] + 1) * FFN_UNITS_PER_STEP

    def block(cur, nxt):
        @pl.when(used)
        def _():
            u0 = jnp.minimum(pos_ref[s] * FFN_UNITS_PER_STEP, n_units - FFN_UNITS_PER_STEP)
            for j in range(FFN_UNITS_PER_STEP):
                stage_unit(u0 + j, nxt)
            wg, wl, wdb = cur
            x = _load_row_tiles(x_ref, bm).astype(BF16)
            glu = _dot(x, wg[...]) + bg_ref[...]
            lin = _dot(x, wl[...]) + bl_ref[...]
            glu = jnp.minimum(glu, SWIGLU_LIMIT)
            lin = jnp.clip(lin, -SWIGLU_LIMIT, SWIGLU_LIMIT)
            act = glu * _sigmoid(SWIGLU_ALPHA * glu) * (lin + 1.0)
            _store_row_tiles(y_ref, _dot(act.astype(BF16), wdb[...]) + bd_ref[...])

        @pl.when(used & (last_ref[s] == 1) & (done < n_units))
        def _():
            def body(u, c):
                stage_unit(u, nxt)
                return c
            lax.fori_loop(done, n_units, body, 0)

    @pl.when(slot == 0)
    def _():
        block(stages[0], stages[1])

    @pl.when(slot == 1)
    def _():
        block(stages[1], stages[0])

    @pl.when(jnp.logical_not(used) & (s > 0))
    def _():
        y_ref[...] = jnp.zeros_like(y_ref)


def _ffn_schedule(block_expert, n_used):
    n = block_expert.shape[0]
    idx = jnp.arange(n, dtype=I32)
    be = block_expert
    first = jnp.concatenate([jnp.ones((1,), bool), be[1:] != be[:-1]])
    run_start = lax.cummax(jnp.where(first, idx, 0))
    ordinal = jnp.cumsum(first.astype(I32)) - 1
    is_last = jnp.concatenate([first[1:], jnp.ones((1,), bool)])
    next_first = lax.cummin(jnp.where(first, idx, n), reverse=True)
    next_start = jnp.concatenate([next_first[1:], jnp.full((1,), n, I32)])
    next_e = be[jnp.minimum(next_start, n - 1)]
    blk = jnp.maximum(jnp.arange(n + 1, dtype=I32) - 1, 0)
    step = jnp.arange(n + 1, dtype=I32)
    cur = be[blk]
    src = jnp.where(step == 0, be[0], next_e[blk])
    slot = ordinal[blk] % 2
    pos = blk - run_start[blk]
    last = is_last[blk].astype(I32)
    used = ((step >= 1) & (blk < n_used[0])).astype(I32)
    return [a.astype(I32) for a in (cur, src, slot, pos, last, used)]


def _ffn(block_expert, n_used, xbuf, w_up, bg, bl, w_down, bd):
    P = xbuf.shape[0] // ROW_TILE
    bm = MOE_BM
    _, D, De2 = w_up.shape
    De = De2 // 2
    assert (De2 // FFN_PREP_COLS) % FFN_UNITS_PER_STEP == 0
    sched = _ffn_schedule(block_expert, n_used)
    cc = np.arange(FFN_PREP_COLS)
    perm = (cc[:, None] == np.where(cc < FFN_PREP_COLS // 2, 2 * cc, 2 * (cc - FFN_PREP_COLS // 2) + 1)[None, :])
    rows = lambda s, *_: (jnp.maximum(s - 1, 0), 0)
    wmap = lambda s, cur, src, *_: (src[s], 0, 0)
    bmap = lambda s, cur, *_: (cur[s], 0, 0)
    grid_spec = pltpu.PrefetchScalarGridSpec(
        num_scalar_prefetch=len(sched),
        grid=(P // bm + 1,),
        in_specs=[pl.BlockSpec((bm * ROW_TILE, V7X_LANES), rows),
                  pl.BlockSpec((None, D, De2), wmap),
                  pl.BlockSpec((None, De, D), wmap),
                  pl.BlockSpec((None, 1, De), bmap),
                  pl.BlockSpec((None, 1, De), bmap),
                  pl.BlockSpec((None, 1, D), bmap),
                  pl.BlockSpec(perm.shape, lambda s, *_: (0, 0))],
        out_specs=pl.BlockSpec((bm * ROW_TILE, V7X_LANES), rows),
        scratch_shapes=[pltpu.VMEM((D, De), BF16), pltpu.VMEM((D, De), BF16), pltpu.VMEM((De, D), BF16)] * 2,
    )
    return pl.pallas_call(
        _ffn_kernel,
        grid_spec=grid_spec,
        out_shape=jax.ShapeDtypeStruct((P * ROW_TILE, V7X_LANES), F32),
        compiler_params=_cparams(("arbitrary",)),
        name="expert_ffn",
    )(*sched, xbuf, w_up, w_down, bg, bl, bd, jnp.asarray(perm, dtype=BF16))


def _combine_kernel(dcur_ref, dnext_ref, gate_ref, x1_ref, mod_ref, y_hbm, o_ref, buf, sem):
    i = pl.program_id(0)
    n = pl.num_programs(0)
    tm = x1_ref.shape[0]
    slot = i % 2

    def issue(dref, s):
        def body(tb, c):
            for u in range(DISPATCH_UNROLL):
                t = tb * DISPATCH_UNROLL + u
                for kk in range(TOP_K):
                    d = dref[t * TOP_K + kk]
                    pltpu.make_async_copy(_row(y_hbm, d), _row(buf.at[s, kk], t), sem.at[s]).start(priority=kk % 2)
            return c
        lax.fori_loop(0, tm // DISPATCH_UNROLL, body, 0)

    @pl.when(i == 0)
    def _():
        issue(dcur_ref, 0)

    @pl.when(i + 1 < n)
    def _():
        issue(dnext_ref, 1 - slot)

    def drain(tb, c):
        for _ in range(DISPATCH_UNROLL * TOP_K):
            pltpu.make_async_copy(_row(y_hbm, 0), _row(buf.at[slot, 0], 0), sem.at[slot]).wait()
        return c

    lax.fori_loop(0, tm // DISPATCH_UNROLL, drain, 0)
    gate = gate_ref[...]
    acc = gate[:, 0:1] * _load_row_tiles(buf.at[slot, 0], tm)
    for kk in range(1, TOP_K):
        acc = acc + gate[:, kk:kk + 1] * _load_row_tiles(buf.at[slot, kk], tm)
    o_ref[...] = x1_ref[...] + mod_ref[5:6, :] * acc


def _combine(dest_flat, gate, x1, mod3, ybuf, S):
    T, D = x1.shape
    tm = TM_COMBINE
    per_b = S // tm
    nt = T // tm
    return pl.pallas_call(
        _combine_kernel,
        grid=(nt,),
        in_specs=[pl.BlockSpec((tm * TOP_K,), lambda i: (i,), memory_space=pltpu.SMEM),
                  pl.BlockSpec((tm * TOP_K,), lambda i: (jnp.minimum(i + 1, nt - 1),), memory_space=pltpu.SMEM),
                  pl.BlockSpec((tm, TOP_K), lambda i: (i, 0)),
                  pl.BlockSpec((tm, D), lambda i: (i, 0)),
                  pl.BlockSpec((None, 6, D), lambda i: (i // per_b, 0, 0)),
                  pl.BlockSpec(memory_space=pl.ANY)],
        out_specs=pl.BlockSpec((tm, D), lambda i: (i, 0)),
        out_shape=jax.ShapeDtypeStruct((T, D), F32),
        scratch_shapes=[pltpu.VMEM((2, TOP_K, tm * ROW_TILE, V7X_LANES), F32), pltpu.SemaphoreType.DMA((2,))],
        compiler_params=_cparams(("arbitrary",)),
        name="combine",
    )(dest_flat, dest_flat, gate, x1, mod3, ybuf)


def kernel(x, c, positions, w_ada, b_ada, norm1_gain, w_in, lower_bound_logits, hg_norm_gain, w_hg_branch,
           q_norm_gain, k_norm_gain, attn_sinks, w_attn_branch, w_out, norm2_gain, w_router, b_router,
           w_up, b_up, w_down, b_down):
    B, S, D = x.shape
    T = B * S
    assert w_ada.shape[0] == 1, "one layer"
    x2 = x.reshape(T, D)

    mod = _ada(c, w_ada[0], b_ada[0])
    mod3 = mod.reshape(B, 6, D)

    hg_in, at_in, gates, f_min = _inproj(x2, mod3, norm1_gain[0], w_in[0].astype(BF16), S)
    f_min = f_min[::V7X_SUBLANES, 0].reshape(B, S // TM_PROJ)
    hg_o = _hgrn(hg_in, f_min, lower_bound_logits, hg_norm_gain[0], B, S)
    at_o = _swa(at_in, positions, q_norm_gain[0], k_norm_gain[0], attn_sinks[0], B, S)

    x1, h2, idx, gate, rank, cnt = _merge_router(
        x2, hg_o, at_o, gates, mod3, w_hg_branch[0].astype(BF16), w_attn_branch[0].astype(BF16),
        w_out[0].astype(BF16), norm2_gain[0], w_router[0].astype(BF16), b_router[0], S)

    bm = MOE_BM
    counts = cnt[0].astype(I32)
    padded = (counts + bm - 1) // bm * bm
    pad_end = jnp.cumsum(padded)
    pad_start = pad_end - padded
    P = T * TOP_K + N_EXPERTS * bm
    n_blocks = P // bm
    block_start = jnp.arange(n_blocks, dtype=I32) * bm
    block_expert = jnp.minimum(
        jnp.sum((pad_end[None, :] <= block_start[:, None]).astype(I32), axis=1), N_EXPERTS - 1).astype(I32)
    n_used = (pad_end[-1:] // bm).astype(I32)
    tail = jnp.concatenate([n_used, n_blocks - n_used]).astype(I32)

    dest = _dest(idx, rank, pad_start)
    dest_flat = dest.reshape(T * TOP_K)
    xbuf = _dispatch((pad_start + counts).astype(I32), (padded - counts).astype(I32), tail, dest_flat, h2, P)

    ybuf = _ffn(block_expert, n_used, xbuf, w_up[0],
                b_up[0][:, None, 0::2], b_up[0][:, None, 1::2],
                w_down[0], b_down[0][:, None, :])

    out = _combine(dest_flat, gate, x1, mod3, ybuf, S)
    return out.reshape(B, S, D)
```

```python
import functools

import numpy as np
import jax
import jax.numpy as jnp
from jax import lax
from jax.experimental import pallas as pl
from jax.experimental.pallas import tpu as pltpu

F32 = jnp.float32
BF16 = jnp.bfloat16
I32 = jnp.int32

HG_HEADS = 4
HG_DIM = 128
HG_W = HG_HEADS * HG_DIM
ATT_Q_HEADS = 8
ATT_KV_HEADS = 2
ATT_GROUP = ATT_Q_HEADS // ATT_KV_HEADS
ATT_DH = 64
ATT_Q_W = ATT_Q_HEADS * ATT_DH
ATT_KV_W = ATT_KV_HEADS * ATT_DH
ATT_BLOCK = 128
ROPE_THETA = 500000.0
ROPE_DIM = ATT_DH // 4
N_EXPERTS = 32
TOP_K = 4
SWIGLU_ALPHA = 1.702
SWIGLU_LIMIT = 7.0
EPS = 1e-6

V7X_LANES = 128
V7X_SUBLANES = 8
V7X_VMEM_LIMIT_BYTES = 56 * 1024 * 1024

TM_PROJ = 256
TM_MERGE = 512
HG_TILE = 512
HG_CHUNK = 128
HG_SUB = 8
MOE_BM = 512
TM_DISPATCH = 512
TM_COMBINE = 256

NEG_INF = float("-inf")


def _cparams(sem, vmem=V7X_VMEM_LIMIT_BYTES):
    return pltpu.CompilerParams(dimension_semantics=sem, vmem_limit_bytes=vmem)


def _sigmoid(x):
    return 1.0 / (1.0 + jnp.exp(-x))


def _dot(a, b):
    return jnp.dot(a, b, preferred_element_type=F32)


def _dot_nt(a, b):
    return lax.dot_general(a, b, (((1,), (1,)), ((), ())), preferred_element_type=F32)


def _dot_tn(a, b):
    return lax.dot_general(a, b, (((0,), (0,)), ((), ())), preferred_element_type=F32)


ROW_TILE = V7X_SUBLANES


def _store_row_tiles(ref, val):
    rows = val.shape[0]
    for g in range(ROW_TILE):
        ref[pl.ds(g, rows, stride=ROW_TILE), :] = val[:, g * V7X_LANES:(g + 1) * V7X_LANES]


def _load_row_tiles(ref, rows):
    return jnp.concatenate([ref[pl.ds(g, rows, stride=ROW_TILE), :] for g in range(ROW_TILE)], axis=1)


def _dot_exact01(m01, x):
    hi = x.astype(BF16)
    r1 = x - hi.astype(F32)
    mid = r1.astype(BF16)
    lo = (r1 - mid.astype(F32)).astype(BF16)
    return _dot(m01, hi) + _dot(m01, mid) + _dot(m01, lo)


def _ada_kernel(c_ref, w_ref, b_ref, o_ref):
    c = c_ref[...]
    cond = c * _sigmoid(c)
    o_ref[...] = _dot(cond.astype(BF16), w_ref[...].astype(BF16)) + b_ref[...]


def _ada(c, w_ada, b_ada):
    B, D = c.shape
    N = w_ada.shape[1]
    tn = D
    return pl.pallas_call(
        _ada_kernel,
        grid=(N // tn,),
        in_specs=[pl.BlockSpec((B, D), lambda j: (0, 0)),
                  pl.BlockSpec((D, tn), lambda j: (0, j)),
                  pl.BlockSpec((1, tn), lambda j: (0, j))],
        out_specs=pl.BlockSpec((B, tn), lambda j: (0, j)),
        out_shape=jax.ShapeDtypeStruct((B, N), F32),
        compiler_params=_cparams(("parallel",)),
        name="ada",
    )(c, w_ada, b_ada.reshape(1, N))


def _inproj_kernel(x_ref, mod_ref, g_ref, w_ref, hg_ref, at_ref, gt_ref, fmin_ref):
    x = x_ref[...]
    ms = jnp.mean(x * x, axis=-1, keepdims=True)
    y = x * lax.rsqrt(ms + EPS) * g_ref[...]
    h = (y * (1.0 + mod_ref[1:2, :]) + mod_ref[0:1, :]).astype(BF16)
    n_hg = hg_ref.shape[1]
    n_at = at_ref.shape[1]
    n_gt = gt_ref.shape[1]
    hg = _dot(h, w_ref[:, 0:n_hg])
    hg_ref[...] = hg
    f_pre = hg[:, HG_W:2 * HG_W]
    f_min = jnp.min(jnp.min(f_pre, axis=-1, keepdims=True), axis=0, keepdims=True)
    fmin_ref[...] = jnp.broadcast_to(f_min, fmin_ref.shape)
    at_ref[...] = _dot(h, w_ref[:, n_hg:n_hg + n_at])
    gt_ref[...] = _dot(h, w_ref[:, n_hg + n_at:n_hg + n_at + n_gt])


def _inproj(x2, mod3, gain, w_in_bf, S):
    T, D = x2.shape
    tm = TM_PROJ
    n_hg = 4 * HG_W
    n_at = ATT_Q_W + 2 * ATT_KV_W
    n_gt = 2 * D
    assert w_in_bf.shape[1] == n_hg + n_at + n_gt
    per_b = S // tm
    return pl.pallas_call(
        _inproj_kernel,
        grid=(T // tm,),
        in_specs=[pl.BlockSpec((tm, D), lambda i: (i, 0)),
                  pl.BlockSpec((None, 6, D), lambda i: (i // per_b, 0, 0)),
                  pl.BlockSpec((1, D), lambda i: (0, 0)),
                  pl.BlockSpec(w_in_bf.shape, lambda i: (0, 0))],
        out_specs=[pl.BlockSpec((tm, n_hg), lambda i: (i, 0)),
                   pl.BlockSpec((tm, n_at), lambda i: (i, 0)),
                   pl.BlockSpec((tm, n_gt), lambda i: (i, 0)),
                   pl.BlockSpec((V7X_SUBLANES, V7X_LANES), lambda i: (i, 0))],
        out_shape=[jax.ShapeDtypeStruct((T, n_hg), F32),
                   jax.ShapeDtypeStruct((T, n_at), F32),
                   jax.ShapeDtypeStruct((T, n_gt), F32),
                   jax.ShapeDtypeStruct((T // tm * V7X_SUBLANES, V7X_LANES), F32)],
        compiler_params=_cparams(("parallel",)),
        name="inproj",
    )(x2, mod3, gain.reshape(1, D), w_in_bf)


def _hgrn_constants():
    C = HG_CHUNK
    tri = (np.arange(C)[None, :] <= np.arange(C)[:, None]).astype(np.float32)
    halves = []
    lh = C // 2
    while lh >= HG_SUB:
        halves.append(lh)
        lh //= 2
    code = np.zeros((C, C), np.int32)
    ii, jj = np.meshgrid(np.arange(C), np.arange(C), indexing="ij")
    for li, lh in enumerate(halves):
        sel = ((ii // (2 * lh)) == (jj // (2 * lh))) & ((ii % (2 * lh)) >= lh) & ((jj % (2 * lh)) < lh)
        code[sel] = li + 1
    return tri, code, halves


HG_MILD_FMIN = -7.0


def _hgrn_kernel(mild_ref, in_ref, lbl_ref, gain_ref, tri_ref, code_ref, wsum_ref, o_ref, st_ref, *, halves):
    C = HG_CHUNK
    W = HG_W
    n_chunks = in_ref.shape[1] // C
    step_is_mild = mild_ref[pl.program_id(0) * pl.num_programs(1) + pl.program_id(1)] == 1

    @pl.when(pl.program_id(1) == 0)
    def _():
        st_ref[...] = jnp.zeros_like(st_ref)

    l0 = lbl_ref[0:1, :]
    l1 = lbl_ref[1:2, :]
    mx = jnp.maximum(l0, l1)
    e0 = jnp.exp(l0 - mx)
    lb = e0 / (e0 + jnp.exp(l1 - mx))
    gain = gain_ref[...]
    tri = tri_ref[...].astype(BF16)
    code = code_ref[...]
    wsum = wsum_ref[...]
    row = lax.broadcasted_iota(I32, (C, C), 0)
    col = lax.broadcasted_iota(I32, (C, C), 1)
    row_in_sub = lax.broadcasted_iota(I32, (C, HG_DIM), 0) % HG_SUB
    same_sub = (row // HG_SUB) == (col // HG_SUB)
    sub_causal = same_sub & (col <= row)

    def group_row(a, group, r):
        a3 = a.reshape(C // group, group, a.shape[1])
        return jnp.broadcast_to(a3[:, r:r + 1, :], a3.shape).reshape(a.shape)

    def bcast_sub(a, j):
        return group_row(a, HG_SUB, j)

    def chunk(mild, c, carry):
        r0 = pl.multiple_of(c * C, C)
        rows = pl.ds(r0, C)
        for bb in range(in_ref.shape[0]):
            qp = in_ref[bb, rows, 0:W]
            fp = in_ref[bb, rows, W:2 * W]
            vv = in_ref[bb, rows, 2 * W:3 * W]
            gp = in_ref[bb, rows, 3 * W:4 * W]
            forget = lb + (1.0 - lb) * _sigmoid(fp)
            q = qp * _sigmoid(qp)
            k = 1.0 - forget
            lf = jnp.log(forget)
            b = _dot_exact01(tri, lf)
            e_b = jnp.exp(b)
            e_st = jnp.exp(group_row(b, C, C - 1) - b)
            e_lv = [jnp.exp(-jnp.abs(b - group_row(b, 2 * lh, lh - 1))) for lh in halves]
            if mild:
                x_sub = group_row(b, HG_SUB, HG_SUB - 1) - b
                e_subk = jnp.exp(x_sub)
                e_subq = jnp.exp(-x_sub)
            outs = []
            for h in range(HG_HEADS):
                ls = slice(h * HG_DIM, (h + 1) * HG_DIM)
                qh, kh, vh = q[:, ls], k[:, ls], vv[:, ls]
                bh = b[:, ls]
                st = st_ref[bb, h]
                o = _dot_nt((qh * e_b[:, ls]).astype(BF16), st.astype(BF16))
                if mild:
                    s = _dot_nt((qh * e_subq[:, ls]).astype(BF16), (kh * e_subk[:, ls]).astype(BF16))
                    scores = jnp.where(sub_causal, s, 0.0)
                else:
                    ps = []
                    for j in range(HG_SUB):
                        d = jnp.where(row_in_sub >= j, bh - bcast_sub(bh, j), NEG_INF)
                        ps.append((qh * bcast_sub(kh, j) * jnp.exp(d)).astype(BF16))
                    scores = jnp.where(same_sub, _dot(jnp.concatenate(ps, axis=1), wsum), 0.0)
                for li in range(len(halves)):
                    e = e_lv[li][:, ls]
                    s = _dot_nt((qh * e).astype(BF16), (kh * e).astype(BF16))
                    scores = jnp.where(code == li + 1, s, scores)
                o = o + _dot(scores.astype(BF16), vh.astype(BF16))
                kst = (kh * e_st[:, ls]).astype(BF16)
                st_ref[bb, h] = st * e_b[C - 1:C, ls] + _dot_tn(vh.astype(BF16), kst)
                ms = jnp.mean(o * o, axis=-1, keepdims=True)
                on = o * lax.rsqrt(ms + EPS) * gain
                gh = gp[:, ls]
                outs.append(on * (gh * _sigmoid(gh)))
            o_ref[bb, rows, :] = jnp.concatenate(outs, axis=-1).astype(o_ref.dtype)
        return carry

    @pl.when(step_is_mild)
    def _():
        lax.fori_loop(0, n_chunks, functools.partial(chunk, True), 0)

    @pl.when(jnp.logical_not(step_is_mild))
    def _():
        lax.fori_loop(0, n_chunks, functools.partial(chunk, False), 0)


HG_SEQS = 4


def _hgrn(hg_in, f_min, lb_logits, gain, B, S):
    T = hg_in.shape[0]
    lt = HG_TILE
    tri, code, halves = _hgrn_constants()
    nseq = HG_SEQS if B % HG_SEQS == 0 else 1
    step_min = jnp.min(f_min.reshape(B // nseq, nseq, S // lt, lt // TM_PROJ), axis=(1, 3))
    mild = (step_min >= HG_MILD_FMIN).astype(I32).reshape(-1)
    wsum = (np.arange(HG_SUB * HG_DIM)[:, None] // HG_DIM == np.arange(HG_CHUNK)[None, :] % HG_SUB).astype(np.float32)
    const = lambda b, s, m: (0, 0)
    grid_spec = pltpu.PrefetchScalarGridSpec(
        num_scalar_prefetch=1,
        grid=(B // nseq, S // lt),
        in_specs=[pl.BlockSpec((nseq, lt, 4 * HG_W), lambda b, s, m: (b, s, 0)),
                  pl.BlockSpec(lb_logits.shape, const),
                  pl.BlockSpec((1, HG_DIM), const),
                  pl.BlockSpec(tri.shape, const),
                  pl.BlockSpec(code.shape, const),
                  pl.BlockSpec(wsum.shape, const)],
        out_specs=pl.BlockSpec((nseq, lt, HG_W), lambda b, s, m: (b, s, 0)),
        scratch_shapes=[pltpu.VMEM((nseq, HG_HEADS, HG_DIM, HG_DIM), F32)],
    )
    out = pl.pallas_call(
        functools.partial(_hgrn_kernel, halves=tuple(halves)),
        grid_spec=grid_spec,
        out_shape=jax.ShapeDtypeStruct((B, S, HG_W), BF16),
        compiler_params=_cparams(("parallel", "arbitrary")),
        name="hgrn2",
    )(mild, hg_in.reshape(B, S, 4 * HG_W), lb_logits, gain.reshape(1, HG_DIM), jnp.asarray(tri),
      jnp.asarray(code), jnp.asarray(wsum, dtype=BF16))
    return out.reshape(T, HG_W)


SWA_QBLOCKS = 8
ROPE_ROWS = 16


def _swa_constants():
    lane = np.arange(V7X_LANES)
    c = lane % ATT_DH
    half = ROPE_DIM // 2
    inv_freq = ROPE_THETA ** (-jnp.arange(half, dtype=F32) / half)
    freq_rows = jnp.broadcast_to(
        jnp.concatenate([inv_freq, jnp.zeros((ROPE_ROWS - half,), F32)])[:, None], (ROPE_ROWS, V7X_LANES))
    sel = ((np.arange(ROPE_ROWS)[:, None] == (c % half)[None, :]) & (c < ROPE_DIM)[None, :]).astype(np.float32)
    sign = np.where(c < half, -1.0, np.where(c < ROPE_DIM, 1.0, 0.0)).astype(np.float32)
    first = (c < half).astype(np.float32)
    tab = np.stack([sign, first], axis=0)
    g = (lane[:, None] // ATT_DH == lane[None, :] // ATT_DH).astype(np.float32) / ATT_DH
    return freq_rows, sel, tab, g


def _swa_kernel(sink_ref, cur_ref, prev_ref, pcur_ref, pprev_ref, qg_ref, kg_ref, freq_ref, sel_ref, tab_ref, g_ref,
                o_ref):
    Bq = ATT_BLOCK
    n = pl.program_id(1)
    tab = tab_ref[...]
    sign, first = tab[0:1, :], tab[1:2, :]
    gmat = g_ref[...].astype(BF16)
    sel = sel_ref[...].astype(BF16)
    freq_rows = freq_ref[...]
    half = ROPE_DIM // 2
    scale = ATT_DH ** -0.5

    def group_ms(x):
        sq = x * x
        hi = sq.astype(BF16)
        lo = (sq - hi.astype(F32)).astype(BF16)
        return _dot(hi, gmat) + _dot(lo, gmat)

    def spread(a):
        hi = a.astype(BF16)
        r1 = a - hi.astype(F32)
        mid = r1.astype(BF16)
        lo = (r1 - mid.astype(F32)).astype(BF16)
        return _dot_tn(hi, sel) + _dot_tn(mid, sel) + _dot_tn(lo, sel)

    def rope_table(pos_row):
        ang = freq_rows * pos_row
        return 1.0 + spread(jnp.cos(ang) - 1.0), spread(jnp.sin(ang)) * sign

    def norm_rope(x, gain, cs_sn):
        y = x * lax.rsqrt(group_ms(x) + EPS) * gain
        partner = jnp.where(first > 0.5, pltpu.roll(y, V7X_LANES - half, 1), pltpu.roll(y, half, 1))
        return y * cs_sn[0] + partner * cs_sn[1]

    kq = ATT_Q_W
    lane = lax.broadcasted_iota(I32, (Bq, V7X_LANES), 1)
    lo_half = lane < ATT_DH

    def pad_variants(a):
        r = pltpu.roll(a, ATT_DH, 1)
        z = jnp.zeros_like(a)
        return [[jnp.where(lo_half, a, z).astype(BF16), jnp.where(lo_half, z, r).astype(BF16)],
                [jnp.where(lo_half, r, z).astype(BF16), jnp.where(lo_half, z, a).astype(BF16)]]

    tables = [rope_table(pprev_ref[0].astype(F32))]
    kblocks = [pad_variants(norm_rope(prev_ref[:, 0:ATT_KV_W], kg_ref[...], tables[0]))]
    vblocks = [pad_variants(prev_ref[:, ATT_KV_W:2 * ATT_KV_W])]
    for j in range(SWA_QBLOCKS):
        rows = slice(j * Bq, (j + 1) * Bq)
        tables.append(rope_table(pcur_ref[j].astype(F32)))
        kblocks.append(pad_variants(norm_rope(cur_ref[rows, kq:kq + ATT_KV_W], kg_ref[...], tables[j + 1])))
        vblocks.append(pad_variants(cur_ref[rows, kq + ATT_KV_W:kq + 2 * ATT_KV_W]))

    qi = lax.broadcasted_iota(I32, (Bq, 2 * Bq), 0)
    kj = lax.broadcasted_iota(I32, (Bq, 2 * Bq), 1)
    in_band = ((kj < Bq) & (kj > qi)) | ((kj >= Bq) & ((kj - Bq) <= qi))
    first_of_seq = (jnp.zeros((Bq, 2 * Bq), I32) + n) == 0
    for j in range(SWA_QBLOCKS):
        rows = slice(j * Bq, (j + 1) * Bq)
        mask = (in_band & jnp.logical_not(first_of_seq & (kj < Bq))) if j == 0 else in_band
        for t in range(ATT_Q_W // V7X_LANES):
            ls = slice(t * V7X_LANES, (t + 1) * V7X_LANES)
            qt = (norm_rope(cur_ref[rows, ls], qg_ref[...], tables[j + 1]) * scale).astype(BF16)
            acc = jnp.zeros((Bq, V7X_LANES), F32)
            for u in range(2):
                head = 2 * t + u
                kvh = head // ATT_GROUP
                kcat = jnp.concatenate([kblocks[j][kvh][u], kblocks[j + 1][kvh][u]], axis=0)
                vcat = jnp.concatenate([vblocks[j][kvh][u], vblocks[j + 1][kvh][u]], axis=0)
                s = jnp.where(mask, _dot_nt(qt, kcat), NEG_INF)
                sink = sink_ref[head]
                m = jnp.maximum(jnp.max(s, axis=-1, keepdims=True), sink)
                p = jnp.exp(s - m)
                denom = jnp.sum(p, axis=-1, keepdims=True) + jnp.exp(sink - m)
                acc = acc + _dot(p.astype(BF16), vcat) * (1.0 / denom)
            o_ref[rows, ls] = acc.astype(o_ref.dtype)


def _swa(at_in, positions, q_gain, k_gain, sinks, B, S):
    T = at_in.shape[0]
    nb = S // ATT_BLOCK
    qb = SWA_QBLOCKS
    assert nb % qb == 0
    steps = nb // qb
    freq_rows, sel, tab, g = _swa_constants()
    qg = jnp.tile(q_gain.reshape(1, ATT_DH), (1, V7X_LANES // ATT_DH))
    kg = jnp.tile(k_gain.reshape(1, ATT_DH), (1, V7X_LANES // ATT_DH))
    pos3 = positions.reshape(B * nb, 1, ATT_BLOCK)
    n_at = at_in.shape[1]
    kv_blk = 2 * ATT_KV_W
    assert ATT_Q_W % kv_blk == 0
    prev_blk = lambda b, n: b * nb + jnp.maximum(qb * n - 1, 0)
    const = lambda b, n: (0, 0)
    return pl.pallas_call(
        _swa_kernel,
        grid=(B, steps),
        in_specs=[pl.BlockSpec(memory_space=pltpu.SMEM),
                  pl.BlockSpec((qb * ATT_BLOCK, n_at), lambda b, n: (b * steps + n, 0)),
                  pl.BlockSpec((ATT_BLOCK, kv_blk), lambda b, n: (prev_blk(b, n), ATT_Q_W // kv_blk)),
                  pl.BlockSpec((qb, 1, ATT_BLOCK), lambda b, n: (b * steps + n, 0, 0)),
                  pl.BlockSpec((1, 1, ATT_BLOCK), lambda b, n: (prev_blk(b, n), 0, 0)),
                  pl.BlockSpec((1, V7X_LANES), const),
                  pl.BlockSpec((1, V7X_LANES), const),
                  pl.BlockSpec(freq_rows.shape, const),
                  pl.BlockSpec(sel.shape, const),
                  pl.BlockSpec(tab.shape, const),
                  pl.BlockSpec(g.shape, const)],
        out_specs=pl.BlockSpec((qb * ATT_BLOCK, ATT_Q_W), lambda b, n: (b * steps + n, 0)),
        out_shape=jax.ShapeDtypeStruct((T, ATT_Q_W), BF16),
        compiler_params=_cparams(("parallel", "parallel")),
        name="swa",
    )(sinks, at_in, at_in, pos3, pos3, qg, kg, freq_rows, jnp.asarray(sel), jnp.asarray(tab), jnp.asarray(g))


def _merge_router_kernel(x_ref, hg_ref, at_ref, gt_ref, mod_ref, whg_ref, wat_ref, wout_ref, g2_ref,
                         wr_ref, br_ref, tri_ref,
                         x1_ref, h2_ref, idx_ref, gate_ref, rank_ref, cnt_ref, run_ref):
    i = pl.program_id(0)
    D = x_ref.shape[1]
    tm = x_ref.shape[0]

    @pl.when(i == 0)
    def _():
        run_ref[...] = jnp.zeros_like(run_ref)

    y_h = _dot(hg_ref[...], whg_ref[...])
    y_a = _dot(at_ref[...], wat_ref[...])
    merged = _sigmoid(gt_ref[:, 0:D]) * y_h + _sigmoid(gt_ref[:, D:2 * D]) * y_a
    x1 = x_ref[...] + mod_ref[2:3, :] * _dot(merged.astype(BF16), wout_ref[...])
    x1_ref[...] = x1
    ms = jnp.mean(x1 * x1, axis=-1, keepdims=True)
    h2 = x1 * lax.rsqrt(ms + EPS) * g2_ref[...] * (1.0 + mod_ref[4:5, :]) + mod_ref[3:4, :]
    _store_row_tiles(h2_ref, h2)
    logits = _dot(h2.astype(BF16), wr_ref[...]) + br_ref[...]
    E = logits.shape[1]
    lane = lax.broadcasted_iota(I32, (tm, E), 1).astype(F32)
    vals, idxs = [], []
    l = logits
    for _ in range(TOP_K):
        m = jnp.max(l, axis=-1, keepdims=True)
        ik = jnp.min(jnp.where(l == m, lane, float(E)), axis=-1, keepdims=True)
        vals.append(m)
        idxs.append(ik)
        l = jnp.where(lane == ik, NEG_INF, l)
    ex = [jnp.exp(v - vals[0]) for v in vals]
    den = ex[0]
    for e in ex[1:]:
        den = den + e
    onehot = jnp.zeros((tm, E), F32)
    for ik in idxs:
        onehot = onehot + (lane == ik).astype(F32)
    cum = _dot(tri_ref[...], onehot.astype(BF16))
    run = run_ref[0:1, 0:E]
    excl = cum - onehot + run
    lane_k = lax.broadcasted_iota(I32, (tm, TOP_K), 1)
    idx_o = jnp.zeros((tm, TOP_K), I32)
    gate_o = jnp.zeros((tm, TOP_K), F32)
    rank_o = jnp.zeros((tm, TOP_K), I32)
    for kk in range(TOP_K):
        rk = jnp.sum(jnp.where(lane == idxs[kk], excl, 0.0), axis=-1, keepdims=True)
        idx_o = jnp.where(lane_k == kk, idxs[kk].astype(I32), idx_o)
        gate_o = jnp.where(lane_k == kk, ex[kk] / den, gate_o)
        rank_o = jnp.where(lane_k == kk, rk.astype(I32), rank_o)
    idx_ref[...] = idx_o
    gate_ref[...] = gate_o
    rank_ref[...] = rank_o
    new_run = run + cum[tm - 1:tm, :]
    run_ref[0:1, 0:E] = new_run
    cnt_ref[...] = jnp.broadcast_to(new_run, cnt_ref.shape)


def _merge_router(x2, hg_o, at_o, gates, mod3, whg, wat, wout, g2, wr, br, S):
    T, D = x2.shape
    tm = TM_MERGE
    per_b = S // tm
    E = wr.shape[1]
    tri = jnp.asarray(np.tril(np.ones((tm, tm), np.float32)), dtype=BF16)
    row = lambda i: (i, 0)
    const = lambda i: (0, 0)
    return pl.pallas_call(
        _merge_router_kernel,
        grid=(T // tm,),
        in_specs=[pl.BlockSpec((tm, D), row),
                  pl.BlockSpec((tm, HG_W), row),
                  pl.BlockSpec((tm, ATT_Q_W), row),
                  pl.BlockSpec((tm, 2 * D), row),
                  pl.BlockSpec((None, 6, D), lambda i: (i // per_b, 0, 0)),
                  pl.BlockSpec(whg.shape, const),
                  pl.BlockSpec(wat.shape, const),
                  pl.BlockSpec(wout.shape, const),
                  pl.BlockSpec((1, D), const),
                  pl.BlockSpec(wr.shape, const),
                  pl.BlockSpec((1, E), const),
                  pl.BlockSpec((tm, tm), const)],
        out_specs=[pl.BlockSpec((tm, D), row),
                   pl.BlockSpec((tm * ROW_TILE, V7X_LANES), row),
                   pl.BlockSpec((tm, TOP_K), row),
                   pl.BlockSpec((tm, TOP_K), row),
                   pl.BlockSpec((tm, TOP_K), row),
                   pl.BlockSpec((V7X_SUBLANES, E), const)],
        out_shape=[jax.ShapeDtypeStruct((T, D), F32),
                   jax.ShapeDtypeStruct((T * ROW_TILE, V7X_LANES), F32),
                   jax.ShapeDtypeStruct((T, TOP_K), I32),
                   jax.ShapeDtypeStruct((T, TOP_K), F32),
                   jax.ShapeDtypeStruct((T, TOP_K), I32),
                   jax.ShapeDtypeStruct((V7X_SUBLANES, E), F32)],
        scratch_shapes=[pltpu.VMEM((V7X_SUBLANES, V7X_LANES), F32)],
        compiler_params=_cparams(("arbitrary",)),
        name="merge_router",
    )(x2, hg_o, at_o, gates, mod3, whg, wat, wout, g2.reshape(1, D), wr, br.reshape(1, E), tri)


def _dest_kernel(idx_ref, rank_ref, tab_ref, o_ref):
    idx = idx_ref[...]
    tm = idx.shape[0]
    E = tab_ref.shape[0] // TOP_K
    lane = lax.broadcasted_iota(I32, (tm, tab_ref.shape[0]), 1)
    hot = lane == idx[:, 0:1]
    for kk in range(1, TOP_K):
        hot = hot | (lane == idx[:, kk:kk + 1] + kk * E)
    start = _dot_exact01(hot.astype(F32).astype(BF16), tab_ref[...])
    o_ref[...] = rank_ref[...] + start[:, 0:TOP_K].astype(I32)


def _dest(idx, rank, pad_start):
    T = idx.shape[0]
    tm = 1024
    E = pad_start.shape[0]
    assert TOP_K * E == V7X_LANES
    k_of = np.arange(TOP_K * E) // E
    tab = jnp.where(k_of[:, None] == np.arange(V7X_LANES)[None, :], jnp.tile(pad_start, TOP_K)[:, None], 0).astype(F32)
    row = lambda i: (i, 0)
    return pl.pallas_call(
        _dest_kernel,
        grid=(T // tm,),
        in_specs=[pl.BlockSpec((tm, TOP_K), row), pl.BlockSpec((tm, TOP_K), row),
                  pl.BlockSpec(tab.shape, lambda i: (0, 0))],
        out_specs=pl.BlockSpec((tm, TOP_K), row),
        out_shape=jax.ShapeDtypeStruct((T, TOP_K), I32),
        compiler_params=_cparams(("parallel",)),
        name="dest",
    )(idx, rank, tab)


DISPATCH_UNROLL = 4


def _row(ref, r):
    return ref.at[pl.ds(pl.multiple_of(r * ROW_TILE, ROW_TILE), ROW_TILE)]


def _dispatch_kernel(fill_start_ref, fill_n_ref, tail_ref, dest_ref, h_ref, xbuf_ref, zero_ref, sem):
    tm = TM_DISPATCH
    zrows = zero_ref.shape[0]

    def zero_rows_copy(r, n):
        src = zero_ref.at[pl.ds(0, n * ROW_TILE)]
        dst = xbuf_ref.at[pl.ds(pl.multiple_of(r * ROW_TILE, ROW_TILE), n * ROW_TILE)]
        return pltpu.make_async_copy(src, dst, sem.at[1])

    def zero_block_copy(b):
        dst = xbuf_ref.at[pl.ds(pl.multiple_of(b * zrows, zrows), zrows)]
        return pltpu.make_async_copy(zero_ref, dst, sem.at[2])

    @pl.when(pl.program_id(0) == 0)
    def _():
        zero_ref[...] = jnp.zeros_like(zero_ref)

        def per_expert(start):
            def body(e, c):
                n = fill_n_ref[e]
                r = fill_start_ref[e]
                size = zrows // ROW_TILE // 2
                while size >= 1:
                    @pl.when((n & size) != 0)
                    def _(r=r, size=size):
                        cp = zero_rows_copy(r, size)
                        cp.start() if start else cp.wait()
                    r = r + (n & size)
                    size //= 2
                return c
            return body

        def tail(start):
            def body(b, c):
                cp = zero_block_copy(tail_ref[0] + b)
                cp.start() if start else cp.wait()
                return c
            return body

        lax.fori_loop(0, N_EXPERTS, per_expert(True), 0)
        lax.fori_loop(0, tail_ref[1], tail(True), 0)
        lax.fori_loop(0, N_EXPERTS, per_expert(False), 0)
        lax.fori_loop(0, tail_ref[1], tail(False), 0)

    t0 = pl.program_id(0) * tm

    def issue(tb, c):
        for u in range(DISPATCH_UNROLL):
            t = tb * DISPATCH_UNROLL + u
            src = _row(h_ref, t0 + t)
            for kk in range(TOP_K):
                d = dest_ref[t * TOP_K + kk]
                pltpu.make_async_copy(src, _row(xbuf_ref, d), sem.at[0]).start(priority=kk % 2)
        return c

    lax.fori_loop(0, tm // DISPATCH_UNROLL, issue, 0)

    def drain(tb, c):
        for _ in range(DISPATCH_UNROLL * TOP_K):
            pltpu.make_async_copy(_row(h_ref, 0), _row(xbuf_ref, 0), sem.at[0]).wait()
        return c

    lax.fori_loop(0, tm // DISPATCH_UNROLL, drain, 0)


def _dispatch(fill_start, fill_n, tail, dest_flat, h2t, P):
    T = h2t.shape[0] // ROW_TILE
    tm = TM_DISPATCH
    grid_spec = pltpu.PrefetchScalarGridSpec(
        num_scalar_prefetch=3,
        grid=(T // tm,),
        in_specs=[pl.BlockSpec((tm * TOP_K,), lambda i, *_: (i,), memory_space=pltpu.SMEM),
                  pl.BlockSpec(memory_space=pl.ANY)],
        out_specs=pl.BlockSpec(memory_space=pl.ANY),
        scratch_shapes=[pltpu.VMEM((MOE_BM * ROW_TILE, V7X_LANES), F32), pltpu.SemaphoreType.DMA((3,))],
    )
    return pl.pallas_call(
        _dispatch_kernel,
        grid_spec=grid_spec,
        out_shape=jax.ShapeDtypeStruct((P * ROW_TILE, V7X_LANES), F32),
        compiler_params=_cparams(("arbitrary",)),
        name="dispatch",
    )(fill_start, fill_n, tail, dest_flat, h2t)


FFN_PREP_COLS = 256
FFN_UNITS_PER_STEP = 1


def _ffn_kernel(cur_ref, src_ref, slot_ref, pos_ref, last_ref, used_ref,
                x_ref, wu_ref, wd_ref, bg_ref, bl_ref, bd_ref, perm_ref, y_ref,
                wg0, wl0, wd0, wg1, wl1, wd1):
    del cur_ref, src_ref
    s = pl.program_id(0)
    bm = x_ref.shape[0] // ROW_TILE
    D, De2 = wu_ref.shape
    De = De2 // 2
    half = FFN_PREP_COLS // 2
    n_units = De2 // FFN_PREP_COLS
    drows = De // n_units
    slot = slot_ref[s]
    used = used_ref[s] == 1
    stages = ((wg0, wl0, wd0), (wg1, wl1, wd1))
    perm = perm_ref[...]

    def stage_unit(u, dst):
        wg, wl, wdb = dst
        c0 = pl.multiple_of(u * FFN_PREP_COLS, FFN_PREP_COLS)
        p = _dot(wu_ref[:, pl.ds(c0, FFN_PREP_COLS)].astype(BF16), perm)
        r0 = pl.multiple_of(u * half, half)
        wg[:, pl.ds(r0, half)] = p[:, 0:half].astype(BF16)
        wl[:, pl.ds(r0, half)] = p[:, half:2 * half].astype(BF16)
        d0 = pl.multiple_of(u * drows, drows)
        wdb[pl.ds(d0, drows), :] = wd_ref[pl.ds(d0, drows), :].astype(BF16)

    @pl.when(s == 0)
    def _():
        for u in range(n_units):
            stage_unit(u, stages[0])

    done = (pos_ref[s] + 1) * FFN_UNITS_PER_STEP

    def block(cur, nxt):
        @pl.when(used)
        def _():
            u0 = jnp.minimum(pos_ref[s] * FFN_UNITS_PER_STEP, n_units - FFN_UNITS_PER_STEP)
            for j in range(FFN_UNITS_PER_STEP):
                stage_unit(u0 + j, nxt)
            wg, wl, wdb = cur
            x = _load_row_tiles(x_ref, bm).astype(BF16)
            glu = _dot(x, wg[...]) + bg_ref[...]
            lin = _dot(x, wl[...]) + bl_ref[...]
            glu = jnp.minimum(glu, SWIGLU_LIMIT)
            lin = jnp.clip(lin, -SWIGLU_LIMIT, SWIGLU_LIMIT)
            act = glu * _sigmoid(SWIGLU_ALPHA * glu) * (lin + 1.0)
            _store_row_tiles(y_ref, _dot(act.astype(BF16), wdb[...]) + bd_ref[...])

        @pl.when(used & (last_ref[s] == 1) & (done < n_units))
        def _():
            def body(u, c):
                stage_unit(u, nxt)
                return c
            lax.fori_loop(done, n_units, body, 0)

    @pl.when(slot == 0)
    def _():
        block(stages[0], stages[1])

    @pl.when(slot == 1)
    def _():
        block(stages[1], stages[0])

    @pl.when(jnp.logical_not(used) & (s > 0))
    def _():
        y_ref[...] = jnp.zeros_like(y_ref)


def _ffn_schedule(block_expert, n_used):
    n = block_expert.shape[0]
    idx = jnp.arange(n, dtype=I32)
    be = block_expert
    first = jnp.concatenate([jnp.ones((1,), bool), be[1:] != be[:-1]])
    run_start = lax.cummax(jnp.where(first, idx, 0))
    ordinal = jnp.cumsum(first.astype(I32)) - 1
    is_last = jnp.concatenate([first[1:], jnp.ones((1,), bool)])
    next_first = lax.cummin(jnp.where(first, idx, n), reverse=True)
    next_start = jnp.concatenate([next_first[1:], jnp.full((1,), n, I32)])
    next_e = be[jnp.minimum(next_start, n - 1)]
    blk = jnp.maximum(jnp.arange(n + 1, dtype=I32) - 1, 0)
    step = jnp.arange(n + 1, dtype=I32)
    cur = be[blk]
    src = jnp.where(step == 0, be[0], next_e[blk])
    slot = ordinal[blk] % 2
    pos = blk - run_start[blk]
    last = is_last[blk].astype(I32)
    used = ((step >= 1) & (blk < n_used[0])).astype(I32)
    return [a.astype(I32) for a in (cur, src, slot, pos, last, used)]


def _ffn(block_expert, n_used, xbuf, w_up, bg, bl, w_down, bd):
    P = xbuf.shape[0] // ROW_TILE
    bm = MOE_BM
    _, D, De2 = w_up.shape
    De = De2 // 2
    assert (De2 // FFN_PREP_COLS) % FFN_UNITS_PER_STEP == 0
    sched = _ffn_schedule(block_expert, n_used)
    cc = np.arange(FFN_PREP_COLS)
    perm = (cc[:, None] == np.where(cc < FFN_PREP_COLS // 2, 2 * cc, 2 * (cc - FFN_PREP_COLS // 2) + 1)[None, :])
    rows = lambda s, *_: (jnp.maximum(s - 1, 0), 0)
    wmap = lambda s, cur, src, *_: (src[s], 0, 0)
    bmap = lambda s, cur, *_: (cur[s], 0, 0)
    grid_spec = pltpu.PrefetchScalarGridSpec(
        num_scalar_prefetch=len(sched),
        grid=(P // bm + 1,),
        in_specs=[pl.BlockSpec((bm * ROW_TILE, V7X_LANES), rows),
                  pl.BlockSpec((None, D, De2), wmap),
                  pl.BlockSpec((None, De, D), wmap),
                  pl.BlockSpec((None, 1, De), bmap),
                  pl.BlockSpec((None, 1, De), bmap),
                  pl.BlockSpec((None, 1, D), bmap),
                  pl.BlockSpec(perm.shape, lambda s, *_: (0, 0))],
        out_specs=pl.BlockSpec((bm * ROW_TILE, V7X_LANES), rows),
        scratch_shapes=[pltpu.VMEM((D, De), BF16), pltpu.VMEM((D, De), BF16), pltpu.VMEM((De, D), BF16)] * 2,
    )
    return pl.pallas_call(
        _ffn_kernel,
        grid_spec=grid_spec,
        out_shape=jax.ShapeDtypeStruct((P * ROW_TILE, V7X_LANES), F32),
        compiler_params=_cparams(("arbitrary",)),
        name="expert_ffn",
    )(*sched, xbuf, w_up, w_down, bg, bl, bd, jnp.asarray(perm, dtype=BF16))


def _combine_kernel(dcur_ref, dnext_ref, gate_ref, x1_ref, mod_ref, y_hbm, o_ref, buf, sem):
    i = pl.program_id(0)
    n = pl.num_programs(0)
    tm = x1_ref.shape[0]
    slot = i % 2

    def issue(dref, s):
        def body(tb, c):
            for u in range(DISPATCH_UNROLL):
                t = tb * DISPATCH_UNROLL + u
                for kk in range(TOP_K):
                    d = dref[t * TOP_K + kk]
                    pltpu.make_async_copy(_row(y_hbm, d), _row(buf.at[s, kk], t), sem.at[s]).start(priority=kk % 2)
            return c
        lax.fori_loop(0, tm // DISPATCH_UNROLL, body, 0)

    @pl.when(i == 0)
    def _():
        issue(dcur_ref, 0)

    @pl.when(i + 1 < n)
    def _():
        issue(dnext_ref, 1 - slot)

    def drain(tb, c):
        for _ in range(DISPATCH_UNROLL * TOP_K):
            pltpu.make_async_copy(_row(y_hbm, 0), _row(buf.at[slot, 0], 0), sem.at[slot]).wait()
        return c

    lax.fori_loop(0, tm // DISPATCH_UNROLL, drain, 0)
    gate = gate_ref[...]
    acc = gate[:, 0:1] * _load_row_tiles(buf.at[slot, 0], tm)
    for kk in range(1, TOP_K):
        acc = acc + gate[:, kk:kk + 1] * _load_row_tiles(buf.at[slot, kk], tm)
    o_ref[...] = x1_ref[...] + mod_ref[5:6, :] * acc


def _combine(dest_flat, gate, x1, mod3, ybuf, S):
    T, D = x1.shape
    tm = TM_COMBINE
    per_b = S // tm
    nt = T // tm
    return pl.pallas_call(
        _combine_kernel,
        grid=(nt,),
        in_specs=[pl.BlockSpec((tm * TOP_K,), lambda i: (i,), memory_space=pltpu.SMEM),
                  pl.BlockSpec((tm * TOP_K,), lambda i: (jnp.minimum(i + 1, nt - 1),), memory_space=pltpu.SMEM),
                  pl.BlockSpec((tm, TOP_K), lambda i: (i, 0)),
                  pl.BlockSpec((tm, D), lambda i: (i, 0)),
                  pl.BlockSpec((None, 6, D), lambda i: (i // per_b, 0, 0)),
                  pl.BlockSpec(memory_space=pl.ANY)],
        out_specs=pl.BlockSpec((tm, D), lambda i: (i, 0)),
        out_shape=jax.ShapeDtypeStruct((T, D), F32),
        scratch_shapes=[pltpu.VMEM((2, TOP_K, tm * ROW_TILE, V7X_LANES), F32), pltpu.SemaphoreType.DMA((2,))],
        compiler_params=_cparams(("arbitrary",)),
        name="combine",
    )(dest_flat, dest_flat, gate, x1, mod3, ybuf)


def kernel(x, c, positions, w_ada, b_ada, norm1_gain, w_in, lower_bound_logits, hg_norm_gain, w_hg_branch,
           q_norm_gain, k_norm_gain, attn_sinks, w_attn_branch, w_out, norm2_gain, w_router, b_router,
           w_up, b_up, w_down, b_down):
    B, S, D = x.shape
    T = B * S
    assert w_ada.shape[0] == 1, "one layer"
    x2 = x.reshape(T, D)

    mod = _ada(c, w_ada[0], b_ada[0])
    mod3 = mod.reshape(B, 6, D)

    hg_in, at_in, gates, f_min = _inproj(x2, mod3, norm1_gain[0], w_in[0].astype(BF16), S)
    f_min = f_min[::V7X_SUBLANES, 0].reshape(B, S // TM_PROJ)
    hg_o = _hgrn(hg_in, f_min, lower_bound_logits, hg_norm_gain[0], B, S)
    at_o = _swa(at_in, positions, q_norm_gain[0], k_norm_gain[0], attn_sinks[0], B, S)

    x1, h2, idx, gate, rank, cnt = _merge_router(
        x2, hg_o, at_o, gates, mod3, w_hg_branch[0].astype(BF16), w_attn_branch[0].astype(BF16),
        w_out[0].astype(BF16), norm2_gain[0], w_router[0].astype(BF16), b_router[0], S)

    bm = MOE_BM
    counts = cnt[0].astype(I32)
    padded = (counts + bm - 1) // bm * bm
    pad_end = jnp.cumsum(padded)
    pad_start = pad_end - padded
    P = T * TOP_K + N_EXPERTS * bm
    n_blocks = P // bm
    block_start = jnp.arange(n_blocks, dtype=I32) * bm
    block_expert = jnp.minimum(
        jnp.sum((pad_end[None, :] <= block_start[:, None]).astype(I32), axis=1), N_EXPERTS - 1).astype(I32)
    n_used = (pad_end[-1:] // bm).astype(I32)
    tail = jnp.concatenate([n_used, n_blocks - n_used]).astype(I32)

    dest = _dest(idx, rank, pad_start)
    dest_flat = dest.reshape(T * TOP_K)
    xbuf = _dispatch((pad_start + counts).astype(I32), (padded - counts).astype(I32), tail, dest_flat, h2, P)

    ybuf = _ffn(block_expert, n_used, xbuf, w_up[0],
                b_up[0][:, None, 0::2], b_up[0][:, None, 1::2],
                w_down[0], b_down[0][:, None, :])

    out = _combine(dest_flat, gate, x1, mod3, ybuf, S)
    return out.reshape(B, S, D)
```

```python
import functools

import numpy as np
import jax
import jax.numpy as jnp
from jax import lax
from jax.experimental import pallas as pl
from jax.experimental.pallas import tpu as pltpu

F32 = jnp.float32
BF16 = jnp.bfloat16
I32 = jnp.int32

HG_HEADS = 4
HG_DIM = 128
HG_W = HG_HEADS * HG_DIM
ATT_Q_HEADS = 8
ATT_KV_HEADS = 2
ATT_GROUP = ATT_Q_HEADS // ATT_KV_HEADS
ATT_DH = 64
ATT_Q_W = ATT_Q_HEADS * ATT_DH
ATT_KV_W = ATT_KV_HEADS * ATT_DH
ATT_BLOCK = 128
ROPE_THETA = 500000.0
ROPE_DIM = ATT_DH // 4
N_EXPERTS = 32
TOP_K = 4
SWIGLU_ALPHA = 1.702
SWIGLU_LIMIT = 7.0
EPS = 1e-6

V7X_LANES = 128
V7X_SUBLANES = 8
V7X_VMEM_LIMIT_BYTES = 56 * 1024 * 1024

TM_PROJ = 256
TM_MERGE = 512
HG_TILE = 512
HG_CHUNK = 128
HG_SUB = 8
MOE_BM = 512
TM_DISPATCH = 1024
TM_COMBINE = 256

NEG_INF = float("-inf")


def _cparams(sem, vmem=V7X_VMEM_LIMIT_BYTES):
    return pltpu.CompilerParams(dimension_semantics=sem, vmem_limit_bytes=vmem)


def _sigmoid(x):
    return 1.0 / (1.0 + jnp.exp(-x))


def _dot(a, b):
    return jnp.dot(a, b, preferred_element_type=F32)


def _dot_nt(a, b):
    return lax.dot_general(a, b, (((1,), (1,)), ((), ())), preferred_element_type=F32)


def _dot_tn(a, b):
    return lax.dot_general(a, b, (((0,), (0,)), ((), ())), preferred_element_type=F32)


ROW_TILE = V7X_SUBLANES


def _store_row_tiles(ref, val):
    rows = val.shape[0]
    for g in range(ROW_TILE):
        ref[pl.ds(g, rows, stride=ROW_TILE), :] = val[:, g * V7X_LANES:(g + 1) * V7X_LANES]


def _load_row_tiles(ref, rows):
    return jnp.concatenate([ref[pl.ds(g, rows, stride=ROW_TILE), :] for g in range(ROW_TILE)], axis=1)


def _dot_exact01(m01, x):
    hi = x.astype(BF16)
    r1 = x - hi.astype(F32)
    mid = r1.astype(BF16)
    lo = (r1 - mid.astype(F32)).astype(BF16)
    return _dot(m01, hi) + _dot(m01, mid) + _dot(m01, lo)


def _ada_kernel(c_ref, w_ref, b_ref, o_ref):
    c = c_ref[...]
    cond = c * _sigmoid(c)
    o_ref[...] = _dot(cond.astype(BF16), w_ref[...].astype(BF16)) + b_ref[...]


def _ada(c, w_ada, b_ada):
    B, D = c.shape
    N = w_ada.shape[1]
    tn = D
    return pl.pallas_call(
        _ada_kernel,
        grid=(N // tn,),
        in_specs=[pl.BlockSpec((B, D), lambda j: (0, 0)),
                  pl.BlockSpec((D, tn), lambda j: (0, j)),
                  pl.BlockSpec((1, tn), lambda j: (0, j))],
        out_specs=pl.BlockSpec((B, tn), lambda j: (0, j)),
        out_shape=jax.ShapeDtypeStruct((B, N), F32),
        compiler_params=_cparams(("parallel",)),
        name="ada",
    )(c, w_ada, b_ada.reshape(1, N))


def _inproj_kernel(x_ref, mod_ref, g_ref, w_ref, hg_ref, at_ref, gt_ref, fmin_ref):
    x = x_ref[...]
    ms = jnp.mean(x * x, axis=-1, keepdims=True)
    y = x * lax.rsqrt(ms + EPS) * g_ref[...]
    h = (y * (1.0 + mod_ref[1:2, :]) + mod_ref[0:1, :]).astype(BF16)
    n_hg = hg_ref.shape[1]
    n_at = at_ref.shape[1]
    n_gt = gt_ref.shape[1]
    hg = _dot(h, w_ref[:, 0:n_hg])
    hg_ref[...] = hg
    f_pre = hg[:, HG_W:2 * HG_W]
    f_min = jnp.min(jnp.min(f_pre, axis=-1, keepdims=True), axis=0, keepdims=True)
    fmin_ref[...] = jnp.broadcast_to(f_min, fmin_ref.shape)
    at_ref[...] = _dot(h, w_ref[:, n_hg:n_hg + n_at])
    gt_ref[...] = _dot(h, w_ref[:, n_hg + n_at:n_hg + n_at + n_gt])


def _inproj(x2, mod3, gain, w_in_bf, S):
    T, D = x2.shape
    tm = TM_PROJ
    n_hg = 4 * HG_W
    n_at = ATT_Q_W + 2 * ATT_KV_W
    n_gt = 2 * D
    assert w_in_bf.shape[1] == n_hg + n_at + n_gt
    per_b = S // tm
    return pl.pallas_call(
        _inproj_kernel,
        grid=(T // tm,),
        in_specs=[pl.BlockSpec((tm, D), lambda i: (i, 0)),
                  pl.BlockSpec((None, 6, D), lambda i: (i // per_b, 0, 0)),
                  pl.BlockSpec((1, D), lambda i: (0, 0)),
                  pl.BlockSpec(w_in_bf.shape, lambda i: (0, 0))],
        out_specs=[pl.BlockSpec((tm, n_hg), lambda i: (i, 0)),
                   pl.BlockSpec((tm, n_at), lambda i: (i, 0)),
                   pl.BlockSpec((tm, n_gt), lambda i: (i, 0)),
                   pl.BlockSpec((V7X_SUBLANES, V7X_LANES), lambda i: (i, 0))],
        out_shape=[jax.ShapeDtypeStruct((T, n_hg), F32),
                   jax.ShapeDtypeStruct((T, n_at), F32),
                   jax.ShapeDtypeStruct((T, n_gt), F32),
                   jax.ShapeDtypeStruct((T // tm * V7X_SUBLANES, V7X_LANES), F32)],
        compiler_params=_cparams(("parallel",)),
        name="inproj",
    )(x2, mod3, gain.reshape(1, D), w_in_bf)


def _hgrn_constants():
    C = HG_CHUNK
    tri = (np.arange(C)[None, :] <= np.arange(C)[:, None]).astype(np.float32)
    halves = []
    lh = C // 2
    while lh >= HG_SUB:
        halves.append(lh)
        lh //= 2
    code = np.zeros((C, C), np.int32)
    ii, jj = np.meshgrid(np.arange(C), np.arange(C), indexing="ij")
    for li, lh in enumerate(halves):
        sel = ((ii // (2 * lh)) == (jj // (2 * lh))) & ((ii % (2 * lh)) >= lh) & ((jj % (2 * lh)) < lh)
        code[sel] = li + 1
    return tri, code, halves


HG_MILD_FMIN = -7.0


def _hgrn_kernel(mild_ref, in_ref, lbl_ref, gain_ref, tri_ref, code_ref, wsum_ref, o_ref, st_ref, *, halves):
    C = HG_CHUNK
    W = HG_W
    n_chunks = in_ref.shape[1] // C
    step_is_mild = mild_ref[pl.program_id(0) * pl.num_programs(1) + pl.program_id(1)] == 1

    @pl.when(pl.program_id(1) == 0)
    def _():
        st_ref[...] = jnp.zeros_like(st_ref)

    l0 = lbl_ref[0:1, :]
    l1 = lbl_ref[1:2, :]
    mx = jnp.maximum(l0, l1)
    e0 = jnp.exp(l0 - mx)
    lb = e0 / (e0 + jnp.exp(l1 - mx))
    gain = gain_ref[...]
    tri = tri_ref[...].astype(BF16)
    code = code_ref[...]
    wsum = wsum_ref[...]
    row = lax.broadcasted_iota(I32, (C, C), 0)
    col = lax.broadcasted_iota(I32, (C, C), 1)
    row_in_sub = lax.broadcasted_iota(I32, (C, HG_DIM), 0) % HG_SUB
    same_sub = (row // HG_SUB) == (col // HG_SUB)
    sub_causal = same_sub & (col <= row)

    def group_row(a, group, r):
        a3 = a.reshape(C // group, group, a.shape[1])
        return jnp.broadcast_to(a3[:, r:r + 1, :], a3.shape).reshape(a.shape)

    def bcast_sub(a, j):
        return group_row(a, HG_SUB, j)

    def chunk(mild, c, carry):
        r0 = pl.multiple_of(c * C, C)
        rows = pl.ds(r0, C)
        for bb in range(in_ref.shape[0]):
            qp = in_ref[bb, rows, 0:W]
            fp = in_ref[bb, rows, W:2 * W]
            vv = in_ref[bb, rows, 2 * W:3 * W]
            gp = in_ref[bb, rows, 3 * W:4 * W]
            forget = lb + (1.0 - lb) * _sigmoid(fp)
            q = qp * _sigmoid(qp)
            k = 1.0 - forget
            lf = jnp.log(forget)
            b = _dot_exact01(tri, lf)
            e_b = jnp.exp(b)
            e_st = jnp.exp(group_row(b, C, C - 1) - b)
            e_lv = [jnp.exp(-jnp.abs(b - group_row(b, 2 * lh, lh - 1))) for lh in halves]
            if mild:
                x_sub = group_row(b, HG_SUB, HG_SUB - 1) - b
                e_subk = jnp.exp(x_sub)
                e_subq = jnp.exp(-x_sub)
            outs = []
            for h in range(HG_HEADS):
                ls = slice(h * HG_DIM, (h + 1) * HG_DIM)
                qh, kh, vh = q[:, ls], k[:, ls], vv[:, ls]
                bh = b[:, ls]
                st = st_ref[bb, h]
                o = _dot_nt((qh * e_b[:, ls]).astype(BF16), st.astype(BF16))
                if mild:
                    s = _dot_nt((qh * e_subq[:, ls]).astype(BF16), (kh * e_subk[:, ls]).astype(BF16))
                    scores = jnp.where(sub_causal, s, 0.0)
                else:
                    ps = []
                    for j in range(HG_SUB):
                        d = jnp.where(row_in_sub >= j, bh - bcast_sub(bh, j), NEG_INF)
                        ps.append((qh * bcast_sub(kh, j) * jnp.exp(d)).astype(BF16))
                    scores = jnp.where(same_sub, _dot(jnp.concatenate(ps, axis=1), wsum), 0.0)
                for li in range(len(halves)):
                    e = e_lv[li][:, ls]
                    s = _dot_nt((qh * e).astype(BF16), (kh * e).astype(BF16))
                    scores = jnp.where(code == li + 1, s, scores)
                o = o + _dot(scores.astype(BF16), vh.astype(BF16))
                kst = (kh * e_st[:, ls]).astype(BF16)
                st_ref[bb, h] = st * e_b[C - 1:C, ls] + _dot_tn(vh.astype(BF16), kst)
                ms = jnp.mean(o * o, axis=-1, keepdims=True)
                on = o * lax.rsqrt(ms + EPS) * gain
                gh = gp[:, ls]
                outs.append(on * (gh * _sigmoid(gh)))
            o_ref[bb, rows, :] = jnp.concatenate(outs, axis=-1).astype(o_ref.dtype)
        return carry

    @pl.when(step_is_mild)
    def _():
        lax.fori_loop(0, n_chunks, functools.partial(chunk, True), 0)

    @pl.when(jnp.logical_not(step_is_mild))
    def _():
        lax.fori_loop(0, n_chunks, functools.partial(chunk, False), 0)


HG_SEQS = 4


def _hgrn(hg_in, f_min, lb_logits, gain, B, S):
    T = hg_in.shape[0]
    lt = HG_TILE
    tri, code, halves = _hgrn_constants()
    nseq = HG_SEQS if B % HG_SEQS == 0 else 1
    step_min = jnp.min(f_min.reshape(B // nseq, nseq, S // lt, lt // TM_PROJ), axis=(1, 3))
    mild = (step_min >= HG_MILD_FMIN).astype(I32).reshape(-1)
    wsum = (np.arange(HG_SUB * HG_DIM)[:, None] // HG_DIM == np.arange(HG_CHUNK)[None, :] % HG_SUB).astype(np.float32)
    const = lambda b, s, m: (0, 0)
    grid_spec = pltpu.PrefetchScalarGridSpec(
        num_scalar_prefetch=1,
        grid=(B // nseq, S // lt),
        in_specs=[pl.BlockSpec((nseq, lt, 4 * HG_W), lambda b, s, m: (b, s, 0)),
                  pl.BlockSpec(lb_logits.shape, const),
                  pl.BlockSpec((1, HG_DIM), const),
                  pl.BlockSpec(tri.shape, const),
                  pl.BlockSpec(code.shape, const),
                  pl.BlockSpec(wsum.shape, const)],
        out_specs=pl.BlockSpec((nseq, lt, HG_W), lambda b, s, m: (b, s, 0)),
        scratch_shapes=[pltpu.VMEM((nseq, HG_HEADS, HG_DIM, HG_DIM), F32)],
    )
    out = pl.pallas_call(
        functools.partial(_hgrn_kernel, halves=tuple(halves)),
        grid_spec=grid_spec,
        out_shape=jax.ShapeDtypeStruct((B, S, HG_W), BF16),
        compiler_params=_cparams(("parallel", "arbitrary")),
        name="hgrn2",
    )(mild, hg_in.reshape(B, S, 4 * HG_W), lb_logits, gain.reshape(1, HG_DIM), jnp.asarray(tri),
      jnp.asarray(code), jnp.asarray(wsum, dtype=BF16))
    return out.reshape(T, HG_W)


SWA_QBLOCKS = 8
ROPE_ROWS = 16


def _swa_constants():
    lane = np.arange(V7X_LANES)
    c = lane % ATT_DH
    half = ROPE_DIM // 2
    inv_freq = ROPE_THETA ** (-jnp.arange(half, dtype=F32) / half)
    freq_rows = jnp.broadcast_to(
        jnp.concatenate([inv_freq, jnp.zeros((ROPE_ROWS - half,), F32)])[:, None], (ROPE_ROWS, V7X_LANES))
    sel = ((np.arange(ROPE_ROWS)[:, None] == (c % half)[None, :]) & (c < ROPE_DIM)[None, :]).astype(np.float32)
    sign = np.where(c < half, -1.0, np.where(c < ROPE_DIM, 1.0, 0.0)).astype(np.float32)
    first = (c < half).astype(np.float32)
    tab = np.stack([sign, first], axis=0)
    g = (lane[:, None] // ATT_DH == lane[None, :] // ATT_DH).astype(np.float32) / ATT_DH
    return freq_rows, sel, tab, g


def _swa_kernel(sink_ref, cur_ref, prev_ref, pcur_ref, pprev_ref, qg_ref, kg_ref, freq_ref, sel_ref, tab_ref, g_ref,
                o_ref):
    Bq = ATT_BLOCK
    n = pl.program_id(1)
    tab = tab_ref[...]
    sign, first = tab[0:1, :], tab[1:2, :]
    gmat = g_ref[...].astype(BF16)
    sel = sel_ref[...].astype(BF16)
    freq_rows = freq_ref[...]
    half = ROPE_DIM // 2
    scale = ATT_DH ** -0.5

    def group_ms(x):
        sq = x * x
        hi = sq.astype(BF16)
        lo = (sq - hi.astype(F32)).astype(BF16)
        return _dot(hi, gmat) + _dot(lo, gmat)

    def spread(a):
        hi = a.astype(BF16)
        r1 = a - hi.astype(F32)
        mid = r1.astype(BF16)
        lo = (r1 - mid.astype(F32)).astype(BF16)
        return _dot_tn(hi, sel) + _dot_tn(mid, sel) + _dot_tn(lo, sel)

    def rope_table(pos_row):
        ang = freq_rows * pos_row
        return 1.0 + spread(jnp.cos(ang) - 1.0), spread(jnp.sin(ang)) * sign

    def norm_rope(x, gain, cs_sn):
        y = x * lax.rsqrt(group_ms(x) + EPS) * gain
        partner = jnp.where(first > 0.5, pltpu.roll(y, V7X_LANES - half, 1), pltpu.roll(y, half, 1))
        return y * cs_sn[0] + partner * cs_sn[1]

    kq = ATT_Q_W
    lane = lax.broadcasted_iota(I32, (Bq, V7X_LANES), 1)
    lo_half = lane < ATT_DH

    def pad_variants(a):
        r = pltpu.roll(a, ATT_DH, 1)
        z = jnp.zeros_like(a)
        return [[jnp.where(lo_half, a, z).astype(BF16), jnp.where(lo_half, z, r).astype(BF16)],
                [jnp.where(lo_half, r, z).astype(BF16), jnp.where(lo_half, z, a).astype(BF16)]]

    tables = [rope_table(pprev_ref[0].astype(F32))]
    kblocks = [pad_variants(norm_rope(prev_ref[:, 0:ATT_KV_W], kg_ref[...], tables[0]))]
    vblocks = [pad_variants(prev_ref[:, ATT_KV_W:2 * ATT_KV_W])]
    for j in range(SWA_QBLOCKS):
        rows = slice(j * Bq, (j + 1) * Bq)
        tables.append(rope_table(pcur_ref[j].astype(F32)))
        kblocks.append(pad_variants(norm_rope(cur_ref[rows, kq:kq + ATT_KV_W], kg_ref[...], tables[j + 1])))
        vblocks.append(pad_variants(cur_ref[rows, kq + ATT_KV_W:kq + 2 * ATT_KV_W]))

    qi = lax.broadcasted_iota(I32, (Bq, 2 * Bq), 0)
    kj = lax.broadcasted_iota(I32, (Bq, 2 * Bq), 1)
    in_band = ((kj < Bq) & (kj > qi)) | ((kj >= Bq) & ((kj - Bq) <= qi))
    first_of_seq = (jnp.zeros((Bq, 2 * Bq), I32) + n) == 0
    for j in range(SWA_QBLOCKS):
        rows = slice(j * Bq, (j + 1) * Bq)
        mask = (in_band & jnp.logical_not(first_of_seq & (kj < Bq))) if j == 0 else in_band
        for t in range(ATT_Q_W // V7X_LANES):
            ls = slice(t * V7X_LANES, (t + 1) * V7X_LANES)
            qt = (norm_rope(cur_ref[rows, ls], qg_ref[...], tables[j + 1]) * scale).astype(BF16)
            acc = jnp.zeros((Bq, V7X_LANES), F32)
            for u in range(2):
                head = 2 * t + u
                kvh = head // ATT_GROUP
                kcat = jnp.concatenate([kblocks[j][kvh][u], kblocks[j + 1][kvh][u]], axis=0)
                vcat = jnp.concatenate([vblocks[j][kvh][u], vblocks[j + 1][kvh][u]], axis=0)
                s = jnp.where(mask, _dot_nt(qt, kcat), NEG_INF)
                sink = sink_ref[head]
                m = jnp.maximum(jnp.max(s, axis=-1, keepdims=True), sink)
                p = jnp.exp(s - m)
                denom = jnp.sum(p, axis=-1, keepdims=True) + jnp.exp(sink - m)
                acc = acc + _dot(p.astype(BF16), vcat) * (1.0 / denom)
            o_ref[rows, ls] = acc.astype(o_ref.dtype)


def _swa(at_in, positions, q_gain, k_gain, sinks, B, S):
    T = at_in.shape[0]
    nb = S // ATT_BLOCK
    qb = SWA_QBLOCKS
    assert nb % qb == 0
    steps = nb // qb
    freq_rows, sel, tab, g = _swa_constants()
    qg = jnp.tile(q_gain.reshape(1, ATT_DH), (1, V7X_LANES // ATT_DH))
    kg = jnp.tile(k_gain.reshape(1, ATT_DH), (1, V7X_LANES // ATT_DH))
    pos3 = positions.reshape(B * nb, 1, ATT_BLOCK)
    n_at = at_in.shape[1]
    kv_blk = 2 * ATT_KV_W
    assert ATT_Q_W % kv_blk == 0
    prev_blk = lambda b, n: b * nb + jnp.maximum(qb * n - 1, 0)
    const = lambda b, n: (0, 0)
    return pl.pallas_call(
        _swa_kernel,
        grid=(B, steps),
        in_specs=[pl.BlockSpec(memory_space=pltpu.SMEM),
                  pl.BlockSpec((qb * ATT_BLOCK, n_at), lambda b, n: (b * steps + n, 0)),
                  pl.BlockSpec((ATT_BLOCK, kv_blk), lambda b, n: (prev_blk(b, n), ATT_Q_W // kv_blk)),
                  pl.BlockSpec((qb, 1, ATT_BLOCK), lambda b, n: (b * steps + n, 0, 0)),
                  pl.BlockSpec((1, 1, ATT_BLOCK), lambda b, n: (prev_blk(b, n), 0, 0)),
                  pl.BlockSpec((1, V7X_LANES), const),
                  pl.BlockSpec((1, V7X_LANES), const),
                  pl.BlockSpec(freq_rows.shape, const),
                  pl.BlockSpec(sel.shape, const),
                  pl.BlockSpec(tab.shape, const),
                  pl.BlockSpec(g.shape, const)],
        out_specs=pl.BlockSpec((qb * ATT_BLOCK, ATT_Q_W), lambda b, n: (b * steps + n, 0)),
        out_shape=jax.ShapeDtypeStruct((T, ATT_Q_W), BF16),
        compiler_params=_cparams(("parallel", "parallel")),
        name="swa",
    )(sinks, at_in, at_in, pos3, pos3, qg, kg, freq_rows, jnp.asarray(sel), jnp.asarray(tab), jnp.asarray(g))


def _merge_router_kernel(x_ref, hg_ref, at_ref, gt_ref, mod_ref, whg_ref, wat_ref, wout_ref, g2_ref,
                         wr_ref, br_ref, tri_ref,
                         x1_ref, h2_ref, idx_ref, gate_ref, rank_ref, cnt_ref, run_ref):
    i = pl.program_id(0)
    D = x_ref.shape[1]
    tm = x_ref.shape[0]

    @pl.when(i == 0)
    def _():
        run_ref[...] = jnp.zeros_like(run_ref)

    y_h = _dot(hg_ref[...], whg_ref[...])
    y_a = _dot(at_ref[...], wat_ref[...])
    merged = _sigmoid(gt_ref[:, 0:D]) * y_h + _sigmoid(gt_ref[:, D:2 * D]) * y_a
    x1 = x_ref[...] + mod_ref[2:3, :] * _dot(merged.astype(BF16), wout_ref[...])
    x1_ref[...] = x1
    ms = jnp.mean(x1 * x1, axis=-1, keepdims=True)
    h2 = x1 * lax.rsqrt(ms + EPS) * g2_ref[...] * (1.0 + mod_ref[4:5, :]) + mod_ref[3:4, :]
    _store_row_tiles(h2_ref, h2)
    logits = _dot(h2.astype(BF16), wr_ref[...]) + br_ref[...]
    E = logits.shape[1]
    lane = lax.broadcasted_iota(I32, (tm, E), 1).astype(F32)
    vals, idxs = [], []
    l = logits
    for _ in range(TOP_K):
        m = jnp.max(l, axis=-1, keepdims=True)
        ik = jnp.min(jnp.where(l == m, lane, float(E)), axis=-1, keepdims=True)
        vals.append(m)
        idxs.append(ik)
        l = jnp.where(lane == ik, NEG_INF, l)
    ex = [jnp.exp(v - vals[0]) for v in vals]
    den = ex[0]
    for e in ex[1:]:
        den = den + e
    onehot = jnp.zeros((tm, E), F32)
    for ik in idxs:
        onehot = onehot + (lane == ik).astype(F32)
    cum = _dot(tri_ref[...], onehot.astype(BF16))
    run = run_ref[0:1, 0:E]
    excl = cum - onehot + run
    lane_k = lax.broadcasted_iota(I32, (tm, TOP_K), 1)
    idx_o = jnp.zeros((tm, TOP_K), I32)
    gate_o = jnp.zeros((tm, TOP_K), F32)
    rank_o = jnp.zeros((tm, TOP_K), I32)
    for kk in range(TOP_K):
        rk = jnp.sum(jnp.where(lane == idxs[kk], excl, 0.0), axis=-1, keepdims=True)
        idx_o = jnp.where(lane_k == kk, idxs[kk].astype(I32), idx_o)
        gate_o = jnp.where(lane_k == kk, ex[kk] / den, gate_o)
        rank_o = jnp.where(lane_k == kk, rk.astype(I32), rank_o)
    idx_ref[...] = idx_o
    gate_ref[...] = gate_o
    rank_ref[...] = rank_o
    new_run = run + cum[tm - 1:tm, :]
    run_ref[0:1, 0:E] = new_run
    cnt_ref[...] = jnp.broadcast_to(new_run, cnt_ref.shape)


def _merge_router(x2, hg_o, at_o, gates, mod3, whg, wat, wout, g2, wr, br, S):
    T, D = x2.shape
    tm = TM_MERGE
    per_b = S // tm
    E = wr.shape[1]
    tri = jnp.asarray(np.tril(np.ones((tm, tm), np.float32)), dtype=BF16)
    row = lambda i: (i, 0)
    const = lambda i: (0, 0)
    return pl.pallas_call(
        _merge_router_kernel,
        grid=(T // tm,),
        in_specs=[pl.BlockSpec((tm, D), row),
                  pl.BlockSpec((tm, HG_W), row),
                  pl.BlockSpec((tm, ATT_Q_W), row),
                  pl.BlockSpec((tm, 2 * D), row),
                  pl.BlockSpec((None, 6, D), lambda i: (i // per_b, 0, 0)),
                  pl.BlockSpec(whg.shape, const),
                  pl.BlockSpec(wat.shape, const),
                  pl.BlockSpec(wout.shape, const),
                  pl.BlockSpec((1, D), const),
                  pl.BlockSpec(wr.shape, const),
                  pl.BlockSpec((1, E), const),
                  pl.BlockSpec((tm, tm), const)],
        out_specs=[pl.BlockSpec((tm, D), row),
                   pl.BlockSpec((tm * ROW_TILE, V7X_LANES), row),
                   pl.BlockSpec((tm, TOP_K), row),
                   pl.BlockSpec((tm, TOP_K), row),
                   pl.BlockSpec((tm, TOP_K), row),
                   pl.BlockSpec((V7X_SUBLANES, E), const)],
        out_shape=[jax.ShapeDtypeStruct((T, D), F32),
                   jax.ShapeDtypeStruct((T * ROW_TILE, V7X_LANES), F32),
                   jax.ShapeDtypeStruct((T, TOP_K), I32),
                   jax.ShapeDtypeStruct((T, TOP_K), F32),
                   jax.ShapeDtypeStruct((T, TOP_K), I32),
                   jax.ShapeDtypeStruct((V7X_SUBLANES, E), F32)],
        scratch_shapes=[pltpu.VMEM((V7X_SUBLANES, V7X_LANES), F32)],
        compiler_params=_cparams(("arbitrary",)),
        name="merge_router",
    )(x2, hg_o, at_o, gates, mod3, whg, wat, wout, g2.reshape(1, D), wr, br.reshape(1, E), tri)


def _dest_kernel(idx_ref, rank_ref, tab_ref, o_ref):
    idx = idx_ref[...]
    tm = idx.shape[0]
    E = tab_ref.shape[0] // TOP_K
    lane = lax.broadcasted_iota(I32, (tm, tab_ref.shape[0]), 1)
    hot = lane == idx[:, 0:1]
    for kk in range(1, TOP_K):
        hot = hot | (lane == idx[:, kk:kk + 1] + kk * E)
    start = _dot_exact01(hot.astype(F32).astype(BF16), tab_ref[...])
    o_ref[...] = rank_ref[...] + start[:, 0:TOP_K].astype(I32)


def _dest(idx, rank, pad_start):
    T = idx.shape[0]
    tm = 1024
    E = pad_start.shape[0]
    assert TOP_K * E == V7X_LANES
    k_of = np.arange(TOP_K * E) // E
    tab = jnp.where(k_of[:, None] == np.arange(V7X_LANES)[None, :], jnp.tile(pad_start, TOP_K)[:, None], 0).astype(F32)
    row = lambda i: (i, 0)
    return pl.pallas_call(
        _dest_kernel,
        grid=(T // tm,),
        in_specs=[pl.BlockSpec((tm, TOP_K), row), pl.BlockSpec((tm, TOP_K), row),
                  pl.BlockSpec(tab.shape, lambda i: (0, 0))],
        out_specs=pl.BlockSpec((tm, TOP_K), row),
        out_shape=jax.ShapeDtypeStruct((T, TOP_K), I32),
        compiler_params=_cparams(("parallel",)),
        name="dest",
    )(idx, rank, tab)


DISPATCH_UNROLL = 4


def _row(ref, r):
    return ref.at[pl.ds(pl.multiple_of(r * ROW_TILE, ROW_TILE), ROW_TILE)]


def _dispatch_kernel(fill_start_ref, fill_n_ref, tail_ref, dest_ref, h_ref, xbuf_ref, zero_ref, sem):
    tm = h_ref.shape[0] // ROW_TILE
    zrows = zero_ref.shape[0]

    def zero_rows_copy(r, n):
        src = zero_ref.at[pl.ds(0, n * ROW_TILE)]
        dst = xbuf_ref.at[pl.ds(pl.multiple_of(r * ROW_TILE, ROW_TILE), n * ROW_TILE)]
        return pltpu.make_async_copy(src, dst, sem.at[1])

    def zero_block_copy(b):
        dst = xbuf_ref.at[pl.ds(pl.multiple_of(b * zrows, zrows), zrows)]
        return pltpu.make_async_copy(zero_ref, dst, sem.at[2])

    @pl.when(pl.program_id(0) == 0)
    def _():
        zero_ref[...] = jnp.zeros_like(zero_ref)

        def per_expert(start):
            def body(e, c):
                n = fill_n_ref[e]
                r = fill_start_ref[e]
                size = zrows // ROW_TILE // 2
                while size >= 1:
                    @pl.when((n & size) != 0)
                    def _(r=r, size=size):
                        cp = zero_rows_copy(r, size)
                        cp.start() if start else cp.wait()
                    r = r + (n & size)
                    size //= 2
                return c
            return body

        def tail(start):
            def body(b, c):
                cp = zero_block_copy(tail_ref[0] + b)
                cp.start() if start else cp.wait()
                return c
            return body

        lax.fori_loop(0, N_EXPERTS, per_expert(True), 0)
        lax.fori_loop(0, tail_ref[1], tail(True), 0)
        lax.fori_loop(0, N_EXPERTS, per_expert(False), 0)
        lax.fori_loop(0, tail_ref[1], tail(False), 0)

    def issue(tb, c):
        for u in range(DISPATCH_UNROLL):
            t = tb * DISPATCH_UNROLL + u
            src = _row(h_ref, t)
            for kk in range(TOP_K):
                d = dest_ref[t * TOP_K + kk]
                pltpu.make_async_copy(src, _row(xbuf_ref, d), sem.at[0]).start(priority=kk % 2)
        return c

    lax.fori_loop(0, tm // DISPATCH_UNROLL, issue, 0)

    def drain(tb, c):
        for _ in range(DISPATCH_UNROLL * TOP_K):
            pltpu.make_async_copy(_row(h_ref, 0), _row(xbuf_ref, 0), sem.at[0]).wait()
        return c

    lax.fori_loop(0, tm // DISPATCH_UNROLL, drain, 0)


def _dispatch(fill_start, fill_n, tail, dest_flat, h2t, P):
    T = h2t.shape[0] // ROW_TILE
    tm = TM_DISPATCH
    grid_spec = pltpu.PrefetchScalarGridSpec(
        num_scalar_prefetch=3,
        grid=(T // tm,),
        in_specs=[pl.BlockSpec((tm * TOP_K,), lambda i, *_: (i,), memory_space=pltpu.SMEM),
                  pl.BlockSpec((tm * ROW_TILE, V7X_LANES), lambda i, *_: (i, 0))],
        out_specs=pl.BlockSpec(memory_space=pl.ANY),
        scratch_shapes=[pltpu.VMEM((MOE_BM * ROW_TILE, V7X_LANES), F32), pltpu.SemaphoreType.DMA((3,))],
    )
    return pl.pallas_call(
        _dispatch_kernel,
        grid_spec=grid_spec,
        out_shape=jax.ShapeDtypeStruct((P * ROW_TILE, V7X_LANES), F32),
        compiler_params=_cparams(("arbitrary",)),
        name="dispatch",
    )(fill_start, fill_n, tail, dest_flat, h2t)


FFN_PREP_COLS = 256
FFN_UNITS_PER_STEP = 1


def _ffn_kernel(cur_ref, src_ref, slot_ref, pos_ref, last_ref, used_ref,
                x_ref, wu_ref, wd_ref, bg_ref, bl_ref, bd_ref, perm_ref, y_ref,
                wg0, wl0, wd0, wg1, wl1, wd1):
    del cur_ref, src_ref
    s = pl.program_id(0)
    bm = x_ref.shape[0] // ROW_TILE
    D, De2 = wu_ref.shape
    De = De2 // 2
    half = FFN_PREP_COLS // 2
    n_units = De2 // FFN_PREP_COLS
    drows = De // n_units
    slot = slot_ref[s]
    used = used_ref[s] == 1
    stages = ((wg0, wl0, wd0), (wg1, wl1, wd1))
    perm = perm_ref[...]

    def stage_unit(u, dst):
        wg, wl, wdb = dst
        c0 = pl.multiple_of(u * FFN_PREP_COLS, FFN_PREP_COLS)
        p = _dot(wu_ref[:, pl.ds(c0, FFN_PREP_COLS)].astype(BF16), perm)
        r0 = pl.multiple_of(u * half, half)
        wg[:, pl.ds(r0, half)] = p[:, 0:half].astype(BF16)
        wl[:, pl.ds(r0, half)] = p[:, half:2 * half].astype(BF16)
        d0 = pl.multiple_of(u * drows, drows)
        wdb[pl.ds(d0, drows), :] = wd_ref[pl.ds(d0, drows), :].astype(BF16)

    @pl.when(s == 0)
    def _():
        for u in range(n_units):
            stage_unit(u, stages[0])

    done = (pos_ref[s] + 1) * FFN_UNITS_PER_STEP

    def block(cur, nxt):
        @pl.when(used)
        def _():
            u0 = jnp.minimum(pos_ref[s] * FFN_UNITS_PER_STEP, n_units - FFN_UNITS_PER_STEP)
            for j in range(FFN_UNITS_PER_STEP):
                stage_unit(u0 + j, nxt)
            wg, wl, wdb = cur
            x = _load_row_tiles(x_ref, bm).astype(BF16)
            glu = _dot(x, wg[...]) + bg_ref[...]
            lin = _dot(x, wl[...]) + bl_ref[...]
            glu = jnp.minimum(glu, SWIGLU_LIMIT)
            lin = jnp.clip(lin, -SWIGLU_LIMIT, SWIGLU_LIMIT)
            act = glu * _sigmoid(SWIGLU_ALPHA * glu) * (lin + 1.0)
            _store_row_tiles(y_ref, _dot(act.astype(BF16), wdb[...]) + bd_ref[...])

        @pl.when(used & (last_ref[s] == 1) & (done < n_units))
        def _():
            def body(u, c):
                stage_unit(u, nxt)
                return c
            lax.fori_loop(done, n_units, body, 0)

    @pl.when(slot == 0)
    def _():
        block(stages[0], stages[1])

    @pl.when(slot == 1)
    def _():
        block(stages[1], stages[0])

    @pl.when(jnp.logical_not(used) & (s > 0))
    def _():
        y_ref[...] = jnp.zeros_like(y_ref)


def _ffn_schedule(block_expert, n_used):
    n = block_expert.shape[0]
    idx = jnp.arange(n, dtype=I32)
    be = block_expert
    first = jnp.concatenate([jnp.ones((1,), bool), be[1:] != be[:-1]])
    run_start = lax.cummax(jnp.where(first, idx, 0))
    ordinal = jnp.cumsum(first.astype(I32)) - 1
    is_last = jnp.concatenate([first[1:], jnp.ones((1,), bool)])
    next_first = lax.cummin(jnp.where(first, idx, n), reverse=True)
    next_start = jnp.concatenate([next_first[1:], jnp.full((1,), n, I32)])
    next_e = be[jnp.minimum(next_start, n - 1)]
    blk = jnp.maximum(jnp.arange(n + 1, dtype=I32) - 1, 0)
    step = jnp.arange(n + 1, dtype=I32)
    cur = be[blk]
    src = jnp.where(step == 0, be[0], next_e[blk])
    slot = ordinal[blk] % 2
    pos = blk - run_start[blk]
    last = is_last[blk].astype(I32)
    used = ((step >= 1) & (blk < n_used[0])).astype(I32)
    return [a.astype(I32) for a in (cur, src, slot, pos, last, used)]


def _ffn(block_expert, n_used, xbuf, w_up, bg, bl, w_down, bd):
    P = xbuf.shape[0] // ROW_TILE
    bm = MOE_BM
    _, D, De2 = w_up.shape
    De = De2 // 2
    assert (De2 // FFN_PREP_COLS) % FFN_UNITS_PER_STEP == 0
    sched = _ffn_schedule(block_expert, n_used)
    cc = np.arange(FFN_PREP_COLS)
    perm = (cc[:, None] == np.where(cc < FFN_PREP_COLS // 2, 2 * cc, 2 * (cc - FFN_PREP_COLS // 2) + 1)[None, :])
    rows = lambda s, *_: (jnp.maximum(s - 1, 0), 0)
    wmap = lambda s, cur, src, *_: (src[s], 0, 0)
    bmap = lambda s, cur, *_: (cur[s], 0, 0)
    grid_spec = pltpu.PrefetchScalarGridSpec(
        num_scalar_prefetch=len(sched),
        grid=(P // bm + 1,),
        in_specs=[pl.BlockSpec((bm * ROW_TILE, V7X_LANES), rows),
                  pl.BlockSpec((None, D, De2), wmap),
                  pl.BlockSpec((None, De, D), wmap),
                  pl.BlockSpec((None, 1, De), bmap),
                  pl.BlockSpec((None, 1, De), bmap),
                  pl.BlockSpec((None, 1, D), bmap),
                  pl.BlockSpec(perm.shape, lambda s, *_: (0, 0))],
        out_specs=pl.BlockSpec((bm * ROW_TILE, V7X_LANES), rows),
        scratch_shapes=[pltpu.VMEM((D, De), BF16), pltpu.VMEM((D, De), BF16), pltpu.VMEM((De, D), BF16)] * 2,
    )
    return pl.pallas_call(
        _ffn_kernel,
        grid_spec=grid_spec,
        out_shape=jax.ShapeDtypeStruct((P * ROW_TILE, V7X_LANES), F32),
        compiler_params=_cparams(("arbitrary",)),
        name="expert_ffn",
    )(*sched, xbuf, w_up, w_down, bg, bl, bd, jnp.asarray(perm, dtype=BF16))


def _combine_kernel(dcur_ref, dnext_ref, gate_ref, x1_ref, mod_ref, y_hbm, o_ref, buf, sem):
    i = pl.program_id(0)
    n = pl.num_programs(0)
    tm = x1_ref.shape[0]
    slot = i % 2

    def issue(dref, s):
        def body(tb, c):
            for u in range(DISPATCH_UNROLL):
                t = tb * DISPATCH_UNROLL + u
                for kk in range(TOP_K):
                    d = dref[t * TOP_K + kk]
                    pltpu.make_async_copy(_row(y_hbm, d), _row(buf.at[s, kk], t), sem.at[s]).start(priority=kk % 2)
            return c
        lax.fori_loop(0, tm // DISPATCH_UNROLL, body, 0)

    @pl.when(i == 0)
    def _():
        issue(dcur_ref, 0)

    @pl.when(i + 1 < n)
    def _():
        issue(dnext_ref, 1 - slot)

    def drain(tb, c):
        for _ in range(DISPATCH_UNROLL * TOP_K):
            pltpu.make_async_copy(_row(y_hbm, 0), _row(buf.at[slot, 0], 0), sem.at[slot]).wait()
        return c

    lax.fori_loop(0, tm // DISPATCH_UNROLL, drain, 0)
    gate = gate_ref[...]
    acc = gate[:, 0:1] * _load_row_tiles(buf.at[slot, 0], tm)
    for kk in range(1, TOP_K):
        acc = acc + gate[:, kk:kk + 1] * _load_row_tiles(buf.at[slot, kk], tm)
    o_ref[...] = x1_ref[...] + mod_ref[5:6, :] * acc


def _combine(dest_flat, gate, x1, mod3, ybuf, S):
    T, D = x1.shape
    tm = TM_COMBINE
    per_b = S // tm
    nt = T // tm
    return pl.pallas_call(
        _combine_kernel,
        grid=(nt,),
        in_specs=[pl.BlockSpec((tm * TOP_K,), lambda i: (i,), memory_space=pltpu.SMEM),
                  pl.BlockSpec((tm * TOP_K,), lambda i: (jnp.minimum(i + 1, nt - 1),), memory_space=pltpu.SMEM),
                  pl.BlockSpec((tm, TOP_K), lambda i: (i, 0)),
                  pl.BlockSpec((tm, D), lambda i: (i, 0)),
                  pl.BlockSpec((None, 6, D), lambda i: (i // per_b, 0, 0)),
                  pl.BlockSpec(memory_space=pl.ANY)],
        out_specs=pl.BlockSpec((tm, D), lambda i: (i, 0)),
        out_shape=jax.ShapeDtypeStruct((T, D), F32),
        scratch_shapes=[pltpu.VMEM((2, TOP_K, tm * ROW_TILE, V7X_LANES), F32), pltpu.SemaphoreType.DMA((2,))],
        compiler_params=_cparams(("arbitrary",)),
        name="combine",
    )(dest_flat, dest_flat, gate, x1, mod3, ybuf)


def kernel(x, c, positions, w_ada, b_ada, norm1_gain, w_in, lower_bound_logits, hg_norm_gain, w_hg_branch,
           q_norm_gain, k_norm_gain, attn_sinks, w_attn_branch, w_out, norm2_gain, w_router, b_router,
           w_up, b_up, w_down, b_down):
    B, S, D = x.shape
    T = B * S
    assert w_ada.shape[0] == 1, "one layer"
    x2 = x.reshape(T, D)

    mod = _ada(c, w_ada[0], b_ada[0])
    mod3 = mod.reshape(B, 6, D)

    hg_in, at_in, gates, f_min = _inproj(x2, mod3, norm1_gain[0], w_in[0].astype(BF16), S)
    f_min = f_min[::V7X_SUBLANES, 0].reshape(B, S // TM_PROJ)
    hg_o = _hgrn(hg_in, f_min, lower_bound_logits, hg_norm_gain[0], B, S)
    at_o = _swa(at_in, positions, q_norm_gain[0], k_norm_gain[0], attn_sinks[0], B, S)

    x1, h2, idx, gate, rank, cnt = _merge_router(
        x2, hg_o, at_o, gates, mod3, w_hg_branch[0].astype(BF16), w_attn_branch[0].astype(BF16),
        w_out[0].astype(BF16), norm2_gain[0], w_router[0].astype(BF16), b_router[0], S)

    bm = MOE_BM
    counts = cnt[0].astype(I32)
    padded = (counts + bm - 1) // bm * bm
    pad_end = jnp.cumsum(padded)
    pad_start = pad_end - padded
    P = T * TOP_K + N_EXPERTS * bm
    n_blocks = P // bm
    block_start = jnp.arange(n_blocks, dtype=I32) * bm
    block_expert = jnp.minimum(
        jnp.sum((pad_end[None, :] <= block_start[:, None]).astype(I32), axis=1), N_EXPERTS - 1).astype(I32)
    n_used = (pad_end[-1:] // bm).astype(I32)
    tail = jnp.concatenate([n_used, n_blocks - n_used]).astype(I32)

    dest = _dest(idx, rank, pad_start)
    dest_flat = dest.reshape(T * TOP_K)
    xbuf = _dispatch((pad_start + counts).astype(I32), (padded - counts).astype(I32), tail, dest_flat, h2, P)

    ybuf = _ffn(block_expert, n_used, xbuf, w_up[0],
                b_up[0][:, None, 0::2], b_up[0][:, None, 1::2],
                w_down[0], b_down[0][:, None, :])

    out = _combine(dest_flat, gate, x1, mod3, ybuf, S)
    return out.reshape(B, S, D)
```
